```python
import jax, jax.numpy as jnp
from jax import lax
import numpy as np

D_MODEL = 1024
BATCH = 2
SEQ = 8192
DEPTH = 4

D_MIX = D_MODEL
D_GM = D_MIX // 4
D_RG = D_MIX // 2
D_FT = D_MIX - D_GM - D_RG
GM_HEADS = 4
GM_HEAD_DIM = D_GM // GM_HEADS
GM_CHUNK = 128
RG_HEADS = 8
RG_HEAD_DIM = D_RG // RG_HEADS
RG_CONV = 4
RG_CONV_LEFT = 2
RG_C = 8.0
FT_GROUPS = 4
FT_GROUP_DIM = D_FT // FT_GROUPS
D_FF = 4 * D_MODEL
D_IN = 2 * D_GM + 2 * D_RG + D_FT
SPLITS = (D_GM, 2 * D_GM, 2 * D_GM + D_RG, 2 * D_GM + 2 * D_RG)
EPS = 1e-6

kernel_name = "hybrid_gmlp_rglru_fnet_encoder"


def rms_norm(x, g):
    xf = x.astype(jnp.float32)
    y = xf * lax.rsqrt(jnp.mean(xf * xf, axis=-1, keepdims=True) + EPS)
    return (y * g.astype(jnp.float32)).astype(x.dtype)


def layer_norm(x, g, b):
    xf = x.astype(jnp.float32)
    mu = jnp.mean(xf, axis=-1, keepdims=True)
    var = jnp.mean(jnp.square(xf - mu), axis=-1, keepdims=True)
    y = (xf - mu) * lax.rsqrt(var + EPS)
    return (y * g.astype(jnp.float32) + b.astype(jnp.float32)).astype(x.dtype)


def spatial_gating(u, v, ln_g, ln_b, w_s, b_s):
    bsz, seq, _ = v.shape
    v = layer_norm(v, ln_g, ln_b).reshape(bsz, seq // GM_CHUNK, GM_CHUNK, GM_HEADS, GM_HEAD_DIM)
    s = jnp.einsum("hpq,bnqhd->bnphd", w_s, v) + b_s.T[:, :, None]
    return u * s.reshape(bsz, seq, D_GM)


def centred_depthwise_conv(x, w, b):
    seq = x.shape[1]
    xp = jnp.pad(x, ((0, 0), (RG_CONV_LEFT, RG_CONV - 1 - RG_CONV_LEFT), (0, 0)))
    y = b
    for k in range(RG_CONV):
        y = y + xp[:, k:k + seq] * w[k]
    return y


def _linear_recurrence(c1, c2):
    a1, b1 = c1
    a2, b2 = c2
    return a1 * a2, a2 * b1 + b2


def rg_lru(x, w_a, b_a, w_x, b_x, lam):
    bsz, seq, _ = x.shape
    f32 = jnp.float32
    xh = x.astype(f32).reshape(bsz, seq, RG_HEADS, RG_HEAD_DIM)
    r = jax.nn.sigmoid(jnp.einsum("bshi,hio->bsho", xh, w_a.astype(f32)) + b_a.astype(f32))
    i = jax.nn.sigmoid(jnp.einsum("bshi,hio->bsho", xh, w_x.astype(f32)) + b_x.astype(f32))
    log_a = -RG_C * r * jax.nn.softplus(-lam.astype(f32))
    a = jnp.exp(log_a)
    inp = jnp.sqrt(-jnp.expm1(2.0 * log_a)) * (i * xh)
    _, h = lax.associative_scan(_linear_recurrence, (a, inp), axis=1)
    return h.reshape(bsz, seq, D_RG).astype(x.dtype)


def bidirectional_rg_lru(x, w_a, b_a, w_x, b_x, lam):
    h_fwd = rg_lru(x, w_a[0], b_a[0], w_x[0], b_x[0], lam[0])
    h_bwd = jnp.flip(rg_lru(jnp.flip(x, axis=1), w_a[1], b_a[1], w_x[1], b_x[1], lam[1]), axis=1)
    return h_fwd + h_bwd


def fourier_mix(x, w_f, b_f):
    bsz, seq, _ = x.shape
    f32 = jnp.float32
    xg = x.astype(f32).reshape(bsz, seq, FT_GROUPS, FT_GROUP_DIM)
    y = jnp.fft.fftn(xg, axes=(1, 3), norm="ortho").real
    y = jnp.einsum("bsgi,gio->bsgo", y, w_f.astype(f32)) + b_f.astype(f32)
    return y.reshape(bsz, seq, D_FT).astype(x.dtype)


def setup_inputs(seed: int = 0) -> dict:
    key = jax.random.key(seed)
    ks = jax.random.split(key, 26)
    f32 = jnp.float32

    def nrm(k, shape, scale):
        return jax.random.normal(k, shape, f32) * scale

    def gain(k, shape):
        return 1.0 + 0.05 * jax.random.normal(k, shape, f32)

    u = jax.random.uniform(ks[17], (DEPTH, 2, RG_HEADS, RG_HEAD_DIM), f32, minval=0.9, maxval=0.999)
    a_base = u ** (1.0 / RG_C)
    rg_lam = jnp.log(a_base) - jnp.log1p(-a_base)

    return {
        "x": nrm(ks[0], (BATCH, SEQ, D_MODEL), 1.0),
        "c": nrm(ks[1], (BATCH, D_MODEL), 1.0),
        "w_ada": nrm(ks[2], (DEPTH, D_MODEL, 6 * D_MODEL), 0.5 * D_MODEL ** -0.5),
        "b_ada": nrm(ks[3], (DEPTH, 6 * D_MODEL), 0.01),
        "g_pre_mix": gain(ks[4], (DEPTH, D_MODEL)),
        "g_post_mix": gain(ks[5], (DEPTH, D_MODEL)),
        "w_in": nrm(ks[6], (DEPTH, D_MODEL, D_IN), D_MODEL ** -0.5),
        "gm_ln_g": gain(ks[7], (DEPTH, D_GM)),
        "gm_ln_b": nrm(ks[8], (DEPTH, D_GM), 0.01),
        "gm_w_s": nrm(ks[9], (DEPTH, GM_HEADS, GM_CHUNK, GM_CHUNK), GM_CHUNK ** -0.5),
        "gm_b_s": gain(ks[10], (DEPTH, GM_HEADS, GM_CHUNK)),
        "rg_conv_w": nrm(ks[11], (DEPTH, RG_CONV, D_RG), RG_CONV ** -0.5),
        "rg_conv_b": nrm(ks[12], (DEPTH, D_RG), 0.01),
        "rg_w_a": nrm(ks[13], (DEPTH, 2, RG_HEADS, RG_HEAD_DIM, RG_HEAD_DIM), RG_HEAD_DIM ** -0.5),
        "rg_b_a": nrm(ks[14], (DEPTH, 2, RG_HEADS, RG_HEAD_DIM), 0.01),
        "rg_w_x": nrm(ks[15], (DEPTH, 2, RG_HEADS, RG_HEAD_DIM, RG_HEAD_DIM), RG_HEAD_DIM ** -0.5),
        "rg_b_x": nrm(ks[16], (DEPTH, 2, RG_HEADS, RG_HEAD_DIM), 0.01),
        "rg_lam": rg_lam,
        "ft_w": nrm(ks[18], (DEPTH, FT_GROUPS, FT_GROUP_DIM, FT_GROUP_DIM), FT_GROUP_DIM ** -0.5),
        "ft_b": nrm(ks[19], (DEPTH, FT_GROUPS, FT_GROUP_DIM), 0.01),
        "g_mix_out": gain(ks[20], (DEPTH, D_MIX)),
        "w_out": nrm(ks[21], (DEPTH, D_MIX, D_MODEL), D_MIX ** -0.5),
        "g_pre_ff": gain(ks[22], (DEPTH, D_MODEL)),
        "g_post_ff": gain(ks[23], (DEPTH, D_MODEL)),
        "w_ff1": nrm(ks[24], (DEPTH, D_MODEL, D_FF), D_MODEL ** -0.5),
        "w_ff2": nrm(ks[25], (DEPTH, D_FF, D_MODEL), D_FF ** -0.5),
    }


def reference(x, c, w_ada, b_ada, g_pre_mix, g_post_mix, w_in, gm_ln_g, gm_ln_b, gm_w_s, gm_b_s,
              rg_conv_w, rg_conv_b, rg_w_a, rg_b_a, rg_w_x, rg_b_x, rg_lam, ft_w, ft_b,
              g_mix_out, w_out, g_pre_ff, g_post_ff, w_ff1, w_ff2):
    cond = jax.nn.silu(c)
    for l in range(DEPTH):
        mod = (cond @ w_ada[l] + b_ada[l])[:, None, :]
        sh1, sc1, gt1, sh2, sc2, gt2 = jnp.split(mod, 6, axis=-1)

        h = rms_norm(x, g_pre_mix[l]) * (1.0 + sc1) + sh1
        z = h @ w_in[l]
        u, v, rg_g, rg_x, ft_x = jnp.split(z, SPLITS, axis=-1)
        y_gm = spatial_gating(jax.nn.gelu(u), jax.nn.gelu(v), gm_ln_g[l], gm_ln_b[l], gm_w_s[l], gm_b_s[l])
        xr = centred_depthwise_conv(rg_x, rg_conv_w[l], rg_conv_b[l])
        y_rg = bidirectional_rg_lru(xr, rg_w_a[l], rg_b_a[l], rg_w_x[l], rg_b_x[l], rg_lam[l]) * jax.nn.gelu(rg_g)
        y_ft = fourier_mix(ft_x, ft_w[l], ft_b[l])
        gm = g_mix_out[l]
        y = jnp.concatenate([
            rms_norm(y_gm, gm[:D_GM]),
            rms_norm(y_rg, gm[D_GM:D_GM + D_RG]),
            rms_norm(y_ft, gm[D_GM + D_RG:]),
        ], axis=-1)
        x = x + gt1 * rms_norm(y @ w_out[l], g_post_mix[l])

        h = rms_norm(x, g_pre_ff[l]) * (1.0 + sc2) + sh2
        f = jnp.square(jax.nn.relu(h @ w_ff1[l])) @ w_ff2[l]
        x = x + gt2 * rms_norm(f, g_post_ff[l])
    return x
```

```python
import functools
import math

import numpy as np
import jax
import jax.numpy as jnp
from jax import lax
from jax.experimental import pallas as pl
from jax.experimental.pallas import tpu as pltpu

F32 = jnp.float32
BF16 = jnp.bfloat16

EPS = 1e-6
RG_C = 8.0
GM_HEADS = 4
GM_CHUNK = 128
RG_HEADS = 8
FT_GROUPS = 4
FT_GROUP_DIM = 64

V7X_SUBLANES = 8
V7X_LANES = 128
V7X_VMEM_LIMIT_BYTES = 56 * 1024 * 1024

ROW_TILE = 512
SCAN_SEGS = V7X_SUBLANES
SCAN_SEG_LEN = ROW_TILE // SCAN_SEGS
SCAN_SEG_STRIDE = SCAN_SEG_LEN + V7X_SUBLANES

FT_N1 = 64
FT_LANES = 128


def _gelu(x):
    return 0.5 * x * (1.0 + jnp.tanh(math.sqrt(2.0 / math.pi) * (x + 0.044715 * (x * x * x))))


def _sigmoid(x):
    return 0.5 * jnp.tanh(0.5 * x) + 0.5


def _rms(x, g):
    return x * lax.rsqrt(jnp.mean(x * x, axis=-1, keepdims=True) + EPS) * g


def _dot(a, b):
    return jnp.dot(a, b, preferred_element_type=F32)


def _params(*sem):
    return pltpu.CompilerParams(dimension_semantics=sem, vmem_limit_bytes=V7X_VMEM_LIMIT_BYTES)


def _mod_kernel(c_ref, w_ref, b_ref, o_ref):
    c = c_ref[...]
    cond = c * _sigmoid(c)
    o_ref[0] = jnp.dot(cond, w_ref[0], preferred_element_type=F32,
                       precision=lax.Precision.HIGHEST) + b_ref[0]


def _modulation(c, w_ada, b_ada):
    depth, d, d6 = w_ada.shape
    bsz = c.shape[0]
    nblk = d6 // d
    return pl.pallas_call(
        _mod_kernel,
        grid=(depth, nblk),
        in_specs=[
            pl.BlockSpec((bsz, d), lambda l, j: (0, 0)),
            pl.BlockSpec((1, d, d), lambda l, j: (l, 0, j)),
            pl.BlockSpec((1, 1, d), lambda l, j: (l, 0, j)),
        ],
        out_specs=pl.BlockSpec((1, bsz, d), lambda l, j: (l, 0, j)),
        out_shape=jax.ShapeDtypeStruct((depth, bsz, d6), F32),
        compiler_params=_params("arbitrary", "arbitrary"),
    )(c, w_ada, b_ada.reshape(depth, 1, d6))


def _mixer_in_kernel(x_ref, m_ref, gpre_ref, win_ref, lng_ref, lnb_ref, ws_ref, bs_ref, gmix_ref,
                     ygm_ref, gact_ref, rgx_ref, ftx_ref, *, d_gm, d_rg):
    x = x_ref[0]
    sh1 = m_ref[0, 0, 0:1, :]
    sc1 = m_ref[0, 0, 1:2, :]
    h = (_rms(x, gpre_ref[0]) * (1.0 + sc1) + sh1).astype(BF16)

    o_v, o_g, o_x, o_f = d_gm, 2 * d_gm, 2 * d_gm + d_rg, 2 * d_gm + 2 * d_rg
    u = _gelu(_dot(h, win_ref[0, :, 0:o_v]))
    v = _gelu(_dot(h, win_ref[0, :, o_v:o_g]))
    gact_ref[0] = _gelu(_dot(h, win_ref[0, :, o_g:o_x]))
    rgx_ref[0] = _dot(h, win_ref[0, :, o_x:o_f])
    ftx_ref[0] = _dot(h, win_ref[0, :, o_f:o_f + d_gm])

    mu = jnp.mean(v, axis=-1, keepdims=True)
    vc = v - mu
    var = jnp.mean(vc * vc, axis=-1, keepdims=True)
    vn = (vc * lax.rsqrt(var + EPS) * lng_ref[0] + lnb_ref[0]).astype(BF16)

    head_dim = d_gm // GM_HEADS
    head = lax.broadcasted_iota(jnp.int32, (GM_CHUNK, d_gm), 1) // head_dim
    for c in range(x.shape[0] // GM_CHUNK):
        rows = slice(c * GM_CHUNK, (c + 1) * GM_CHUNK)
        r = _dot(ws_ref[0], vn[rows])
        s = r[0:GM_CHUNK]
        for hh in range(1, GM_HEADS):
            s = jnp.where(head == hh, r[hh * GM_CHUNK:(hh + 1) * GM_CHUNK], s)
        y = u[rows] * (s + bs_ref[0])
        ygm_ref[0, rows, :] = _rms(y, gmix_ref[0, :, 0:d_gm])


def _mixer_in(l, x, mod4, g_pre, w_in, ln_g, ln_b, ws, bs, g_mix, *, d_gm, d_rg):
    bsz, seq, d = x.shape
    d_in = w_in.shape[-1]
    nt = seq // ROW_TILE
    lay = lambda b, i: (l, 0, 0)
    row = lambda b, i: (b, i, 0)
    return pl.pallas_call(
        functools.partial(_mixer_in_kernel, d_gm=d_gm, d_rg=d_rg),
        grid=(bsz, nt),
        in_specs=[
            pl.BlockSpec((1, ROW_TILE, d), row),
            pl.BlockSpec((1, 1, 6, d), lambda b, i: (l, b, 0, 0)),
            pl.BlockSpec((1, 1, d), lay),
            pl.BlockSpec((1, d, d_in), lay),
            pl.BlockSpec((1, 1, d_gm), lay),
            pl.BlockSpec((1, 1, d_gm), lay),
            pl.BlockSpec((1, GM_HEADS * GM_CHUNK, GM_CHUNK), lay),
            pl.BlockSpec((1, GM_CHUNK, d_gm), lay),
            pl.BlockSpec((1, 1, d), lay),
        ],
        out_specs=[
            pl.BlockSpec((1, ROW_TILE, d_gm), row),
            pl.BlockSpec((1, ROW_TILE, d_rg), row),
            pl.BlockSpec((1, ROW_TILE, d_rg), row),
            pl.BlockSpec((1, ROW_TILE, d_gm), row),
        ],
        out_shape=[
            jax.ShapeDtypeStruct((bsz, seq, d_gm), F32),
            jax.ShapeDtypeStruct((bsz, seq, d_rg), F32),
            jax.ShapeDtypeStruct((bsz, seq, d_rg), F32),
            jax.ShapeDtypeStruct((bsz, seq, d_gm), F32),
        ],
        compiler_params=_params("arbitrary", "arbitrary"),
    )(x, mod4, g_pre, w_in, ln_g, ln_b, ws, bs, g_mix)


def _fourier_tables(seq):
    n1 = FT_N1
    n2 = seq // n1
    gd = FT_GROUP_DIM
    j = np.arange(gd)
    ang = 2.0 * np.pi * np.outer(j, j) / gd
    eye = np.eye(FT_LANES // gd)
    w1 = np.concatenate([np.kron(eye, np.cos(ang)), -np.kron(eye, np.sin(ang))], axis=1)
    k2 = np.arange(n2)[None, :, None]
    s2 = np.arange(n2)[None, None, :]
    s1 = np.arange(n1)[:, None, None]
    ang_a = 2.0 * np.pi * ((k2 * (n1 * s2 + s1)) % seq) / seq
    ta = np.concatenate([np.cos(ang_a), np.sin(ang_a)], axis=1)
    i1 = np.arange(n1)
    ang_b = 2.0 * np.pi * np.outer(i1, i1) / n1
    tb = np.concatenate([np.cos(ang_b), np.sin(ang_b)], axis=0)
    return (jnp.asarray(w1, F32), jnp.asarray(ta, F32), jnp.asarray(tb, F32))


def _fourier_kernel(x_ref, w1_ref, ta_ref, tb_ref, o_ref, p_ref, *, seq, scale):
    n1 = FT_N1
    n2 = seq // n1
    ln = FT_LANES

    for c in range(seq // ROW_TILE):
        rows = slice(c * ROW_TILE, (c + 1) * ROW_TILE)
        p = _dot(x_ref[0, rows, :].astype(BF16), w1_ref[...].astype(BF16))
        p_ref[0, rows, :] = p[:, 0:ln]
        p_ref[1, rows, :] = p[:, ln:2 * ln]

    def stage_a(s1, carry):
        rows = pl.ds(s1, n2, stride=n1)
        g = jnp.concatenate([p_ref[0, rows, :], p_ref[1, rows, :]], axis=1).astype(BF16)
        q = _dot(ta_ref[s1].astype(BF16), g)
        p_ref[0, rows, :] = q[0:n2, 0:ln] + q[n2:2 * n2, ln:2 * ln]
        p_ref[1, rows, :] = q[0:n2, ln:2 * ln] - q[n2:2 * n2, 0:ln]
        return carry

    lax.fori_loop(0, n1, stage_a, 0)

    def stage_b(k2, carry):
        rows = pl.ds(pl.multiple_of(k2 * n1, n1), n1)
        blk = jnp.concatenate([p_ref[0, rows, :], p_ref[1, rows, :]], axis=1).astype(BF16)
        q = _dot(tb_ref[...].astype(BF16), blk)
        y = q[0:n1, 0:ln] + q[n1:2 * n1, ln:2 * ln]
        o_ref[0, pl.ds(k2, n1, stride=n2), :] = y * scale
        return carry

    lax.fori_loop(0, n2, stage_b, 0)


def _fourier(ftx, tables):
    bsz, seq, d_ft = ftx.shape
    w1, ta, tb = tables
    n1 = FT_N1
    n2 = seq // n1
    blk = pl.BlockSpec((1, seq, FT_LANES), lambda b, j: (b, 0, j))
    return pl.pallas_call(
        functools.partial(_fourier_kernel, seq=seq, scale=1.0 / math.sqrt(seq * FT_GROUP_DIM)),
        grid=(bsz, d_ft // FT_LANES),
        in_specs=[
            blk,
            pl.BlockSpec((FT_LANES, 2 * FT_LANES), lambda b, j: (0, 0)),
            pl.BlockSpec((n1, 2 * n2, n2), lambda b, j: (0, 0, 0), pipeline_mode=pl.Buffered(1)),
            pl.BlockSpec((2 * n1, n1), lambda b, j: (0, 0)),
        ],
        out_specs=blk,
        out_shape=jax.ShapeDtypeStruct((bsz, seq, d_ft), F32),
        scratch_shapes=[pltpu.VMEM((2, seq, FT_LANES), F32)],
        compiler_params=_params("arbitrary", "arbitrary"),
    )(ftx, w1, ta, tb)


def _rg_scan_kernel(*refs, reverse, nt):
    if reverse:
        (prev_ref, cur_ref, next_ref, cw_ref, cb_ref, wg_ref, ba_ref, bx_ref, lam_ref, hf_ref,
         o_ref, ext_ref, a_ref, b_ref, h_ref, e_ref, p_ref, c_ref, carry_ref) = refs
    else:
        (prev_ref, cur_ref, next_ref, cw_ref, cb_ref, wg_ref, ba_ref, bx_ref, lam_ref,
         o_ref, ext_ref, a_ref, b_ref, h_ref, e_ref, p_ref, c_ref, carry_ref) = refs
        hf_ref = None
    tm = ROW_TILE
    halo = V7X_SUBLANES
    d_rg = cur_ref.shape[-1]
    half = d_rg // 2
    step = pl.program_id(1)
    ti = (nt - 1 - step) if reverse else step

    @pl.when(step == 0)
    def _():
        carry_ref[...] = jnp.zeros_like(carry_ref)

    ext_ref[0:halo, :] = jnp.where(ti == 0, 0.0, prev_ref[0])
    ext_ref[halo:halo + tm, :] = cur_ref[0]
    ext_ref[halo + tm:halo + tm + halo, :] = jnp.where(ti == nt - 1, 0.0, next_ref[0])
    xr = cb_ref[0]
    for k in range(4):
        xr = xr + ext_ref[halo - 2 + k:halo - 2 + k + tm, :] * cw_ref[0, 0, k:k + 1, :]
    xrb = xr.astype(BF16)

    nl = -lam_ref[0, 0]
    c8 = -RG_C * (jnp.maximum(nl, 0.0) + jnp.log1p(jnp.exp(-jnp.abs(nl))))

    for hh in range(2):
        cols = slice(hh * half, (hh + 1) * half)
        pre = _dot(xrb[:, cols], wg_ref[0, 0, hh])
        r = _sigmoid(pre[:, 0:half] + ba_ref[0, 0, :, cols])
        i = _sigmoid(pre[:, half:2 * half] + bx_ref[0, 0, :, cols])
        log_a = c8[:, cols] * r
        a = jnp.exp(log_a)
        th = jnp.tanh(log_a)
        bq = jnp.sqrt(-2.0 * th / (1.0 - th)) * (i * xr[:, cols])
        for kk in range(half // V7X_LANES):
            k = hh * (half // V7X_LANES) + kk
            lanes = slice(kk * V7X_LANES, (kk + 1) * V7X_LANES)
            for j in range(SCAN_SEGS):
                src = slice(j * SCAN_SEG_LEN, (j + 1) * SCAN_SEG_LEN)
                dst = slice(j * SCAN_SEG_STRIDE, j * SCAN_SEG_STRIDE + SCAN_SEG_LEN)
                a_ref[k, dst, :] = a[src, lanes]
                b_ref[k, dst, :] = bq[src, lanes]

    nblk = d_rg // V7X_LANES

    def seg_rows(t):
        tt = (SCAN_SEG_LEN - 1 - t) if reverse else t
        return pl.ds(tt, SCAN_SEGS, stride=SCAN_SEG_STRIDE)

    def sweep1(t, hp):
        rows = seg_rows(t)
        out = []
        for k in range(nblk):
            at = a_ref[k, rows, :]
            out.append((at * hp[k][0] + b_ref[k, rows, :], at * hp[k][1]))
        return tuple(out)

    zero = jnp.zeros((SCAN_SEGS, V7X_LANES), F32)
    ep = lax.fori_loop(0, SCAN_SEG_LEN, sweep1, tuple((zero, zero + 1.0) for _ in range(nblk)))
    for k in range(nblk):
        lanes = slice(k * V7X_LANES, (k + 1) * V7X_LANES)
        e_ref[:, lanes] = ep[k][0]
        p_ref[:, lanes] = ep[k][1]

    c = carry_ref[...]
    order = range(SCAN_SEGS - 1, -1, -1) if reverse else range(SCAN_SEGS)
    for j in order:
        c_ref[j:j + 1, :] = c
        c = p_ref[j:j + 1, :] * c + e_ref[j:j + 1, :]
    carry_ref[...] = c

    def sweep2(t, hs):
        rows = seg_rows(t)
        out = []
        for k in range(nblk):
            h = a_ref[k, rows, :] * hs[k] + b_ref[k, rows, :]
            h_ref[k, rows, :] = h
            out.append(h)
        return tuple(out)

    lax.fori_loop(0, SCAN_SEG_LEN, sweep2,
                  tuple(c_ref[:, k * V7X_LANES:(k + 1) * V7X_LANES] for k in range(nblk)))

    for k in range(nblk):
        lanes = slice(k * V7X_LANES, (k + 1) * V7X_LANES)
        for j in range(SCAN_SEGS):
            src = slice(j * SCAN_SEG_STRIDE, j * SCAN_SEG_STRIDE + SCAN_SEG_LEN)
            dst = slice(j * SCAN_SEG_LEN, (j + 1) * SCAN_SEG_LEN)
            if reverse:
                o_ref[0, dst, lanes] = h_ref[k, src, :] + hf_ref[0, dst, lanes]
            else:
                o_ref[0, dst, lanes] = h_ref[k, src, :]


def _rg_scan(l, rgx, conv_w, conv_b, wg, ba, bx, lam, h_fwd, *, reverse):
    bsz, seq, d_rg = rgx.shape
    nt = seq // ROW_TILE
    halo = V7X_SUBLANES
    per = ROW_TILE // halo
    nh = seq // halo
    dirn = 1 if reverse else 0

    def tile(i):
        return (nt - 1 - i) if reverse else i

    row = lambda b, i: (b, tile(i), 0)
    lay = lambda b, i: (l, 0, 0)
    ldir = lambda b, i: (l, dirn, 0, 0)
    in_specs = [
        pl.BlockSpec((1, halo, d_rg), lambda b, i: (b, jnp.maximum(tile(i) * per - 1, 0), 0)),
        pl.BlockSpec((1, ROW_TILE, d_rg), row),
        pl.BlockSpec((1, halo, d_rg), lambda b, i: (b, jnp.minimum((tile(i) + 1) * per, nh - 1), 0)),
        pl.BlockSpec((1, 1, 4, d_rg), lambda b, i: (l, 0, 0, 0)),
        pl.BlockSpec((1, 1, d_rg), lay),
        pl.BlockSpec((1, 1, 2, d_rg // 2, d_rg), lambda b, i: (l, dirn, 0, 0, 0)),
        pl.BlockSpec((1, 1, 1, d_rg), ldir),
        pl.BlockSpec((1, 1, 1, d_rg), ldir),
        pl.BlockSpec((1, 1, 1, d_rg), ldir),
    ]
    args = [rgx, rgx, rgx, conv_w, conv_b, wg, ba, bx, lam]
    if reverse:
        in_specs.append(pl.BlockSpec((1, ROW_TILE, d_rg), row))
        args.append(h_fwd)
    pad_rows = SCAN_SEGS * SCAN_SEG_STRIDE
    return pl.pallas_call(
        functools.partial(_rg_scan_kernel, reverse=reverse, nt=nt),
        grid=(bsz, nt),
        in_specs=in_specs,
        out_specs=pl.BlockSpec((1, ROW_TILE, d_rg), row),
        out_shape=jax.ShapeDtypeStruct((bsz, seq, d_rg), F32),
        scratch_shapes=[
            pltpu.VMEM((ROW_TILE + 2 * halo, d_rg), F32),
            pltpu.VMEM((d_rg // V7X_LANES, pad_rows, V7X_LANES), F32),
            pltpu.VMEM((d_rg // V7X_LANES, pad_rows, V7X_LANES), F32),
            pltpu.VMEM((d_rg // V7X_LANES, pad_rows, V7X_LANES), F32),
            pltpu.VMEM((SCAN_SEGS, d_rg), F32),
            pltpu.VMEM((SCAN_SEGS, d_rg), F32),
            pltpu.VMEM((SCAN_SEGS, d_rg), F32),
            pltpu.VMEM((1, d_rg), F32),
        ],
        compiler_params=_params("arbitrary", "arbitrary"),
    )(*args)


def _combine_kernel(x_ref, ygm_ref, gact_ref, h_ref, yft_ref, m_ref, gmix_ref, ftw_ref, ftb_ref,
                    wout_ref, gpost_ref, o_ref, *, d_gm, d_rg):
    gt1 = m_ref[0, 0, 2:3, :]
    o_ft = d_gm + d_rg
    d_mix = wout_ref.shape[1]
    yrg = _rms(h_ref[0] * gact_ref[0], gmix_ref[0, :, d_gm:o_ft])
    yft = _dot(yft_ref[0].astype(BF16), ftw_ref[0]) + ftb_ref[0]
    yft = _rms(yft, gmix_ref[0, :, o_ft:d_mix])
    o = _dot(ygm_ref[0].astype(BF16), wout_ref[0, 0:d_gm, :])
    o = o + _dot(yrg.astype(BF16), wout_ref[0, d_gm:o_ft, :])
    o = o + _dot(yft.astype(BF16), wout_ref[0, o_ft:d_mix, :])
    o_ref[0] = x_ref[0] + gt1 * _rms(o, gpost_ref[0])


def _combine(l, x, ygm, gact, h, yft, mod4, g_mix, ftw, ftb, w_out, g_post):
    bsz, seq, d = x.shape
    d_gm = ygm.shape[-1]
    d_rg = h.shape[-1]
    d_ft = yft.shape[-1]
    d_mix = w_out.shape[1]
    nt = seq // ROW_TILE
    lay = lambda b, i: (l, 0, 0)
    row = lambda b, i: (b, i, 0)
    return pl.pallas_call(
        functools.partial(_combine_kernel, d_gm=d_gm, d_rg=d_rg),
        grid=(bsz, nt),
        in_specs=[
            pl.BlockSpec((1, ROW_TILE, d), row),
            pl.BlockSpec((1, ROW_TILE, d_gm), row),
            pl.BlockSpec((1, ROW_TILE, d_rg), row),
            pl.BlockSpec((1, ROW_TILE, d_rg), row),
            pl.BlockSpec((1, ROW_TILE, d_ft), row),
            pl.BlockSpec((1, 1, 6, d), lambda b, i: (l, b, 0, 0)),
            pl.BlockSpec((1, 1, d_mix), lay),
            pl.BlockSpec((1, d_ft, d_ft), lay),
            pl.BlockSpec((1, 1, d_ft), lay),
            pl.BlockSpec((1, d_mix, d), lay),
            pl.BlockSpec((1, 1, d), lay),
        ],
        out_specs=pl.BlockSpec((1, ROW_TILE, d), row),
        out_shape=jax.ShapeDtypeStruct((bsz, seq, d), F32),
        compiler_params=_params("arbitrary", "arbitrary"),
    )(x, ygm, gact, h, yft, mod4, g_mix, ftw, ftb, w_out, g_post)


def _ffn_kernel(x_ref, m_ref, gpre_ref, w1_ref, w2_ref, gpost_ref, o_ref, *, ff_chunk):
    x = x_ref[0]
    sh2 = m_ref[0, 0, 3:4, :]
    sc2 = m_ref[0, 0, 4:5, :]
    gt2 = m_ref[0, 0, 5:6, :]
    h = (_rms(x, gpre_ref[0]) * (1.0 + sc2) + sh2).astype(BF16)
    d_ff = w1_ref.shape[-1]
    acc = None
    for c in range(d_ff // ff_chunk):
        cols = slice(c * ff_chunk, (c + 1) * ff_chunk)
        a = jnp.maximum(_dot(h, w1_ref[0, :, cols]), 0.0)
        part = _dot((a * a).astype(BF16), w2_ref[0, cols, :])
        acc = part if acc is None else acc + part
    o_ref[0] = x + gt2 * _rms(acc, gpost_ref[0])


def _ffn(l, x, mod4, g_pre, w1, w2, g_post):
    bsz, seq, d = x.shape
    d_ff = w1.shape[-1]
    nt = seq // ROW_TILE
    lay = lambda b, i: (l, 0, 0)
    row = lambda b, i: (b, i, 0)
    return pl.pallas_call(
        functools.partial(_ffn_kernel, ff_chunk=d),
        grid=(bsz, nt),
        in_specs=[
            pl.BlockSpec((1, ROW_TILE, d), row),
            pl.BlockSpec((1, 1, 6, d), lambda b, i: (l, b, 0, 0)),
            pl.BlockSpec((1, 1, d), lay),
            pl.BlockSpec((1, d, d_ff), lay, pipeline_mode=pl.Buffered(1)),
            pl.BlockSpec((1, d_ff, d), lay, pipeline_mode=pl.Buffered(1)),
            pl.BlockSpec((1, 1, d), lay),
        ],
        out_specs=pl.BlockSpec((1, ROW_TILE, d), row),
        out_shape=jax.ShapeDtypeStruct((bsz, seq, d), F32),
        compiler_params=_params("arbitrary", "arbitrary"),
    )(x, mod4, g_pre, w1, w2, g_post)


def _block_diag(w, per):
    *lead, n, hd, _ = w.shape
    w = w.reshape(*lead, n // per, per, hd, hd)
    eye = jnp.eye(per, dtype=w.dtype)
    bd = jnp.einsum("...aio,ab->...aibo", w, eye)
    return bd.reshape(*lead, n // per, per * hd, per * hd)


def kernel(x, c, w_ada, b_ada, g_pre_mix, g_post_mix, w_in, gm_ln_g, gm_ln_b, gm_w_s, gm_b_s,
           rg_conv_w, rg_conv_b, rg_w_a, rg_b_a, rg_w_x, rg_b_x, rg_lam, ft_w, ft_b,
           g_mix_out, w_out, g_pre_ff, g_post_ff, w_ff1, w_ff2):
    bsz, seq, d = x.shape
    depth = w_in.shape[0]
    d_gm = gm_ln_g.shape[-1]
    d_rg = rg_conv_b.shape[-1]
    d_ft = ft_w.shape[1] * ft_w.shape[2]
    assert seq % ROW_TILE == 0 and ROW_TILE % GM_CHUNK == 0 and seq % FT_N1 == 0
    assert d_ft % FT_LANES == 0 and ft_w.shape[2] == FT_GROUP_DIM

    vec = lambda a: a.reshape(depth, 1, a.shape[-1])
    w_in_b = w_in.astype(BF16)
    w_out_b = w_out.astype(BF16)
    w_ff1_b = w_ff1.astype(BF16)
    w_ff2_b = w_ff2.astype(BF16)
    ws = gm_w_s.reshape(depth, GM_HEADS * GM_CHUNK, GM_CHUNK).astype(BF16)
    bs = jnp.repeat(jnp.swapaxes(gm_b_s, 1, 2), d_gm // GM_HEADS, axis=2)
    per = RG_HEADS // 2
    wg = jnp.concatenate([_block_diag(rg_w_a, per), _block_diag(rg_w_x, per)], axis=-1).astype(BF16)
    ba = rg_b_a.reshape(depth, 2, 1, d_rg)
    bx = rg_b_x.reshape(depth, 2, 1, d_rg)
    lam = rg_lam.reshape(depth, 2, 1, d_rg)
    conv_w = rg_conv_w.reshape(depth, 1, 4, d_rg)
    ftw = _block_diag(ft_w, FT_GROUPS).reshape(depth, d_ft, d_ft).astype(BF16)
    ftb = ft_b.reshape(depth, 1, d_ft)
    tables = _fourier_tables(seq)

    mod4 = _modulation(c, w_ada, b_ada).reshape(depth, bsz, 6, d)

    for l in range(depth):
        ygm, gact, rgx, ftx = _mixer_in(l, x, mod4, vec(g_pre_mix), w_in_b, vec(gm_ln_g), vec(gm_ln_b),
                                        ws, bs, vec(g_mix_out), d_gm=d_gm, d_rg=d_rg)
        yft = _fourier(ftx, tables)
        scan = functools.partial(_rg_scan, l, rgx, conv_w, vec(rg_conv_b), wg, ba, bx, lam)
        h = scan(scan(None, reverse=False), reverse=True)
        x = _combine(l, x, ygm, gact, h, yft, mod4, vec(g_mix_out), ftw, ftb, w_out_b, vec(g_post_mix))
        x = _ffn(l, x, mod4, vec(g_pre_ff), w_ff1_b, w_ff2_b, vec(g_post_ff))
    return x
```

```python
import functools
import math

import numpy as np
import jax
import jax.numpy as jnp
from jax import lax
from jax.experimental import pallas as pl
from jax.experimental.pallas import tpu as pltpu

F32 = jnp.float32
BF16 = jnp.bfloat16

EPS = 1e-6
RG_C = 8.0
GM_HEADS = 4
GM_CHUNK = 128
RG_HEADS = 8
FT_GROUPS = 4
FT_GROUP_DIM = 64

V7X_SUBLANES = 8
V7X_LANES = 128
V7X_VMEM_LIMIT_BYTES = 56 * 1024 * 1024

ROW_TILE = 512
SCAN_SEGS = V7X_SUBLANES
SCAN_SEG_LEN = ROW_TILE // SCAN_SEGS
SCAN_SEG_STRIDE = SCAN_SEG_LEN + V7X_SUBLANES

FT_N1 = 64
FT_PAD = FT_N1 + V7X_SUBLANES
FT_LANES = 128
FT_UNROLL_A = 2
FT_UNROLL_B = 4


def _gelu(x):
    return 0.5 * x * (1.0 + jnp.tanh(math.sqrt(2.0 / math.pi) * (x + 0.044715 * (x * x * x))))


def _sigmoid(x):
    return 0.5 * jnp.tanh(0.5 * x) + 0.5


def _rms(x, g):
    return x * lax.rsqrt(jnp.mean(x * x, axis=-1, keepdims=True) + EPS) * g


def _dot(a, b):
    return jnp.dot(a, b, preferred_element_type=F32)


def _params(*sem):
    return pltpu.CompilerParams(dimension_semantics=sem, vmem_limit_bytes=V7X_VMEM_LIMIT_BYTES)


def _mod_kernel(ct_ref, w_ref, b_ref, o_ref):
    ct = ct_ref[...]
    cond = ct * _sigmoid(ct)
    w = w_ref[0]
    for b in range(ct.shape[1]):
        o_ref[0, b:b + 1, :] = jnp.sum(w * cond[:, b:b + 1], axis=0, keepdims=True) + b_ref[0]


def _modulation(c, w_ada, b_ada):
    depth, d, d6 = w_ada.shape
    bsz = c.shape[0]
    nblk = d6 // d
    return pl.pallas_call(
        _mod_kernel,
        grid=(depth, nblk),
        in_specs=[
            pl.BlockSpec((d, bsz), lambda l, j: (0, 0)),
            pl.BlockSpec((1, d, d), lambda l, j: (l, 0, j)),
            pl.BlockSpec((1, 1, d), lambda l, j: (l, 0, j)),
        ],
        out_specs=pl.BlockSpec((1, bsz, d), lambda l, j: (l, 0, j)),
        out_shape=jax.ShapeDtypeStruct((depth, bsz, d6), F32),
        compiler_params=_params("arbitrary", "arbitrary"),
        name="modulation",
    )(c.T, w_ada, b_ada.reshape(depth, 1, d6))


def _mixer_in_kernel(x_ref, m_ref, gpre_ref, win_ref, lng_ref, lnb_ref, ws_ref, bs_ref, gmix_ref,
                     ygm_ref, gact_ref, rgx_ref, ftx_ref, *, d_gm, d_rg):
    x = x_ref[0]
    sh1 = m_ref[0, 0, 0:1, :]
    sc1 = m_ref[0, 0, 1:2, :]
    h = (_rms(x, gpre_ref[0]) * (1.0 + sc1) + sh1).astype(BF16)

    o_v, o_g, o_x, o_f = d_gm, 2 * d_gm, 2 * d_gm + d_rg, 2 * d_gm + 2 * d_rg
    u = _gelu(_dot(h, win_ref[0, :, 0:o_v]))
    v = _gelu(_dot(h, win_ref[0, :, o_v:o_g]))
    gact_ref[0] = _gelu(_dot(h, win_ref[0, :, o_g:o_x]))
    rgx_ref[0] = _dot(h, win_ref[0, :, o_x:o_f])
    ftx_ref[0] = _dot(h, win_ref[0, :, o_f:o_f + d_gm])

    mu = jnp.mean(v, axis=-1, keepdims=True)
    vc = v - mu
    var = jnp.mean(vc * vc, axis=-1, keepdims=True)
    vn = (vc * lax.rsqrt(var + EPS) * lng_ref[0] + lnb_ref[0]).astype(BF16)

    head_dim = d_gm // GM_HEADS
    head = lax.broadcasted_iota(jnp.int32, (GM_CHUNK, d_gm), 1) // head_dim
    for c in range(x.shape[0] // GM_CHUNK):
        rows = slice(c * GM_CHUNK, (c + 1) * GM_CHUNK)
        r = _dot(ws_ref[0], vn[rows])
        s = r[0:GM_CHUNK]
        for hh in range(1, GM_HEADS):
            s = jnp.where(head == hh, r[hh * GM_CHUNK:(hh + 1) * GM_CHUNK], s)
        y = u[rows] * (s + bs_ref[0])
        ygm_ref[0, rows, :] = _rms(y, gmix_ref[0, :, 0:d_gm])


def _mixer_in(l, x, mod4, g_pre, w_in, ln_g, ln_b, ws, bs, g_mix, *, d_gm, d_rg):
    bsz, seq, d = x.shape
    d_in = w_in.shape[-1]
    nt = seq // ROW_TILE
    lay = lambda b, i: (l, 0, 0)
    row = lambda b, i: (b, i, 0)
    return pl.pallas_call(
        functools.partial(_mixer_in_kernel, d_gm=d_gm, d_rg=d_rg),
        grid=(bsz, nt),
        in_specs=[
            pl.BlockSpec((1, ROW_TILE, d), row),
            pl.BlockSpec((1, 1, 6, d), lambda b, i: (l, b, 0, 0)),
            pl.BlockSpec((1, 1, d), lay),
            pl.BlockSpec((1, d, d_in), lay),
            pl.BlockSpec((1, 1, d_gm), lay),
            pl.BlockSpec((1, 1, d_gm), lay),
            pl.BlockSpec((1, GM_HEADS * GM_CHUNK, GM_CHUNK), lay),
            pl.BlockSpec((1, GM_CHUNK, d_gm), lay),
            pl.BlockSpec((1, 1, d), lay),
        ],
        out_specs=[
            pl.BlockSpec((1, ROW_TILE, d_gm), row),
            pl.BlockSpec((1, ROW_TILE, d_rg), row),
            pl.BlockSpec((1, ROW_TILE, d_rg), row),
            pl.BlockSpec((1, ROW_TILE, d_gm), row),
        ],
        out_shape=[
            jax.ShapeDtypeStruct((bsz, seq, d_gm), F32),
            jax.ShapeDtypeStruct((bsz, seq, d_rg), F32),
            jax.ShapeDtypeStruct((bsz, seq, d_rg), F32),
            jax.ShapeDtypeStruct((bsz, seq, d_gm), F32),
        ],
        compiler_params=_params("arbitrary", "arbitrary"),
        name="mixer_in",
    )(x, mod4, g_pre, w_in, ln_g, ln_b, ws, bs, g_mix)


def _fourier_tables(seq):
    n1 = FT_N1
    n2 = seq // n1
    gd = FT_GROUP_DIM
    j = np.arange(gd)
    ang = 2.0 * np.pi * np.outer(j, j) / gd
    eye = np.eye(FT_LANES // gd)
    w1 = np.concatenate([np.kron(eye, np.cos(ang)), -np.kron(eye, np.sin(ang))], axis=1)
    k2 = np.arange(n2)[None, :, None]
    s2 = np.arange(n2)[None, None, :]
    s1 = np.arange(n1)[:, None, None]
    ang_a = 2.0 * np.pi * ((k2 * (n1 * s2 + s1)) % seq) / seq
    ta = np.concatenate([np.cos(ang_a), np.sin(ang_a)], axis=2)
    i1 = np.arange(n1)
    ang_b = 2.0 * np.pi * np.outer(i1, i1) / n1
    tb = np.concatenate([np.cos(ang_b), np.sin(ang_b)], axis=1)
    return (jnp.asarray(w1, F32), jnp.asarray(ta, F32), jnp.asarray(tb, F32))


def _fourier_kernel(x_ref, w1_ref, ta_ref, tb_ref, o_ref, p_ref, *, seq, scale):
    n1 = FT_N1
    n2 = seq // n1
    ln = FT_LANES

    w1 = w1_ref[...].astype(BF16)
    for c in range(seq // ROW_TILE):
        p = _dot(x_ref[0, c * ROW_TILE:(c + 1) * ROW_TILE, :].astype(BF16), w1)
        for r in range(ROW_TILE // n1):
            dst = pl.ds((c * (ROW_TILE // n1) + r) * FT_PAD, n1)
            p_ref[0, dst, :] = p[r * n1:(r + 1) * n1, 0:ln]
            p_ref[1, dst, :] = p[r * n1:(r + 1) * n1, ln:2 * ln]

    def stage_a(u, carry):
        res = []
        for d in range(FT_UNROLL_A):
            s1 = u * FT_UNROLL_A + d
            rows = pl.ds(s1, n2, stride=FT_PAD)
            g = jnp.concatenate([p_ref[0, rows, :], p_ref[1, rows, :]], axis=0).astype(BF16)
            t = ta_ref[s1]
            t_im = jnp.concatenate([-t[:, n2:2 * n2], t[:, 0:n2]], axis=1)
            res.append((rows, _dot(t.astype(BF16), g), _dot(t_im.astype(BF16), g)))
        for rows, b_re, b_im in res:
            p_ref[0, rows, :] = b_re
            p_ref[1, rows, :] = b_im
        return carry

    lax.fori_loop(0, n1 // FT_UNROLL_A, stage_a, 0)

    tb = tb_ref[...].astype(BF16)

    def stage_b(u, carry):
        for d in range(FT_UNROLL_B):
            k2 = u * FT_UNROLL_B + d
            rows = pl.ds(pl.multiple_of(k2 * FT_PAD, V7X_SUBLANES), n1)
            blk = jnp.concatenate([p_ref[0, rows, :], p_ref[1, rows, :]], axis=0).astype(BF16)
            o_ref[0, pl.ds(k2, n1, stride=n2), :] = _dot(tb, blk) * scale
        return carry

    lax.fori_loop(0, n2 // FT_UNROLL_B, stage_b, 0)


def _fourier(ftx, tables):
    bsz, seq, d_ft = ftx.shape
    w1, ta, tb = tables
    n1 = FT_N1
    n2 = seq // n1
    blk = pl.BlockSpec((1, seq, FT_LANES), lambda b, j: (b, 0, j))
    return pl.pallas_call(
        functools.partial(_fourier_kernel, seq=seq, scale=1.0 / math.sqrt(seq * FT_GROUP_DIM)),
        grid=(bsz, d_ft // FT_LANES),
        in_specs=[
            blk,
            pl.BlockSpec((FT_LANES, 2 * FT_LANES), lambda b, j: (0, 0)),
            pl.BlockSpec((n1, n2, 2 * n2), lambda b, j: (0, 0, 0), pipeline_mode=pl.Buffered(1)),
            pl.BlockSpec((n1, 2 * n1), lambda b, j: (0, 0)),
        ],
        out_specs=blk,
        out_shape=jax.ShapeDtypeStruct((bsz, seq, d_ft), F32),
        scratch_shapes=[pltpu.VMEM((2, n2 * FT_PAD, FT_LANES), F32)],
        compiler_params=_params("arbitrary", "arbitrary"),
        name="fourier",
    )(ftx, w1, ta, tb)


def _rg_scan_kernel(*refs, reverse, nt):
    if reverse:
        (prev_ref, cur_ref, next_ref, cw_ref, cb_ref, wg_ref, ba_ref, bx_ref, lam_ref, hf_ref,
         o_ref, ext_ref, a_ref, b_ref, h_ref, e_ref, p_ref, c_ref, carry_ref) = refs
    else:
        (prev_ref, cur_ref, next_ref, cw_ref, cb_ref, wg_ref, ba_ref, bx_ref, lam_ref,
         o_ref, ext_ref, a_ref, b_ref, h_ref, e_ref, p_ref, c_ref, carry_ref) = refs
        hf_ref = None
    tm = ROW_TILE
    halo = V7X_SUBLANES
    d_rg = cur_ref.shape[-1]
    half = d_rg // 2
    step = pl.program_id(1)
    ti = (nt - 1 - step) if reverse else step

    @pl.when(step == 0)
    def _():
        carry_ref[...] = jnp.zeros_like(carry_ref)

    ext_ref[0:halo, :] = jnp.where(ti == 0, 0.0, prev_ref[0])
    ext_ref[halo:halo + tm, :] = cur_ref[0]
    ext_ref[halo + tm:halo + tm + halo, :] = jnp.where(ti == nt - 1, 0.0, next_ref[0])
    xr = cb_ref[0]
    for k in range(4):
        xr = xr + ext_ref[halo - 2 + k:halo - 2 + k + tm, :] * cw_ref[0, 0, k:k + 1, :]
    xrb = xr.astype(BF16)

    nl = -lam_ref[0, 0]
    c8 = -RG_C * (jnp.maximum(nl, 0.0) + jnp.log1p(jnp.exp(-jnp.abs(nl))))

    for hh in range(2):
        cols = slice(hh * half, (hh + 1) * half)
        pre = _dot(xrb[:, cols], wg_ref[0, 0, hh])
        r = _sigmoid(pre[:, 0:half] + ba_ref[0, 0, :, cols])
        i = _sigmoid(pre[:, half:2 * half] + bx_ref[0, 0, :, cols])
        log_a = c8[:, cols] * r
        a = jnp.exp(log_a)
        th = jnp.tanh(log_a)
        nth = -2.0 * th
        sq = jnp.where(nth > 0.0, nth * lax.rsqrt(nth), 0.0) * lax.rsqrt(1.0 - th)
        bq = sq * (i * xr[:, cols])
        for kk in range(half // V7X_LANES):
            k = hh * (half // V7X_LANES) + kk
            lanes = slice(kk * V7X_LANES, (kk + 1) * V7X_LANES)
            for j in range(SCAN_SEGS):
                src = slice(j * SCAN_SEG_LEN, (j + 1) * SCAN_SEG_LEN)
                dst = slice(j * SCAN_SEG_STRIDE, j * SCAN_SEG_STRIDE + SCAN_SEG_LEN)
                a_ref[k, dst, :] = a[src, lanes]
                b_ref[k, dst, :] = bq[src, lanes]

    nblk = d_rg // V7X_LANES

    def seg_rows(t):
        tt = (SCAN_SEG_LEN - 1 - t) if reverse else t
        return pl.ds(tt, SCAN_SEGS, stride=SCAN_SEG_STRIDE)

    def sweep1(t, hp):
        rows = seg_rows(t)
        out = []
        for k in range(nblk):
            at = a_ref[k, rows, :]
            out.append((at * hp[k][0] + b_ref[k, rows, :], at * hp[k][1]))
        return tuple(out)

    zero = jnp.zeros((SCAN_SEGS, V7X_LANES), F32)
    ep = lax.fori_loop(0, SCAN_SEG_LEN, sweep1, tuple((zero, zero + 1.0) for _ in range(nblk)))
    for k in range(nblk):
        lanes = slice(k * V7X_LANES, (k + 1) * V7X_LANES)
        e_ref[:, lanes] = ep[k][0]
        p_ref[:, lanes] = ep[k][1]

    c = carry_ref[...]
    order = range(SCAN_SEGS - 1, -1, -1) if reverse else range(SCAN_SEGS)
    for j in order:
        c_ref[j:j + 1, :] = c
        c = p_ref[j:j + 1, :] * c + e_ref[j:j + 1, :]
    carry_ref[...] = c

    def sweep2(t, hs):
        rows = seg_rows(t)
        out = []
        for k in range(nblk):
            h = a_ref[k, rows, :] * hs[k] + b_ref[k, rows, :]
            h_ref[k, rows, :] = h
            out.append(h)
        return tuple(out)

    lax.fori_loop(0, SCAN_SEG_LEN, sweep2,
                  tuple(c_ref[:, k * V7X_LANES:(k + 1) * V7X_LANES] for k in range(nblk)))

    for k in range(nblk):
        lanes = slice(k * V7X_LANES, (k + 1) * V7X_LANES)
        for j in range(SCAN_SEGS):
            src = slice(j * SCAN_SEG_STRIDE, j * SCAN_SEG_STRIDE + SCAN_SEG_LEN)
            dst = slice(j * SCAN_SEG_LEN, (j + 1) * SCAN_SEG_LEN)
            if reverse:
                o_ref[0, dst, lanes] = h_ref[k, src, :] + hf_ref[0, dst, lanes]
            else:
                o_ref[0, dst, lanes] = h_ref[k, src, :]


def _rg_scan(l, rgx, conv_w, conv_b, wg, ba, bx, lam, h_fwd, *, reverse):
    bsz, seq, d_rg = rgx.shape
    nt = seq // ROW_TILE
    halo = V7X_SUBLANES
    per = ROW_TILE // halo
    nh = seq // halo
    dirn = 1 if reverse else 0

    def tile(i):
        return (nt - 1 - i) if reverse else i

    row = lambda b, i: (b, tile(i), 0)
    lay = lambda b, i: (l, 0, 0)
    ldir = lambda b, i: (l, dirn, 0, 0)
    in_specs = [
        pl.BlockSpec((1, halo, d_rg), lambda b, i: (b, jnp.maximum(tile(i) * per - 1, 0), 0)),
        pl.BlockSpec((1, ROW_TILE, d_rg), row),
        pl.BlockSpec((1, halo, d_rg), lambda b, i: (b, jnp.minimum((tile(i) + 1) * per, nh - 1), 0)),
        pl.BlockSpec((1, 1, 4, d_rg), lambda b, i: (l, 0, 0, 0)),
        pl.BlockSpec((1, 1, d_rg), lay),
        pl.BlockSpec((1, 1, 2, d_rg // 2, d_rg), lambda b, i: (l, dirn, 0, 0, 0)),
        pl.BlockSpec((1, 1, 1, d_rg), ldir),
        pl.BlockSpec((1, 1, 1, d_rg), ldir),
        pl.BlockSpec((1, 1, 1, d_rg), ldir),
    ]
    args = [rgx, rgx, rgx, conv_w, conv_b, wg, ba, bx, lam]
    if reverse:
        in_specs.append(pl.BlockSpec((1, ROW_TILE, d_rg), row))
        args.append(h_fwd)
    pad_rows = SCAN_SEGS * SCAN_SEG_STRIDE
    return pl.pallas_call(
        functools.partial(_rg_scan_kernel, reverse=reverse, nt=nt),
        grid=(bsz, nt),
        in_specs=in_specs,
        out_specs=pl.BlockSpec((1, ROW_TILE, d_rg), row),
        out_shape=jax.ShapeDtypeStruct((bsz, seq, d_rg), F32),
        scratch_shapes=[
            pltpu.VMEM((ROW_TILE + 2 * halo, d_rg), F32),
            pltpu.VMEM((d_rg // V7X_LANES, pad_rows, V7X_LANES), F32),
            pltpu.VMEM((d_rg // V7X_LANES, pad_rows, V7X_LANES), F32),
            pltpu.VMEM((d_rg // V7X_LANES, pad_rows, V7X_LANES), F32),
            pltpu.VMEM((SCAN_SEGS, d_rg), F32),
            pltpu.VMEM((SCAN_SEGS, d_rg), F32),
            pltpu.VMEM((SCAN_SEGS, d_rg), F32),
            pltpu.VMEM((1, d_rg), F32),
        ],
        compiler_params=_params("arbitrary", "arbitrary"),
        name="rg_scan_bwd" if reverse else "rg_scan_fwd",
    )(*args)


def _combine_kernel(x_ref, ygm_ref, gact_ref, h_ref, yft_ref, m_ref, gmix_ref, ftw_ref, ftb_ref,
                    wout_ref, gpost_ref, o_ref, *, d_gm, d_rg):
    gt1 = m_ref[0, 0, 2:3, :]
    o_ft = d_gm + d_rg
    d_mix = wout_ref.shape[1]
    yrg = _rms(h_ref[0] * gact_ref[0], gmix_ref[0, :, d_gm:o_ft])
    yft = _dot(yft_ref[0].astype(BF16), ftw_ref[0]) + ftb_ref[0]
    yft = _rms(yft, gmix_ref[0, :, o_ft:d_mix])
    o = _dot(ygm_ref[0].astype(BF16), wout_ref[0, 0:d_gm, :])
    o = o + _dot(yrg.astype(BF16), wout_ref[0, d_gm:o_ft, :])
    o = o + _dot(yft.astype(BF16), wout_ref[0, o_ft:d_mix, :])
    o_ref[0] = x_ref[0] + gt1 * _rms(o, gpost_ref[0])


def _combine(l, x, ygm, gact, h, yft, mod4, g_mix, ftw, ftb, w_out, g_post):
    bsz, seq, d = x.shape
    d_gm = ygm.shape[-1]
    d_rg = h.shape[-1]
    d_ft = yft.shape[-1]
    d_mix = w_out.shape[1]
    nt = seq // ROW_TILE
    lay = lambda b, i: (l, 0, 0)
    row = lambda b, i: (b, i, 0)
    return pl.pallas_call(
        functools.partial(_combine_kernel, d_gm=d_gm, d_rg=d_rg),
        grid=(bsz, nt),
        in_specs=[
            pl.BlockSpec((1, ROW_TILE, d), row),
            pl.BlockSpec((1, ROW_TILE, d_gm), row),
            pl.BlockSpec((1, ROW_TILE, d_rg), row),
            pl.BlockSpec((1, ROW_TILE, d_rg), row),
            pl.BlockSpec((1, ROW_TILE, d_ft), row),
            pl.BlockSpec((1, 1, 6, d), lambda b, i: (l, b, 0, 0)),
            pl.BlockSpec((1, 1, d_mix), lay),
            pl.BlockSpec((1, d_ft, d_ft), lay),
            pl.BlockSpec((1, 1, d_ft), lay),
            pl.BlockSpec((1, d_mix, d), lay),
            pl.BlockSpec((1, 1, d), lay),
        ],
        out_specs=pl.BlockSpec((1, ROW_TILE, d), row),
        out_shape=jax.ShapeDtypeStruct((bsz, seq, d), F32),
        compiler_params=_params("arbitrary", "arbitrary"),
        name="combine",
    )(x, ygm, gact, h, yft, mod4, g_mix, ftw, ftb, w_out, g_post)


def _ffn_kernel(x_ref, m_ref, gpre_ref, w1_ref, w2_ref, gpost_ref, o_ref, *, ff_chunk):
    x = x_ref[0]
    sh2 = m_ref[0, 0, 3:4, :]
    sc2 = m_ref[0, 0, 4:5, :]
    gt2 = m_ref[0, 0, 5:6, :]
    h = (_rms(x, gpre_ref[0]) * (1.0 + sc2) + sh2).astype(BF16)
    d_ff = w1_ref.shape[-1]
    acc = None
    for c in range(d_ff // ff_chunk):
        cols = slice(c * ff_chunk, (c + 1) * ff_chunk)
        a = jnp.maximum(_dot(h, w1_ref[0, :, cols]), 0.0)
        part = _dot((a * a).astype(BF16), w2_ref[0, cols, :])
        acc = part if acc is None else acc + part
    o_ref[0] = x + gt2 * _rms(acc, gpost_ref[0])


def _ffn(l, x, mod4, g_pre, w1, w2, g_post):
    bsz, seq, d = x.shape
    d_ff = w1.shape[-1]
    nt = seq // ROW_TILE
    lay = lambda b, i: (l, 0, 0)
    row = lambda b, i: (b, i, 0)
    return pl.pallas_call(
        functools.partial(_ffn_kernel, ff_chunk=d),
        grid=(bsz, nt),
        in_specs=[
            pl.BlockSpec((1, ROW_TILE, d), row),
            pl.BlockSpec((1, 1, 6, d), lambda b, i: (l, b, 0, 0)),
            pl.BlockSpec((1, 1, d), lay),
            pl.BlockSpec((1, d, d_ff), lay, pipeline_mode=pl.Buffered(1)),
            pl.BlockSpec((1, d_ff, d), lay, pipeline_mode=pl.Buffered(1)),
            pl.BlockSpec((1, 1, d), lay),
        ],
        out_specs=pl.BlockSpec((1, ROW_TILE, d), row),
        out_shape=jax.ShapeDtypeStruct((bsz, seq, d), F32),
        compiler_params=_params("arbitrary", "arbitrary"),
        name="ffn",
    )(x, mod4, g_pre, w1, w2, g_post)


def _block_diag(w, per):
    *lead, n, hd, _ = w.shape
    w = w.reshape(*lead, n // per, per, hd, hd)
    eye = jnp.eye(per, dtype=w.dtype)
    bd = jnp.einsum("...aio,ab->...aibo", w, eye)
    return bd.reshape(*lead, n // per, per * hd, per * hd)


def kernel(x, c, w_ada, b_ada, g_pre_mix, g_post_mix, w_in, gm_ln_g, gm_ln_b, gm_w_s, gm_b_s,
           rg_conv_w, rg_conv_b, rg_w_a, rg_b_a, rg_w_x, rg_b_x, rg_lam, ft_w, ft_b,
           g_mix_out, w_out, g_pre_ff, g_post_ff, w_ff1, w_ff2):
    bsz, seq, d = x.shape
    depth = w_in.shape[0]
    d_gm = gm_ln_g.shape[-1]
    d_rg = rg_conv_b.shape[-1]
    d_ft = ft_w.shape[1] * ft_w.shape[2]
    assert seq % ROW_TILE == 0 and ROW_TILE % GM_CHUNK == 0 and seq % FT_N1 == 0
    assert d_ft % FT_LANES == 0 and ft_w.shape[2] == FT_GROUP_DIM

    vec = lambda a: a.reshape(depth, 1, a.shape[-1])
    w_in_b = w_in.astype(BF16)
    w_out_b = w_out.astype(BF16)
    w_ff1_b = w_ff1.astype(BF16)
    w_ff2_b = w_ff2.astype(BF16)
    ws = gm_w_s.reshape(depth, GM_HEADS * GM_CHUNK, GM_CHUNK).astype(BF16)
    bs = jnp.repeat(jnp.swapaxes(gm_b_s, 1, 2), d_gm // GM_HEADS, axis=2)
    per = RG_HEADS // 2
    wg = jnp.concatenate([_block_diag(rg_w_a, per), _block_diag(rg_w_x, per)], axis=-1).astype(BF16)
    ba = rg_b_a.reshape(depth, 2, 1, d_rg)
    bx = rg_b_x.reshape(depth, 2, 1, d_rg)
    lam = rg_lam.reshape(depth, 2, 1, d_rg)
    conv_w = rg_conv_w.reshape(depth, 1, 4, d_rg)
    ftw = _block_diag(ft_w, FT_GROUPS).reshape(depth, d_ft, d_ft).astype(BF16)
    ftb = ft_b.reshape(depth, 1, d_ft)
    tables = _fourier_tables(seq)

    mod4 = _modulation(c, w_ada, b_ada).reshape(depth, bsz, 6, d)

    for l in range(depth):
        ygm, gact, rgx, ftx = _mixer_in(l, x, mod4, vec(g_pre_mix), w_in_b, vec(gm_ln_g), vec(gm_ln_b),
                                        ws, bs, vec(g_mix_out), d_gm=d_gm, d_rg=d_rg)
        yft = _fourier(ftx, tables)
        scan = functools.partial(_rg_scan, l, rgx, conv_w, vec(rg_conv_b), wg, ba, bx, lam)
        h = scan(scan(None, reverse=False), reverse=True)
        x = _combine(l, x, ygm, gact, h, yft, mod4, vec(g_mix_out), ftw, ftb, w_out_b, vec(g_post_mix))
        x = _ffn(l, x, mod4, vec(g_pre_ff), w_ff1_b, w_ff2_b, vec(g_post_ff))
    return x
```

```python
import functools
import math

import numpy as np
import jax
import jax.numpy as jnp
from jax import lax
from jax.experimental import pallas as pl
from jax.experimental.pallas import tpu as pltpu

F32 = jnp.float32
BF16 = jnp.bfloat16

EPS = 1e-6
RG_C = 8.0
RG_CONV = 4
RG_CONV_LEFT = 2
GM_HEADS = 4
GM_CHUNK = 128
RG_HEADS = 8
FT_GROUPS = 4
FT_GROUP_DIM = 64

V7X_SUBLANES = 8
V7X_LANES = 128
V7X_VMEM_LIMIT_BYTES = 56 * 1024 * 1024

ROW_TILE = 512
SCAN_SEGS = V7X_SUBLANES
SCAN_SEG_LEN = ROW_TILE // SCAN_SEGS
SCAN_SEG_STRIDE = SCAN_SEG_LEN + V7X_SUBLANES

FT_N1 = 64
FT_PAD = FT_N1 + V7X_SUBLANES
FT_LANES = 128
FT_UNROLL_A = 2
FT_UNROLL_B = 4


def _gelu(x):
    return 0.5 * x * (1.0 + jnp.tanh(math.sqrt(2.0 / math.pi) * (x + 0.044715 * (x * x * x))))


def _sigmoid(x):
    return 0.5 * jnp.tanh(0.5 * x) + 0.5


def _rms(x, g):
    return x * lax.rsqrt(jnp.mean(x * x, axis=-1, keepdims=True) + EPS) * g


def _dot(a, b):
    return jnp.dot(a, b, preferred_element_type=F32)


def _params(*sem):
    return pltpu.CompilerParams(dimension_semantics=sem, vmem_limit_bytes=V7X_VMEM_LIMIT_BYTES)


def _resident(shape, index_map):
    return pl.BlockSpec(shape, index_map, pipeline_mode=pl.Buffered(1))


def _mod_kernel(ct_ref, w_ref, b_ref, o_ref):
    ct = ct_ref[...]
    cond = ct * _sigmoid(ct)
    w = w_ref[0]
    for b in range(ct.shape[1]):
        o_ref[0, b:b + 1, :] = jnp.sum(w * cond[:, b:b + 1], axis=0, keepdims=True) + b_ref[0]


def _modulation(c, w_ada, b_ada):
    depth, d, d6 = w_ada.shape
    bsz = c.shape[0]
    nblk = d6 // d
    return pl.pallas_call(
        _mod_kernel,
        grid=(depth, nblk),
        in_specs=[
            pl.BlockSpec((d, bsz), lambda l, j: (0, 0)),
            pl.BlockSpec((1, d, d), lambda l, j: (l, 0, j)),
            pl.BlockSpec((1, 1, d), lambda l, j: (l, 0, j)),
        ],
        out_specs=pl.BlockSpec((1, bsz, d), lambda l, j: (l, 0, j)),
        out_shape=jax.ShapeDtypeStruct((depth, bsz, d6), F32),
        compiler_params=_params("arbitrary", "arbitrary"),
        name="modulation",
    )(c.T, w_ada, b_ada.reshape(depth, 1, d6))


def _mixer_in_kernel(x_ref, m_ref, gpre_ref, win_ref, lng_ref, lnb_ref, ws_ref, bs_ref, gmix_ref,
                     ygm_ref, gact_ref, rgx_ref, ftx_ref, *, d_gm, d_rg):
    x = x_ref[0]
    sh1 = m_ref[0, 0, 0:1, :]
    sc1 = m_ref[0, 0, 1:2, :]
    h = (_rms(x, gpre_ref[0]) * (1.0 + sc1) + sh1).astype(BF16)

    o_v, o_g, o_x, o_f = d_gm, 2 * d_gm, 2 * d_gm + d_rg, 2 * d_gm + 2 * d_rg
    u = _gelu(_dot(h, win_ref[0, :, 0:o_v]))
    v = _gelu(_dot(h, win_ref[0, :, o_v:o_g]))
    gact_ref[0] = _gelu(_dot(h, win_ref[0, :, o_g:o_x])).astype(BF16)
    rgx_ref[0] = _dot(h, win_ref[0, :, o_x:o_f])
    ftx_ref[0] = _dot(h, win_ref[0, :, o_f:o_f + d_gm]).astype(BF16)

    mu = jnp.mean(v, axis=-1, keepdims=True)
    vc = v - mu
    var = jnp.mean(vc * vc, axis=-1, keepdims=True)
    vn = (vc * lax.rsqrt(var + EPS) * lng_ref[0] + lnb_ref[0]).astype(BF16)

    head_dim = d_gm // GM_HEADS
    head = lax.broadcasted_iota(jnp.int32, (GM_CHUNK, d_gm), 1) // head_dim
    for c in range(x.shape[0] // GM_CHUNK):
        rows = slice(c * GM_CHUNK, (c + 1) * GM_CHUNK)
        r = _dot(ws_ref[0], vn[rows])
        s = r[0:GM_CHUNK]
        for hh in range(1, GM_HEADS):
            s = jnp.where(head == hh, r[hh * GM_CHUNK:(hh + 1) * GM_CHUNK], s)
        y = u[rows] * (s + bs_ref[0])
        ygm_ref[0, rows, :] = _rms(y, gmix_ref[0, :, 0:d_gm]).astype(BF16)


def _mixer_in(l, x, mod4, g_pre, w_in, ln_g, ln_b, ws, bs, g_mix, *, d_gm, d_rg):
    bsz, seq, d = x.shape
    d_in = w_in.shape[-1]
    nt = seq // ROW_TILE
    lay = lambda b, i: (l, 0, 0)
    row = lambda b, i: (b, i, 0)
    return pl.pallas_call(
        functools.partial(_mixer_in_kernel, d_gm=d_gm, d_rg=d_rg),
        grid=(bsz, nt),
        in_specs=[
            pl.BlockSpec((1, ROW_TILE, d), row),
            pl.BlockSpec((1, 1, 6, d), lambda b, i: (l, b, 0, 0)),
            pl.BlockSpec((1, 1, d), lay),
            _resident((1, d, d_in), lay),
            pl.BlockSpec((1, 1, d_gm), lay),
            pl.BlockSpec((1, 1, d_gm), lay),
            pl.BlockSpec((1, GM_HEADS * GM_CHUNK, GM_CHUNK), lay),
            pl.BlockSpec((1, GM_CHUNK, d_gm), lay),
            pl.BlockSpec((1, 1, d), lay),
        ],
        out_specs=[
            pl.BlockSpec((1, ROW_TILE, d_gm), row),
            pl.BlockSpec((1, ROW_TILE, d_rg), row),
            pl.BlockSpec((1, ROW_TILE, d_rg), row),
            pl.BlockSpec((1, ROW_TILE, d_gm), row),
        ],
        out_shape=[
            jax.ShapeDtypeStruct((bsz, seq, d_gm), BF16),
            jax.ShapeDtypeStruct((bsz, seq, d_rg), BF16),
            jax.ShapeDtypeStruct((bsz, seq, d_rg), F32),
            jax.ShapeDtypeStruct((bsz, seq, d_gm), BF16),
        ],
        compiler_params=_params("arbitrary", "arbitrary"),
        name="mixer_in",
    )(x, mod4, g_pre, w_in, ln_g, ln_b, ws, bs, g_mix)


def _fourier_tables(seq):
    n1 = FT_N1
    n2 = seq // n1
    gd = FT_GROUP_DIM
    j = np.arange(gd)
    ang = 2.0 * np.pi * np.outer(j, j) / gd
    eye = np.eye(FT_LANES // gd)
    w1 = np.concatenate([np.kron(eye, np.cos(ang)), -np.kron(eye, np.sin(ang))], axis=1)
    k2 = np.arange(n2)[None, :, None]
    s2 = np.arange(n2)[None, None, :]
    s1 = np.arange(n1)[:, None, None]
    ang_a = 2.0 * np.pi * ((k2 * (n1 * s2 + s1)) % seq) / seq
    ta = np.concatenate([np.cos(ang_a), np.sin(ang_a)], axis=2)
    i1 = np.arange(n1)
    ang_b = 2.0 * np.pi * np.outer(i1, i1) / n1
    tb = np.concatenate([np.cos(ang_b), np.sin(ang_b)], axis=1)
    return (jnp.asarray(w1, F32), jnp.asarray(ta, F32), jnp.asarray(tb, F32))


def _fourier_kernel(x_ref, w1_ref, ta_ref, tb_ref, o_ref, p_ref, *, seq, scale):
    n1 = FT_N1
    n2 = seq // n1
    ln = FT_LANES

    w1 = w1_ref[...].astype(BF16)
    for c in range(seq // ROW_TILE):
        p = _dot(x_ref[0, c * ROW_TILE:(c + 1) * ROW_TILE, :].astype(BF16), w1)
        for r in range(ROW_TILE // n1):
            dst = pl.ds((c * (ROW_TILE // n1) + r) * FT_PAD, n1)
            p_ref[0, dst, :] = p[r * n1:(r + 1) * n1, 0:ln]
            p_ref[1, dst, :] = p[r * n1:(r + 1) * n1, ln:2 * ln]

    def stage_a(u, carry):
        res = []
        for d in range(FT_UNROLL_A):
            s1 = u * FT_UNROLL_A + d
            rows = pl.ds(s1, n2, stride=FT_PAD)
            g = jnp.concatenate([p_ref[0, rows, :], p_ref[1, rows, :]], axis=0).astype(BF16)
            t = ta_ref[s1]
            t_im = jnp.concatenate([-t[:, n2:2 * n2], t[:, 0:n2]], axis=1)
            res.append((rows, _dot(t.astype(BF16), g), _dot(t_im.astype(BF16), g)))
        for rows, b_re, b_im in res:
            p_ref[0, rows, :] = b_re
            p_ref[1, rows, :] = b_im
        return carry

    lax.fori_loop(0, n1 // FT_UNROLL_A, stage_a, 0)

    tb = tb_ref[...].astype(BF16)

    def stage_b(u, carry):
        for d in range(FT_UNROLL_B):
            k2 = u * FT_UNROLL_B + d
            rows = pl.ds(pl.multiple_of(k2 * FT_PAD, V7X_SUBLANES), n1)
            blk = jnp.concatenate([p_ref[0, rows, :], p_ref[1, rows, :]], axis=0).astype(BF16)
            o_ref[0, pl.ds(k2, n1, stride=n2), :] = _dot(tb, blk) * scale
        return carry

    lax.fori_loop(0, n2 // FT_UNROLL_B, stage_b, 0)


def _fourier(ftx, tables):
    bsz, seq, d_ft = ftx.shape
    w1, ta, tb = tables
    n1 = FT_N1
    n2 = seq // n1
    blk = pl.BlockSpec((1, seq, FT_LANES), lambda b, j: (b, 0, j))
    return pl.pallas_call(
        functools.partial(_fourier_kernel, seq=seq, scale=1.0 / math.sqrt(seq * FT_GROUP_DIM)),
        grid=(bsz, d_ft // FT_LANES),
        in_specs=[
            blk,
            pl.BlockSpec((FT_LANES, 2 * FT_LANES), lambda b, j: (0, 0)),
            _resident((n1, n2, 2 * n2), lambda b, j: (0, 0, 0)),
            pl.BlockSpec((n1, 2 * n1), lambda b, j: (0, 0)),
        ],
        out_specs=blk,
        out_shape=jax.ShapeDtypeStruct((bsz, seq, d_ft), F32),
        scratch_shapes=[pltpu.VMEM((2, n2 * FT_PAD, FT_LANES), F32)],
        compiler_params=_params("arbitrary", "arbitrary"),
        name="fourier",
    )(ftx, w1, ta, tb)


class _ScanScratch:
    def __init__(self, ext, a, b, h, e, p, c, carry):
        self.ext, self.a, self.b, self.h = ext, a, b, h
        self.e, self.p, self.c, self.carry = e, p, c, carry


def _scan_scratch_shapes(d_rg):
    nblk = d_rg // V7X_LANES
    pad_rows = SCAN_SEGS * SCAN_SEG_STRIDE
    return [
        pltpu.VMEM((ROW_TILE + 2 * V7X_SUBLANES, d_rg), F32),
        pltpu.VMEM((nblk, pad_rows, V7X_LANES), F32),
        pltpu.VMEM((nblk, pad_rows, V7X_LANES), F32),
        pltpu.VMEM((nblk, pad_rows, V7X_LANES), F32),
        pltpu.VMEM((SCAN_SEGS, d_rg), F32),
        pltpu.VMEM((SCAN_SEGS, d_rg), F32),
        pltpu.VMEM((SCAN_SEGS, d_rg), F32),
        pltpu.VMEM((1, d_rg), F32),
    ]


def _scan_in_specs(l, dirn, tile, seq, d_rg):
    halo = V7X_SUBLANES
    per = ROW_TILE // halo
    nh = seq // halo
    lay = lambda b, i: (l, 0, 0)
    ldir = lambda b, i: (l, dirn, 0, 0)
    return [
        pl.BlockSpec((1, halo, d_rg), lambda b, i: (b, jnp.maximum(tile(i) * per - 1, 0), 0)),
        pl.BlockSpec((1, ROW_TILE, d_rg), lambda b, i: (b, tile(i), 0)),
        pl.BlockSpec((1, halo, d_rg), lambda b, i: (b, jnp.minimum((tile(i) + 1) * per, nh - 1), 0)),
        pl.BlockSpec((1, 1, RG_CONV, d_rg), lambda b, i: (l, 0, 0, 0)),
        pl.BlockSpec((1, 1, d_rg), lay),
        _resident((1, 1, 2, d_rg // 2, d_rg), lambda b, i: (l, dirn, 0, 0, 0)),
        pl.BlockSpec((1, 1, 1, d_rg), ldir),
        pl.BlockSpec((1, 1, 1, d_rg), ldir),
        pl.BlockSpec((1, 1, 1, d_rg), ldir),
    ]


def _rg_scan_tile(ti, nt, first_step, prev_ref, cur_ref, next_ref, cw_ref, cb_ref, wg_ref, ba_ref, bx_ref,
                  lam_ref, scr, *, reverse):
    tm = ROW_TILE
    halo = V7X_SUBLANES
    d_rg = cur_ref.shape[-1]
    half = d_rg // 2
    nblk = d_rg // V7X_LANES

    @pl.when(first_step)
    def _():
        scr.carry[...] = jnp.zeros_like(scr.carry)

    scr.ext[0:halo, :] = jnp.where(ti == 0, 0.0, prev_ref[0])
    scr.ext[halo:halo + tm, :] = cur_ref[0]
    scr.ext[halo + tm:halo + tm + halo, :] = jnp.where(ti == nt - 1, 0.0, next_ref[0])
    xr = cb_ref[0]
    for k in range(RG_CONV):
        lo = halo - RG_CONV_LEFT + k
        xr = xr + scr.ext[lo:lo + tm, :] * cw_ref[0, 0, k:k + 1, :]
    xrb = xr.astype(BF16)

    nl = -lam_ref[0, 0]
    c8 = -RG_C * (jnp.maximum(nl, 0.0) + jnp.log1p(jnp.exp(-jnp.abs(nl))))

    for hh in range(2):
        cols = slice(hh * half, (hh + 1) * half)
        pre = _dot(xrb[:, cols], wg_ref[0, 0, hh])
        r = _sigmoid(pre[:, 0:half] + ba_ref[0, 0, :, cols])
        i = _sigmoid(pre[:, half:2 * half] + bx_ref[0, 0, :, cols])
        log_a = c8[:, cols] * r
        a = jnp.exp(log_a)
        th = jnp.tanh(log_a)
        nth = -2.0 * th
        sq = jnp.where(nth > 0.0, nth * lax.rsqrt(nth), 0.0) * lax.rsqrt(1.0 - th)
        bq = sq * (i * xr[:, cols])
        for kk in range(half // V7X_LANES):
            k = hh * (half // V7X_LANES) + kk
            lanes = slice(kk * V7X_LANES, (kk + 1) * V7X_LANES)
            for j in range(SCAN_SEGS):
                src = slice(j * SCAN_SEG_LEN, (j + 1) * SCAN_SEG_LEN)
                dst = slice(j * SCAN_SEG_STRIDE, j * SCAN_SEG_STRIDE + SCAN_SEG_LEN)
                scr.a[k, dst, :] = a[src, lanes]
                scr.b[k, dst, :] = bq[src, lanes]

    steps = range(SCAN_SEG_LEN - 1, -1, -1) if reverse else range(SCAN_SEG_LEN)

    def seg_rows(t):
        return pl.ds(t, SCAN_SEGS, stride=SCAN_SEG_STRIDE)

    for k in range(nblk):
        lanes = slice(k * V7X_LANES, (k + 1) * V7X_LANES)
        e = jnp.zeros((SCAN_SEGS, V7X_LANES), F32)
        p = e + 1.0
        for t in steps:
            at = scr.a[k, seg_rows(t), :]
            e = at * e + scr.b[k, seg_rows(t), :]
            p = at * p
        scr.e[:, lanes] = e
        scr.p[:, lanes] = p

    c = scr.carry[...]
    order = range(SCAN_SEGS - 1, -1, -1) if reverse else range(SCAN_SEGS)
    for j in order:
        scr.c[j:j + 1, :] = c
        c = scr.p[j:j + 1, :] * c + scr.e[j:j + 1, :]
    scr.carry[...] = c

    for k in range(nblk):
        h = scr.c[:, k * V7X_LANES:(k + 1) * V7X_LANES]
        for t in steps:
            h = scr.a[k, seg_rows(t), :] * h + scr.b[k, seg_rows(t), :]
            scr.h[k, seg_rows(t), :] = h


def _scan_out_block(scr, k, j):
    return scr.h[k, j * SCAN_SEG_STRIDE:j * SCAN_SEG_STRIDE + SCAN_SEG_LEN, :]


def _rg_fwd_kernel(prev_ref, cur_ref, next_ref, cw_ref, cb_ref, wg_ref, ba_ref, bx_ref, lam_ref,
                   o_ref, *scratch, nt):
    scr = _ScanScratch(*scratch)
    step = pl.program_id(1)
    _rg_scan_tile(step, nt, step == 0, prev_ref, cur_ref, next_ref, cw_ref, cb_ref, wg_ref, ba_ref,
                  bx_ref, lam_ref, scr, reverse=False)
    for k in range(cur_ref.shape[-1] // V7X_LANES):
        for j in range(SCAN_SEGS):
            o_ref[0, j * SCAN_SEG_LEN:(j + 1) * SCAN_SEG_LEN, k * V7X_LANES:(k + 1) * V7X_LANES] = (
                _scan_out_block(scr, k, j))


def _rg_fwd(l, rgx, conv_w, conv_b, wg, ba, bx, lam):
    bsz, seq, d_rg = rgx.shape
    nt = seq // ROW_TILE
    return pl.pallas_call(
        functools.partial(_rg_fwd_kernel, nt=nt),
        grid=(bsz, nt),
        in_specs=_scan_in_specs(l, 0, lambda i: i, seq, d_rg),
        out_specs=pl.BlockSpec((1, ROW_TILE, d_rg), lambda b, i: (b, i, 0)),
        out_shape=jax.ShapeDtypeStruct((bsz, seq, d_rg), F32),
        scratch_shapes=_scan_scratch_shapes(d_rg),
        compiler_params=_params("arbitrary", "arbitrary"),
        name="rg_fwd",
    )(rgx, rgx, rgx, conv_w, conv_b, wg, ba, bx, lam)


def _mix_out_ffn_kernel(prev_ref, cur_ref, next_ref, cw_ref, cb_ref, wg_ref, ba_ref, bx_ref, lam_ref,
                        hf_ref, x_ref, ygm_ref, gact_ref, yft_ref, m_ref, gmix_ref, ftw_ref, ftb_ref,
                        wout_ref, gpost_ref, gpre2_ref, w1_ref, w2_ref, gpost2_ref,
                        o_ref, *scratch, nt, ff_chunk):
    scr = _ScanScratch(*scratch)
    step = pl.program_id(1)
    _rg_scan_tile(nt - 1 - step, nt, step == 0, prev_ref, cur_ref, next_ref, cw_ref, cb_ref, wg_ref,
                  ba_ref, bx_ref, lam_ref, scr, reverse=True)
    d_rg = cur_ref.shape[-1]
    d_gm = ygm_ref.shape[-1]
    o_ft = d_gm + d_rg
    d_mix = wout_ref.shape[1]
    h_bwd = jnp.concatenate(
        [jnp.concatenate([_scan_out_block(scr, k, j) for j in range(SCAN_SEGS)], axis=0)
         for k in range(d_rg // V7X_LANES)], axis=1)

    gt1 = m_ref[0, 0, 2:3, :]
    yrg = _rms((hf_ref[0] + h_bwd) * gact_ref[0].astype(F32), gmix_ref[0, :, d_gm:o_ft])
    yft = _dot(yft_ref[0].astype(BF16), ftw_ref[0]) + ftb_ref[0]
    yft = _rms(yft, gmix_ref[0, :, o_ft:d_mix])
    o = _dot(ygm_ref[0], wout_ref[0, 0:d_gm, :])
    o = o + _dot(yrg.astype(BF16), wout_ref[0, d_gm:o_ft, :])
    o = o + _dot(yft.astype(BF16), wout_ref[0, o_ft:d_mix, :])
    x = x_ref[0] + gt1 * _rms(o, gpost_ref[0])

    sh2 = m_ref[0, 0, 3:4, :]
    sc2 = m_ref[0, 0, 4:5, :]
    gt2 = m_ref[0, 0, 5:6, :]
    h = (_rms(x, gpre2_ref[0]) * (1.0 + sc2) + sh2).astype(BF16)
    d_ff = w1_ref.shape[-1]
    acc = None
    for c in range(d_ff // ff_chunk):
        cols = slice(c * ff_chunk, (c + 1) * ff_chunk)
        a = jnp.maximum(_dot(h, w1_ref[0, :, cols]), 0.0)
        part = _dot((a * a).astype(BF16), w2_ref[0, cols, :])
        acc = part if acc is None else acc + part
    o_ref[0] = x + gt2 * _rms(acc, gpost2_ref[0])


def _mix_out_ffn(l, rgx, conv_w, conv_b, wg, ba, bx, lam, h_fwd, x, ygm, gact, yft, mod4, g_mix, ftw, ftb,
                 w_out, g_post, g_pre2, w1, w2, g_post2):
    bsz, seq, d = x.shape
    d_rg = rgx.shape[-1]
    d_gm = ygm.shape[-1]
    d_ft = yft.shape[-1]
    d_mix = w_out.shape[1]
    d_ff = w1.shape[-1]
    nt = seq // ROW_TILE
    tile = lambda i: nt - 1 - i
    lay = lambda b, i: (l, 0, 0)
    row = lambda b, i: (b, tile(i), 0)
    in_specs = _scan_in_specs(l, 1, tile, seq, d_rg) + [
        pl.BlockSpec((1, ROW_TILE, d_rg), row),
        pl.BlockSpec((1, ROW_TILE, d), row),
        pl.BlockSpec((1, ROW_TILE, d_gm), row),
        pl.BlockSpec((1, ROW_TILE, d_rg), row),
        pl.BlockSpec((1, ROW_TILE, d_ft), row),
        pl.BlockSpec((1, 1, 6, d), lambda b, i: (l, b, 0, 0)),
        pl.BlockSpec((1, 1, d_mix), lay),
        _resident((1, d_ft, d_ft), lay),
        pl.BlockSpec((1, 1, d_ft), lay),
        _resident((1, d_mix, d), lay),
        pl.BlockSpec((1, 1, d), lay),
        pl.BlockSpec((1, 1, d), lay),
        _resident((1, d, d_ff), lay),
        _resident((1, d_ff, d), lay),
        pl.BlockSpec((1, 1, d), lay),
    ]
    return pl.pallas_call(
        functools.partial(_mix_out_ffn_kernel, nt=nt, ff_chunk=d),
        grid=(bsz, nt),
        in_specs=in_specs,
        out_specs=pl.BlockSpec((1, ROW_TILE, d), row),
        out_shape=jax.ShapeDtypeStruct((bsz, seq, d), F32),
        scratch_shapes=_scan_scratch_shapes(d_rg),
        compiler_params=_params("arbitrary", "arbitrary"),
        name="mix_out_ffn",
    )(rgx, rgx, rgx, conv_w, conv_b, wg, ba, bx, lam, h_fwd, x, ygm, gact, yft, mod4, g_mix, ftw, ftb,
      w_out, g_post, g_pre2, w1, w2, g_post2)


def _block_diag(w, per):
    *lead, n, hd, _ = w.shape
    w = w.reshape(*lead, n // per, per, hd, hd)
    eye = jnp.eye(per, dtype=w.dtype)
    bd = jnp.einsum("...aio,ab->...aibo", w, eye)
    return bd.reshape(*lead, n // per, per * hd, per * hd)


def kernel(x, c, w_ada, b_ada, g_pre_mix, g_post_mix, w_in, gm_ln_g, gm_ln_b, gm_w_s, gm_b_s,
           rg_conv_w, rg_conv_b, rg_w_a, rg_b_a, rg_w_x, rg_b_x, rg_lam, ft_w, ft_b,
           g_mix_out, w_out, g_pre_ff, g_post_ff, w_ff1, w_ff2):
    bsz, seq, d = x.shape
    depth = w_in.shape[0]
    d_gm = gm_ln_g.shape[-1]
    d_rg = rg_conv_b.shape[-1]
    d_ft = ft_w.shape[1] * ft_w.shape[2]
    assert seq % ROW_TILE == 0 and ROW_TILE % GM_CHUNK == 0 and seq % FT_N1 == 0
    assert d_ft % FT_LANES == 0 and ft_w.shape[2] == FT_GROUP_DIM
    assert rg_conv_w.shape[1] == RG_CONV and rg_w_a.shape[2] == RG_HEADS

    vec = lambda a: a.reshape(depth, 1, a.shape[-1])
    w_in_b = w_in.astype(BF16)
    w_out_b = w_out.astype(BF16)
    w_ff1_b = w_ff1.astype(BF16)
    w_ff2_b = w_ff2.astype(BF16)
    ws = gm_w_s.reshape(depth, GM_HEADS * GM_CHUNK, GM_CHUNK).astype(BF16)
    bs = jnp.repeat(jnp.swapaxes(gm_b_s, 1, 2), d_gm // GM_HEADS, axis=2)
    per = RG_HEADS // 2
    wg = jnp.concatenate([_block_diag(rg_w_a, per), _block_diag(rg_w_x, per)], axis=-1).astype(BF16)
    ba = rg_b_a.reshape(depth, 2, 1, d_rg)
    bx = rg_b_x.reshape(depth, 2, 1, d_rg)
    lam = rg_lam.reshape(depth, 2, 1, d_rg)
    conv_w = rg_conv_w.reshape(depth, 1, RG_CONV, d_rg)
    ftw = _block_diag(ft_w, FT_GROUPS).reshape(depth, d_ft, d_ft).astype(BF16)
    ftb = ft_b.reshape(depth, 1, d_ft)
    tables = _fourier_tables(seq)

    mod4 = _modulation(c, w_ada, b_ada).reshape(depth, bsz, 6, d)

    for l in range(depth):
        ygm, gact, rgx, ftx = _mixer_in(l, x, mod4, vec(g_pre_mix), w_in_b, vec(gm_ln_g), vec(gm_ln_b),
                                        ws, bs, vec(g_mix_out), d_gm=d_gm, d_rg=d_rg)
        yft = _fourier(ftx, tables)
        rg = (rgx, conv_w, vec(rg_conv_b), wg, ba, bx, lam)
        h_fwd = _rg_fwd(l, *rg)
        x = _mix_out_ffn(l, *rg, h_fwd, x, ygm, gact, yft, mod4, vec(g_mix_out), ftw, ftb, w_out_b,
                         vec(g_post_mix), vec(g_pre_ff), w_ff1_b, w_ff2_b, vec(g_post_ff))
    return x
```

```python
import functools
import math

import numpy as np
import jax
import jax.numpy as jnp
from jax import lax
from jax.experimental import pallas as pl
from jax.experimental.pallas import tpu as pltpu

F32 = jnp.float32
BF16 = jnp.bfloat16

EPS = 1e-6
RG_C = 8.0
RG_CONV = 4
RG_CONV_LEFT = 2
GM_HEADS = 4
GM_CHUNK = 128
RG_HEADS = 8
FT_GROUPS = 4
FT_GROUP_DIM = 64

V7X_SUBLANES = 8
V7X_LANES = 128
V7X_VMEM_LIMIT_BYTES = 56 * 1024 * 1024

ROW_TILE = 512
FF_CHUNKS = 8
SCAN_SEGS = V7X_SUBLANES
SCAN_SEG_LEN = ROW_TILE // SCAN_SEGS
SCAN_SEG_STRIDE = SCAN_SEG_LEN + V7X_SUBLANES

FT_N1 = 64
FT_PAD = FT_N1 + V7X_SUBLANES
FT_LANES = 128
FT_UNROLL_A = 2
FT_UNROLL_B = 4


def _gelu(x):
    return 0.5 * x * (1.0 + jnp.tanh(math.sqrt(2.0 / math.pi) * (x + 0.044715 * (x * x * x))))


def _sigmoid(x):
    return 0.5 * jnp.tanh(0.5 * x) + 0.5


def _rms(x, g):
    return x * lax.rsqrt(jnp.mean(x * x, axis=-1, keepdims=True) + EPS) * g


def _dot(a, b):
    return jnp.dot(a, b, preferred_element_type=F32)


def _params(*sem):
    return pltpu.CompilerParams(dimension_semantics=sem, vmem_limit_bytes=V7X_VMEM_LIMIT_BYTES)


def _resident(shape, index_map):
    return pl.BlockSpec(shape, index_map, pipeline_mode=pl.Buffered(1))


def _mod_kernel(ct_ref, w_ref, b_ref, o_ref):
    ct = ct_ref[...]
    cond = ct * _sigmoid(ct)
    w = w_ref[0]
    for b in range(ct.shape[1]):
        o_ref[0, b:b + 1, :] = jnp.sum(w * cond[:, b:b + 1], axis=0, keepdims=True) + b_ref[0]


def _modulation(c, w_ada, b_ada):
    depth, d, d6 = w_ada.shape
    bsz = c.shape[0]
    nblk = d6 // d
    return pl.pallas_call(
        _mod_kernel,
        grid=(depth, nblk),
        in_specs=[
            pl.BlockSpec((d, bsz), lambda l, j: (0, 0)),
            pl.BlockSpec((1, d, d), lambda l, j: (l, 0, j)),
            pl.BlockSpec((1, 1, d), lambda l, j: (l, 0, j)),
        ],
        out_specs=pl.BlockSpec((1, bsz, d), lambda l, j: (l, 0, j)),
        out_shape=jax.ShapeDtypeStruct((depth, bsz, d6), F32),
        compiler_params=_params("arbitrary", "arbitrary"),
        name="modulation",
    )(c.T, w_ada, b_ada.reshape(depth, 1, d6))


def _mixer_in_kernel(x_ref, m_ref, gpre_ref, win_ref, lng_ref, lnb_ref, ws_ref, bs_ref, gmix_ref,
                     ygm_ref, gact_ref, rgx_ref, ftx_ref, *, d_gm, d_rg):
    x = x_ref[0]
    sh1 = m_ref[0, 0, 0:1, :]
    sc1 = m_ref[0, 0, 1:2, :]
    h = (_rms(x, gpre_ref[0]) * (1.0 + sc1) + sh1).astype(BF16)

    o_v, o_g, o_x, o_f = d_gm, 2 * d_gm, 2 * d_gm + d_rg, 2 * d_gm + 2 * d_rg
    u = _gelu(_dot(h, win_ref[0, :, 0:o_v]))
    v = _gelu(_dot(h, win_ref[0, :, o_v:o_g]))
    gact_ref[0] = _gelu(_dot(h, win_ref[0, :, o_g:o_x])).astype(BF16)
    rgx_ref[0] = _dot(h, win_ref[0, :, o_x:o_f])
    ftx_ref[0] = _dot(h, win_ref[0, :, o_f:o_f + d_gm]).astype(BF16)

    mu = jnp.mean(v, axis=-1, keepdims=True)
    vc = v - mu
    var = jnp.mean(vc * vc, axis=-1, keepdims=True)
    vn = (vc * lax.rsqrt(var + EPS) * lng_ref[0] + lnb_ref[0]).astype(BF16)

    head_dim = d_gm // GM_HEADS
    head = lax.broadcasted_iota(jnp.int32, (GM_CHUNK, d_gm), 1) // head_dim
    for c in range(x.shape[0] // GM_CHUNK):
        rows = slice(c * GM_CHUNK, (c + 1) * GM_CHUNK)
        r = _dot(ws_ref[0], vn[rows])
        s = r[0:GM_CHUNK]
        for hh in range(1, GM_HEADS):
            s = jnp.where(head == hh, r[hh * GM_CHUNK:(hh + 1) * GM_CHUNK], s)
        y = u[rows] * (s + bs_ref[0])
        ygm_ref[0, rows, :] = _rms(y, gmix_ref[0, :, 0:d_gm]).astype(BF16)


def _mixer_in(l, x, mod4, g_pre, w_in, ln_g, ln_b, ws, bs, g_mix, *, d_gm, d_rg):
    bsz, seq, d = x.shape
    d_in = w_in.shape[-1]
    nt = seq // ROW_TILE
    lay = lambda b, i: (l, 0, 0)
    row = lambda b, i: (b, i, 0)
    return pl.pallas_call(
        functools.partial(_mixer_in_kernel, d_gm=d_gm, d_rg=d_rg),
        grid=(bsz, nt),
        in_specs=[
            pl.BlockSpec((1, ROW_TILE, d), row),
            pl.BlockSpec((1, 1, 6, d), lambda b, i: (l, b, 0, 0)),
            pl.BlockSpec((1, 1, d), lay),
            _resident((1, d, d_in), lay),
            pl.BlockSpec((1, 1, d_gm), lay),
            pl.BlockSpec((1, 1, d_gm), lay),
            pl.BlockSpec((1, GM_HEADS * GM_CHUNK, GM_CHUNK), lay),
            pl.BlockSpec((1, GM_CHUNK, d_gm), lay),
            pl.BlockSpec((1, 1, d), lay),
        ],
        out_specs=[
            pl.BlockSpec((1, ROW_TILE, d_gm), row),
            pl.BlockSpec((1, ROW_TILE, d_rg), row),
            pl.BlockSpec((1, ROW_TILE, d_rg), row),
            pl.BlockSpec((1, ROW_TILE, d_gm), row),
        ],
        out_shape=[
            jax.ShapeDtypeStruct((bsz, seq, d_gm), BF16),
            jax.ShapeDtypeStruct((bsz, seq, d_rg), BF16),
            jax.ShapeDtypeStruct((bsz, seq, d_rg), F32),
            jax.ShapeDtypeStruct((bsz, seq, d_gm), BF16),
        ],
        compiler_params=_params("arbitrary", "arbitrary"),
        name="mixer_in",
    )(x, mod4, g_pre, w_in, ln_g, ln_b, ws, bs, g_mix)


def _fourier_tables(seq):
    n1 = FT_N1
    n2 = seq // n1
    gd = FT_GROUP_DIM
    j = np.arange(gd)
    ang = 2.0 * np.pi * np.outer(j, j) / gd
    eye = np.eye(FT_LANES // gd)
    w1 = np.concatenate([np.kron(eye, np.cos(ang)), -np.kron(eye, np.sin(ang))], axis=1)
    k2 = np.arange(n2)[None, :, None]
    s2 = np.arange(n2)[None, None, :]
    s1 = np.arange(n1)[:, None, None]
    ang_a = 2.0 * np.pi * ((k2 * (n1 * s2 + s1)) % seq) / seq
    ta = np.concatenate([np.cos(ang_a), np.sin(ang_a)], axis=2)
    i1 = np.arange(n1)
    ang_b = 2.0 * np.pi * np.outer(i1, i1) / n1
    tb = np.concatenate([np.cos(ang_b), np.sin(ang_b)], axis=1)
    return (jnp.asarray(w1, F32), jnp.asarray(ta, F32), jnp.asarray(tb, F32))


def _fourier_kernel(x_ref, w1_ref, ta_ref, tb_ref, o_ref, p_ref, *, seq, scale):
    n1 = FT_N1
    n2 = seq // n1
    ln = FT_LANES

    w1 = w1_ref[...].astype(BF16)
    for c in range(seq // ROW_TILE):
        p = _dot(x_ref[0, c * ROW_TILE:(c + 1) * ROW_TILE, :].astype(BF16), w1)
        for r in range(ROW_TILE // n1):
            dst = pl.ds((c * (ROW_TILE // n1) + r) * FT_PAD, n1)
            p_ref[0, dst, :] = p[r * n1:(r + 1) * n1, 0:ln]
            p_ref[1, dst, :] = p[r * n1:(r + 1) * n1, ln:2 * ln]

    def stage_a(u, carry):
        res = []
        for d in range(FT_UNROLL_A):
            s1 = u * FT_UNROLL_A + d
            rows = pl.ds(s1, n2, stride=FT_PAD)
            g = jnp.concatenate([p_ref[0, rows, :], p_ref[1, rows, :]], axis=0).astype(BF16)
            t = ta_ref[s1]
            t_im = jnp.concatenate([-t[:, n2:2 * n2], t[:, 0:n2]], axis=1)
            res.append((rows, _dot(t.astype(BF16), g), _dot(t_im.astype(BF16), g)))
        for rows, b_re, b_im in res:
            p_ref[0, rows, :] = b_re
            p_ref[1, rows, :] = b_im
        return carry

    lax.fori_loop(0, n1 // FT_UNROLL_A, stage_a, 0)

    tb = tb_ref[...].astype(BF16)

    def stage_b(u, carry):
        for d in range(FT_UNROLL_B):
            k2 = u * FT_UNROLL_B + d
            rows = pl.ds(pl.multiple_of(k2 * FT_PAD, V7X_SUBLANES), n1)
            blk = jnp.concatenate([p_ref[0, rows, :], p_ref[1, rows, :]], axis=0).astype(BF16)
            o_ref[0, pl.ds(k2, n1, stride=n2), :] = _dot(tb, blk) * scale
        return carry

    lax.fori_loop(0, n2 // FT_UNROLL_B, stage_b, 0)


def _fourier(ftx, tables):
    bsz, seq, d_ft = ftx.shape
    w1, ta, tb = tables
    n1 = FT_N1
    n2 = seq // n1
    blk = pl.BlockSpec((1, seq, FT_LANES), lambda b, j: (b, 0, j))
    return pl.pallas_call(
        functools.partial(_fourier_kernel, seq=seq, scale=1.0 / math.sqrt(seq * FT_GROUP_DIM)),
        grid=(bsz, d_ft // FT_LANES),
        in_specs=[
            blk,
            pl.BlockSpec((FT_LANES, 2 * FT_LANES), lambda b, j: (0, 0)),
            _resident((n1, n2, 2 * n2), lambda b, j: (0, 0, 0)),
            pl.BlockSpec((n1, 2 * n1), lambda b, j: (0, 0)),
        ],
        out_specs=blk,
        out_shape=jax.ShapeDtypeStruct((bsz, seq, d_ft), F32),
        scratch_shapes=[pltpu.VMEM((2, n2 * FT_PAD, FT_LANES), F32)],
        compiler_params=_params("arbitrary", "arbitrary"),
        name="fourier",
    )(ftx, w1, ta, tb)


class _ScanScratch:
    def __init__(self, ext, a, b, h, e, p, c, carry):
        self.ext, self.a, self.b, self.h = ext, a, b, h
        self.e, self.p, self.c, self.carry = e, p, c, carry


def _scan_scratch_shapes(d_rg):
    nblk = d_rg // V7X_LANES
    pad_rows = SCAN_SEGS * SCAN_SEG_STRIDE
    return [
        pltpu.VMEM((ROW_TILE + 2 * V7X_SUBLANES, d_rg), F32),
        pltpu.VMEM((nblk, pad_rows, V7X_LANES), F32),
        pltpu.VMEM((nblk, pad_rows, V7X_LANES), F32),
        pltpu.VMEM((nblk, pad_rows, V7X_LANES), F32),
        pltpu.VMEM((SCAN_SEGS, d_rg), F32),
        pltpu.VMEM((SCAN_SEGS, d_rg), F32),
        pltpu.VMEM((SCAN_SEGS, d_rg), F32),
        pltpu.VMEM((1, d_rg), F32),
    ]


def _scan_in_specs(l, dirn, bt, seq, d_rg):
    halo = V7X_SUBLANES
    per = ROW_TILE // halo
    nh = seq // halo
    lay = lambda *g: (l, 0, 0)
    ldir = lambda *g: (l, dirn, 0, 0)

    def prev_halo(*g):
        b, t = bt(*g)
        return (b, jnp.maximum(t * per - 1, 0), 0)

    def next_halo(*g):
        b, t = bt(*g)
        return (b, jnp.minimum((t + 1) * per, nh - 1), 0)

    return [
        pl.BlockSpec((1, halo, d_rg), prev_halo),
        pl.BlockSpec((1, ROW_TILE, d_rg), lambda *g: (*bt(*g), 0)),
        pl.BlockSpec((1, halo, d_rg), next_halo),
        pl.BlockSpec((1, 1, RG_CONV, d_rg), lambda *g: (l, 0, 0, 0)),
        pl.BlockSpec((1, 1, d_rg), lay),
        _resident((1, 1, 2, d_rg // 2, d_rg), lambda *g: (l, dirn, 0, 0, 0)),
        pl.BlockSpec((1, 1, 1, d_rg), ldir),
        pl.BlockSpec((1, 1, 1, d_rg), ldir),
        pl.BlockSpec((1, 1, 1, d_rg), ldir),
    ]


def _rg_scan_tile(ti, nt, first_step, prev_ref, cur_ref, next_ref, cw_ref, cb_ref, wg_ref, ba_ref, bx_ref,
                  lam_ref, scr, *, reverse):
    tm = ROW_TILE
    halo = V7X_SUBLANES
    d_rg = cur_ref.shape[-1]
    half = d_rg // 2
    nblk = d_rg // V7X_LANES

    scr.ext[0:halo, :] = jnp.where(ti == 0, 0.0, prev_ref[0])
    scr.ext[halo:halo + tm, :] = cur_ref[0]
    scr.ext[halo + tm:halo + tm + halo, :] = jnp.where(ti == nt - 1, 0.0, next_ref[0])
    rh = tm // 2
    xrs = []
    for part in range(2):
        xr = cb_ref[0]
        for k in range(RG_CONV):
            lo = halo - RG_CONV_LEFT + k + part * rh
            xr = xr + scr.ext[lo:lo + rh, :] * cw_ref[0, 0, k:k + 1, :]
        xrs.append((xr, xr.astype(BF16)))
        yield

    nl = -lam_ref[0, 0]
    c8 = -RG_C * (jnp.maximum(nl, 0.0) + jnp.log1p(jnp.exp(-jnp.abs(nl))))

    segs_per_part = SCAN_SEGS // 2
    for hh in range(2):
        cols = slice(hh * half, (hh + 1) * half)
        for part, (xr, xrb) in enumerate(xrs):
            pre = _dot(xrb[:, cols], wg_ref[0, 0, hh])
            r = _sigmoid(pre[:, 0:half] + ba_ref[0, 0, :, cols])
            i = _sigmoid(pre[:, half:2 * half] + bx_ref[0, 0, :, cols])
            log_a = c8[:, cols] * r
            a = jnp.exp(log_a)
            th = jnp.tanh(log_a)
            nth = -2.0 * th
            sq = jnp.where(nth > 0.0, nth * lax.rsqrt(nth), 0.0) * lax.rsqrt(1.0 - th)
            bq = sq * (i * xr[:, cols])
            for kk in range(half // V7X_LANES):
                k = hh * (half // V7X_LANES) + kk
                lanes = slice(kk * V7X_LANES, (kk + 1) * V7X_LANES)
                for jj in range(segs_per_part):
                    j = part * segs_per_part + jj
                    src = slice(jj * SCAN_SEG_LEN, (jj + 1) * SCAN_SEG_LEN)
                    dst = slice(j * SCAN_SEG_STRIDE, j * SCAN_SEG_STRIDE + SCAN_SEG_LEN)
                    scr.a[k, dst, :] = a[src, lanes]
                    scr.b[k, dst, :] = bq[src, lanes]
            yield

    steps = range(SCAN_SEG_LEN - 1, -1, -1) if reverse else range(SCAN_SEG_LEN)

    def seg_rows(t):
        return pl.ds(t, SCAN_SEGS, stride=SCAN_SEG_STRIDE)

    for k in range(nblk):
        lanes = slice(k * V7X_LANES, (k + 1) * V7X_LANES)
        e = jnp.zeros((SCAN_SEGS, V7X_LANES), F32)
        p = e + 1.0
        for t in steps:
            at = scr.a[k, seg_rows(t), :]
            e = at * e + scr.b[k, seg_rows(t), :]
            p = at * p
        scr.e[:, lanes] = e
        scr.p[:, lanes] = p
    yield

    c = jnp.where(first_step, 0.0, scr.carry[...])
    order = range(SCAN_SEGS - 1, -1, -1) if reverse else range(SCAN_SEGS)
    for j in order:
        scr.c[j:j + 1, :] = c
        c = scr.p[j:j + 1, :] * c + scr.e[j:j + 1, :]
    scr.carry[...] = c

    for k in range(nblk):
        h = scr.c[:, k * V7X_LANES:(k + 1) * V7X_LANES]
        for t in steps:
            h = scr.a[k, seg_rows(t), :] * h + scr.b[k, seg_rows(t), :]
            scr.h[k, seg_rows(t), :] = h


def _scan_out_block(scr, k, j):
    return scr.h[k, j * SCAN_SEG_STRIDE:j * SCAN_SEG_STRIDE + SCAN_SEG_LEN, :]


def _rg_fwd_kernel(prev_ref, cur_ref, next_ref, cw_ref, cb_ref, wg_ref, ba_ref, bx_ref, lam_ref,
                   o_ref, *scratch, nt):
    scr = _ScanScratch(*scratch)
    step = pl.program_id(1)

    @pl.when((pl.program_id(0) == 0) & (step == 0))
    def _():
        scr.carry[...] = jnp.zeros_like(scr.carry)

    for _ in _rg_scan_tile(step, nt, step == 0, prev_ref, cur_ref, next_ref, cw_ref, cb_ref, wg_ref,
                           ba_ref, bx_ref, lam_ref, scr, reverse=False):
        pass
    for k in range(cur_ref.shape[-1] // V7X_LANES):
        for j in range(SCAN_SEGS):
            o_ref[0, j * SCAN_SEG_LEN:(j + 1) * SCAN_SEG_LEN, k * V7X_LANES:(k + 1) * V7X_LANES] = (
                _scan_out_block(scr, k, j))


def _rg_fwd(l, rgx, conv_w, conv_b, wg, ba, bx, lam):
    bsz, seq, d_rg = rgx.shape
    nt = seq // ROW_TILE
    return pl.pallas_call(
        functools.partial(_rg_fwd_kernel, nt=nt),
        grid=(bsz, nt),
        in_specs=_scan_in_specs(l, 0, lambda b, i: (b, i), seq, d_rg),
        out_specs=pl.BlockSpec((1, ROW_TILE, d_rg), lambda b, i: (b, i, 0)),
        out_shape=jax.ShapeDtypeStruct((bsz, seq, d_rg), F32),
        scratch_shapes=_scan_scratch_shapes(d_rg),
        compiler_params=_params("arbitrary", "arbitrary"),
        name="rg_fwd",
    )(rgx, rgx, rgx, conv_w, conv_b, wg, ba, bx, lam)


def _mix_out_ffn_kernel(prev_ref, cur_ref, next_ref, cw_ref, cb_ref, wg_ref, ba_ref, bx_ref, lam_ref,
                        hf_ref, x_ref, ygm_ref, gact_ref, yft_ref, m_ref, gmix_ref, ftw_ref, ftb_ref,
                        wout_ref, gpost_ref, gpre2_ref, w1_ref, w2_ref, gpost2_ref,
                        o_ref, *scratch, nt, ntot, ff_chunk):
    scr = _ScanScratch(*scratch)
    g = pl.program_id(0)

    @pl.when(g == 0)
    def _():
        scr.h[...] = jnp.zeros_like(scr.h)
        scr.carry[...] = jnp.zeros_like(scr.carry)

    d_rg = cur_ref.shape[-1]
    d_gm = ygm_ref.shape[-1]
    o_ft = d_gm + d_rg
    d_mix = wout_ref.shape[1]
    h_bwd = jnp.concatenate(
        [jnp.concatenate([_scan_out_block(scr, k, j) for j in range(SCAN_SEGS)], axis=0)
         for k in range(d_rg // V7X_LANES)], axis=1)

    gt1 = m_ref[0, 0, 2:3, :]
    yrg = _rms((hf_ref[0] + h_bwd) * gact_ref[0].astype(F32), gmix_ref[0, :, d_gm:o_ft])
    yft = _dot(yft_ref[0].astype(BF16), ftw_ref[0]) + ftb_ref[0]
    yft = _rms(yft, gmix_ref[0, :, o_ft:d_mix])
    o = _dot(ygm_ref[0], wout_ref[0, 0:d_gm, :])
    o = o + _dot(yrg.astype(BF16), wout_ref[0, d_gm:o_ft, :])
    o = o + _dot(yft.astype(BF16), wout_ref[0, o_ft:d_mix, :])
    x = x_ref[0] + gt1 * _rms(o, gpost_ref[0])

    sh2 = m_ref[0, 0, 3:4, :]
    sc2 = m_ref[0, 0, 4:5, :]
    gt2 = m_ref[0, 0, 5:6, :]
    h = (_rms(x, gpre2_ref[0]) * (1.0 + sc2) + sh2).astype(BF16)

    gs = jnp.minimum(g, ntot - 1) % nt
    scan = _rg_scan_tile(nt - 1 - gs, nt, gs == 0, prev_ref, cur_ref, next_ref, cw_ref, cb_ref, wg_ref,
                         ba_ref, bx_ref, lam_ref, scr, reverse=True)
    next(scan)
    d_ff = w1_ref.shape[-1]
    acc = None
    for c in range(d_ff // ff_chunk):
        cols = slice(c * ff_chunk, (c + 1) * ff_chunk)
        a = jnp.maximum(_dot(h, w1_ref[0, :, cols]), 0.0)
        part = _dot((a * a).astype(BF16), w2_ref[0, cols, :])
        acc = part if acc is None else acc + part
        next(scan, None)
    for _ in scan:
        pass
    o_ref[0] = x + gt2 * _rms(acc, gpost2_ref[0])


def _mix_out_ffn(l, rgx, conv_w, conv_b, wg, ba, bx, lam, h_fwd, x, ygm, gact, yft, mod4, g_mix, ftw, ftb,
                 w_out, g_post, g_pre2, w1, w2, g_post2):
    bsz, seq, d = x.shape
    d_rg = rgx.shape[-1]
    d_gm = ygm.shape[-1]
    d_ft = yft.shape[-1]
    d_mix = w_out.shape[1]
    d_ff = w1.shape[-1]
    nt = seq // ROW_TILE
    ntot = bsz * nt

    def scan_bt(g):
        gs = jnp.minimum(g, ntot - 1)
        return gs // nt, nt - 1 - gs % nt

    def tail_bt(g):
        gf = jnp.maximum(g - 1, 0)
        return gf // nt, nt - 1 - gf % nt

    lay = lambda g: (l, 0, 0)
    row = lambda g: (*tail_bt(g), 0)
    in_specs = _scan_in_specs(l, 1, scan_bt, seq, d_rg) + [
        pl.BlockSpec((1, ROW_TILE, d_rg), row),
        pl.BlockSpec((1, ROW_TILE, d), row),
        pl.BlockSpec((1, ROW_TILE, d_gm), row),
        pl.BlockSpec((1, ROW_TILE, d_rg), row),
        pl.BlockSpec((1, ROW_TILE, d_ft), row),
        pl.BlockSpec((1, 1, 6, d), lambda g: (l, tail_bt(g)[0], 0, 0)),
        pl.BlockSpec((1, 1, d_mix), lay),
        _resident((1, d_ft, d_ft), lay),
        pl.BlockSpec((1, 1, d_ft), lay),
        _resident((1, d_mix, d), lay),
        pl.BlockSpec((1, 1, d), lay),
        pl.BlockSpec((1, 1, d), lay),
        _resident((1, d, d_ff), lay),
        _resident((1, d_ff, d), lay),
        pl.BlockSpec((1, 1, d), lay),
    ]
    return pl.pallas_call(
        functools.partial(_mix_out_ffn_kernel, nt=nt, ntot=ntot, ff_chunk=d_ff // FF_CHUNKS),
        grid=(ntot + 1,),
        in_specs=in_specs,
        out_specs=pl.BlockSpec((1, ROW_TILE, d), row),
        out_shape=jax.ShapeDtypeStruct((bsz, seq, d), F32),
        scratch_shapes=_scan_scratch_shapes(d_rg),
        compiler_params=_params("arbitrary"),
        name="mix_out_ffn",
    )(rgx, rgx, rgx, conv_w, conv_b, wg, ba, bx, lam, h_fwd, x, ygm, gact, yft, mod4, g_mix, ftw, ftb,
      w_out, g_post, g_pre2, w1, w2, g_post2)


def _block_diag(w, per):
    *lead, n, hd, _ = w.shape
    w = w.reshape(*lead, n // per, per, hd, hd)
    eye = jnp.eye(per, dtype=w.dtype)
    bd = jnp.einsum("...aio,ab->...aibo", w, eye)
    return bd.reshape(*lead, n // per, per * hd, per * hd)


def kernel(x, c, w_ada, b_ada, g_pre_mix, g_post_mix, w_in, gm_ln_g, gm_ln_b, gm_w_s, gm_b_s,
           rg_conv_w, rg_conv_b, rg_w_a, rg_b_a, rg_w_x, rg_b_x, rg_lam, ft_w, ft_b,
           g_mix_out, w_out, g_pre_ff, g_post_ff, w_ff1, w_ff2):
    bsz, seq, d = x.shape
    depth = w_in.shape[0]
    d_gm = gm_ln_g.shape[-1]
    d_rg = rg_conv_b.shape[-1]
    d_ft = ft_w.shape[1] * ft_w.shape[2]
    assert seq % ROW_TILE == 0 and ROW_TILE % GM_CHUNK == 0 and seq % FT_N1 == 0
    assert d_ft % FT_LANES == 0 and ft_w.shape[2] == FT_GROUP_DIM
    assert rg_conv_w.shape[1] == RG_CONV and rg_w_a.shape[2] == RG_HEADS

    vec = lambda a: a.reshape(depth, 1, a.shape[-1])
    w_in_b = w_in.astype(BF16)
    w_out_b = w_out.astype(BF16)
    w_ff1_b = w_ff1.astype(BF16)
    w_ff2_b = w_ff2.astype(BF16)
    ws = gm_w_s.reshape(depth, GM_HEADS * GM_CHUNK, GM_CHUNK).astype(BF16)
    bs = jnp.repeat(jnp.swapaxes(gm_b_s, 1, 2), d_gm // GM_HEADS, axis=2)
    per = RG_HEADS // 2
    wg = jnp.concatenate([_block_diag(rg_w_a, per), _block_diag(rg_w_x, per)], axis=-1).astype(BF16)
    ba = rg_b_a.reshape(depth, 2, 1, d_rg)
    bx = rg_b_x.reshape(depth, 2, 1, d_rg)
    lam = rg_lam.reshape(depth, 2, 1, d_rg)
    conv_w = rg_conv_w.reshape(depth, 1, RG_CONV, d_rg)
    ftw = _block_diag(ft_w, FT_GROUPS).reshape(depth, d_ft, d_ft).astype(BF16)
    ftb = ft_b.reshape(depth, 1, d_ft)
    tables = _fourier_tables(seq)

    mod4 = _modulation(c, w_ada, b_ada).reshape(depth, bsz, 6, d)

    for l in range(depth):
        ygm, gact, rgx, ftx = _mixer_in(l, x, mod4, vec(g_pre_mix), w_in_b, vec(gm_ln_g), vec(gm_ln_b),
                                        ws, bs, vec(g_mix_out), d_gm=d_gm, d_rg=d_rg)
        yft = _fourier(ftx, tables)
        rg = (rgx, conv_w, vec(rg_conv_b), wg, ba, bx, lam)
        h_fwd = _rg_fwd(l, *rg)
        x = _mix_out_ffn(l, *rg, h_fwd, x, ygm, gact, yft, mod4, vec(g_mix_out), ftw, ftb, w_out_b,
                         vec(g_post_mix), vec(g_pre_ff), w_ff1_b, w_ff2_b, vec(g_post_ff))
    return x
```

```python
import functools
import math

import numpy as np
import jax
import jax.numpy as jnp
from jax import lax
from jax.experimental import pallas as pl
from jax.experimental.pallas import tpu as pltpu

F32 = jnp.float32
BF16 = jnp.bfloat16

EPS = 1e-6
RG_C = 8.0
RG_CONV = 4
RG_CONV_LEFT = 2
GM_HEADS = 4
GM_CHUNK = 128
RG_HEADS = 8
FT_GROUPS = 4
FT_GROUP_DIM = 64

V7X_SUBLANES = 8
V7X_LANES = 128
V7X_VMEM_LIMIT_BYTES = 56 * 1024 * 1024

ROW_TILE = 512
FF_CHUNKS = 8
SCAN_SEGS = V7X_SUBLANES
SCAN_SEG_LEN = ROW_TILE // SCAN_SEGS
SCAN_SEG_STRIDE = SCAN_SEG_LEN + V7X_SUBLANES

FT_N1 = 64
FT_PAD = FT_N1 + V7X_SUBLANES
FT_LANES = 128
FT_UNROLL_A = 2
FT_UNROLL_B = 4


def _gelu(x):
    return 0.5 * x * (1.0 + jnp.tanh(math.sqrt(2.0 / math.pi) * (x + 0.044715 * (x * x * x))))


def _sigmoid(x):
    return 0.5 * jnp.tanh(0.5 * x) + 0.5


def _rms(x, g):
    return x * lax.rsqrt(jnp.mean(x * x, axis=-1, keepdims=True) + EPS) * g


def _dot(a, b):
    return jnp.dot(a, b, preferred_element_type=F32)


def _params(*sem):
    return pltpu.CompilerParams(dimension_semantics=sem, vmem_limit_bytes=V7X_VMEM_LIMIT_BYTES)


def _resident(shape, index_map):
    return pl.BlockSpec(shape, index_map, pipeline_mode=pl.Buffered(1))


def _mod_kernel(ct_ref, w_ref, b_ref, o_ref):
    ct = ct_ref[...]
    cond = ct * _sigmoid(ct)
    w = w_ref[0]
    for b in range(ct.shape[1]):
        o_ref[0, b:b + 1, :] = jnp.sum(w * cond[:, b:b + 1], axis=0, keepdims=True) + b_ref[0]


def _modulation(c, w_ada, b_ada):
    depth, d, d6 = w_ada.shape
    bsz = c.shape[0]
    nblk = d6 // d
    return pl.pallas_call(
        _mod_kernel,
        grid=(depth, nblk),
        in_specs=[
            pl.BlockSpec((d, bsz), lambda l, j: (0, 0)),
            pl.BlockSpec((1, d, d), lambda l, j: (l, 0, j)),
            pl.BlockSpec((1, 1, d), lambda l, j: (l, 0, j)),
        ],
        out_specs=pl.BlockSpec((1, bsz, d), lambda l, j: (l, 0, j)),
        out_shape=jax.ShapeDtypeStruct((depth, bsz, d6), F32),
        compiler_params=_params("arbitrary", "arbitrary"),
        name="modulation",
    )(c.T, w_ada, b_ada.reshape(depth, 1, d6))


def _fourier_tables(seq):
    n1 = FT_N1
    n2 = seq // n1
    gd = FT_GROUP_DIM
    j = np.arange(gd)
    ang = 2.0 * np.pi * np.outer(j, j) / gd
    eye = np.eye(FT_LANES // gd)
    w1 = np.concatenate([np.kron(eye, np.cos(ang)), -np.kron(eye, np.sin(ang))], axis=1)
    k2 = np.arange(n2)[None, :, None]
    s2 = np.arange(n2)[None, None, :]
    s1 = np.arange(n1)[:, None, None]
    ang_a = 2.0 * np.pi * ((k2 * (n1 * s2 + s1)) % seq) / seq
    ta = np.concatenate([np.cos(ang_a), np.sin(ang_a)], axis=2)
    i1 = np.arange(n1)
    ang_b = 2.0 * np.pi * np.outer(i1, i1) / n1
    tb = np.concatenate([np.cos(ang_b), np.sin(ang_b)], axis=1)
    return (jnp.asarray(w1, F32), jnp.asarray(ta, F32), jnp.asarray(tb, F32))


def _fourier_kernel(x_ref, w1_ref, ta_ref, tb_ref, o_ref, p_ref, *, seq, scale):
    n1 = FT_N1
    n2 = seq // n1
    ln = FT_LANES

    w1 = w1_ref[...].astype(BF16)
    for c in range(seq // ROW_TILE):
        p = _dot(x_ref[0, c * ROW_TILE:(c + 1) * ROW_TILE, :].astype(BF16), w1)
        for r in range(ROW_TILE // n1):
            dst = pl.ds((c * (ROW_TILE // n1) + r) * FT_PAD, n1)
            p_ref[0, dst, :] = p[r * n1:(r + 1) * n1, 0:ln]
            p_ref[1, dst, :] = p[r * n1:(r + 1) * n1, ln:2 * ln]

    def stage_a(u, carry):
        res = []
        for d in range(FT_UNROLL_A):
            s1 = u * FT_UNROLL_A + d
            rows = pl.ds(s1, n2, stride=FT_PAD)
            g = jnp.concatenate([p_ref[0, rows, :], p_ref[1, rows, :]], axis=0).astype(BF16)
            t = ta_ref[s1]
            t_im = jnp.concatenate([-t[:, n2:2 * n2], t[:, 0:n2]], axis=1)
            res.append((rows, _dot(t.astype(BF16), g), _dot(t_im.astype(BF16), g)))
        for rows, b_re, b_im in res:
            p_ref[0, rows, :] = b_re
            p_ref[1, rows, :] = b_im
        return carry

    lax.fori_loop(0, n1 // FT_UNROLL_A, stage_a, 0)

    tb = tb_ref[...].astype(BF16)

    def stage_b(u, carry):
        for d in range(FT_UNROLL_B):
            k2 = u * FT_UNROLL_B + d
            rows = pl.ds(pl.multiple_of(k2 * FT_PAD, V7X_SUBLANES), n1)
            blk = jnp.concatenate([p_ref[0, rows, :], p_ref[1, rows, :]], axis=0).astype(BF16)
            o_ref[0, pl.ds(k2, n1, stride=n2), :] = _dot(tb, blk) * scale
        return carry

    lax.fori_loop(0, n2 // FT_UNROLL_B, stage_b, 0)


def _fourier(ftx, tables):
    bsz, seq, d_ft = ftx.shape
    w1, ta, tb = tables
    n1 = FT_N1
    n2 = seq // n1
    blk = pl.BlockSpec((1, seq, FT_LANES), lambda b, j: (b, 0, j))
    return pl.pallas_call(
        functools.partial(_fourier_kernel, seq=seq, scale=1.0 / math.sqrt(seq * FT_GROUP_DIM)),
        grid=(bsz, d_ft // FT_LANES),
        in_specs=[
            blk,
            pl.BlockSpec((FT_LANES, 2 * FT_LANES), lambda b, j: (0, 0)),
            _resident((n1, n2, 2 * n2), lambda b, j: (0, 0, 0)),
            pl.BlockSpec((n1, 2 * n1), lambda b, j: (0, 0)),
        ],
        out_specs=blk,
        out_shape=jax.ShapeDtypeStruct((bsz, seq, d_ft), F32),
        scratch_shapes=[pltpu.VMEM((2, n2 * FT_PAD, FT_LANES), F32)],
        compiler_params=_params("arbitrary", "arbitrary"),
        name="fourier",
    )(ftx, w1, ta, tb)


class _ScanScratch:
    def __init__(self, ext, a, b, h, e, p, c, carry):
        self.ext, self.a, self.b, self.h = ext, a, b, h
        self.e, self.p, self.c, self.carry = e, p, c, carry


def _scan_scratch_shapes(d_rg):
    nblk = d_rg // V7X_LANES
    pad_rows = SCAN_SEGS * SCAN_SEG_STRIDE
    return [
        pltpu.VMEM((ROW_TILE + 2 * V7X_SUBLANES, d_rg), F32),
        pltpu.VMEM((nblk, ROW_TILE, V7X_LANES), F32),
        pltpu.VMEM((nblk, ROW_TILE, V7X_LANES), F32),
        pltpu.VMEM((nblk, pad_rows, V7X_LANES), F32),
        pltpu.VMEM((SCAN_SEGS, d_rg), F32),
        pltpu.VMEM((SCAN_SEGS, d_rg), F32),
        pltpu.VMEM((SCAN_SEGS, d_rg), F32),
        pltpu.VMEM((1, d_rg), F32),
    ]


def _scan_in_specs(l, dirn, bt, seq, d_rg):
    halo = V7X_SUBLANES
    per = ROW_TILE // halo
    nh = seq // halo
    lay = lambda *g: (l, 0, 0)
    ldir = lambda *g: (l, dirn, 0, 0)

    def prev_halo(*g):
        b, t = bt(*g)
        return (b, jnp.maximum(t * per - 1, 0), 0)

    def next_halo(*g):
        b, t = bt(*g)
        return (b, jnp.minimum((t + 1) * per, nh - 1), 0)

    return [
        pl.BlockSpec((1, halo, d_rg), prev_halo),
        pl.BlockSpec((1, ROW_TILE, d_rg), lambda *g: (*bt(*g), 0)),
        pl.BlockSpec((1, halo, d_rg), next_halo),
        pl.BlockSpec((1, 1, RG_CONV, d_rg), lambda *g: (l, 0, 0, 0)),
        pl.BlockSpec((1, 1, d_rg), lay),
        _resident((1, 1, 2, d_rg // 2, d_rg), lambda *g: (l, dirn, 0, 0, 0)),
        pl.BlockSpec((1, 1, 1, d_rg), ldir),
        pl.BlockSpec((1, 1, 1, d_rg), ldir),
        pl.BlockSpec((1, 1, 1, d_rg), ldir),
    ]


def _rg_scan_tile(fill_ext, first_step, cw_ref, cb_ref, wg_ref, ba_ref, bx_ref, lam_ref, scr, *, reverse):
    tm = ROW_TILE
    halo = V7X_SUBLANES
    d_rg = scr.ext.shape[-1]
    half = d_rg // 2
    nblk = d_rg // V7X_LANES

    fill_ext()
    rh = tm // 2
    xrs = []
    for part in range(2):
        xr = cb_ref[0]
        for k in range(RG_CONV):
            lo = halo - RG_CONV_LEFT + k + part * rh
            xr = xr + scr.ext[lo:lo + rh, :] * cw_ref[0, 0, k:k + 1, :]
        xrs.append((xr, xr.astype(BF16)))
        yield

    nl = -lam_ref[0, 0]
    c8h = (-0.5 * RG_C) * (jnp.maximum(nl, 0.0) + jnp.log1p(jnp.exp(-jnp.abs(nl))))

    segs_per_part = SCAN_SEGS // 2
    for hh in range(2):
        cols = slice(hh * half, (hh + 1) * half)
        for part, (xr, xrb) in enumerate(xrs):
            pre = _dot(xrb[:, cols], wg_ref[0, 0, hh])
            t_r = jnp.tanh(pre[:, 0:half] + ba_ref[0, 0, :, cols])
            t_i = jnp.tanh(pre[:, half:2 * half] + bx_ref[0, 0, :, cols])
            log_a = c8h[:, cols] * t_r + c8h[:, cols]
            a = jnp.exp(log_a)
            th = jnp.tanh(log_a)
            nth = -0.5 * th
            sq = jnp.where(nth > 0.0, nth * lax.rsqrt(nth), 0.0) * lax.rsqrt(1.0 - th)
            bq = sq * ((t_i + 1.0) * xr[:, cols])
            for kk in range(half // V7X_LANES):
                k = hh * (half // V7X_LANES) + kk
                lanes = slice(kk * V7X_LANES, (kk + 1) * V7X_LANES)
                for jj in range(segs_per_part):
                    src = slice(jj * SCAN_SEG_LEN, (jj + 1) * SCAN_SEG_LEN)
                    dst = pl.ds(part * segs_per_part + jj, SCAN_SEG_LEN, stride=SCAN_SEGS)
                    scr.a[k, dst, :] = a[src, lanes]
                    scr.b[k, dst, :] = bq[src, lanes]
            yield

    steps = range(SCAN_SEG_LEN - 1, -1, -1) if reverse else range(SCAN_SEG_LEN)

    def at_step(t):
        return slice(t * SCAN_SEGS, (t + 1) * SCAN_SEGS)

    for k in range(nblk):
        lanes = slice(k * V7X_LANES, (k + 1) * V7X_LANES)
        e = jnp.zeros((SCAN_SEGS, V7X_LANES), F32)
        p = e + 1.0
        for t in steps:
            at = scr.a[k, at_step(t), :]
            e = at * e + scr.b[k, at_step(t), :]
            p = at * p
        scr.e[:, lanes] = e
        scr.p[:, lanes] = p
    yield

    c = jnp.where(first_step, 0.0, scr.carry[...])
    order = range(SCAN_SEGS - 1, -1, -1) if reverse else range(SCAN_SEGS)
    for j in order:
        scr.c[j:j + 1, :] = c
        c = scr.p[j:j + 1, :] * c + scr.e[j:j + 1, :]
    scr.carry[...] = c

    for k in range(nblk):
        h = scr.c[:, k * V7X_LANES:(k + 1) * V7X_LANES]
        for t in steps:
            h = scr.a[k, at_step(t), :] * h + scr.b[k, at_step(t), :]
            scr.h[k, pl.ds(t, SCAN_SEGS, stride=SCAN_SEG_STRIDE), :] = h


def _scan_out_block(scr, k, j):
    return scr.h[k, j * SCAN_SEG_STRIDE:j * SCAN_SEG_STRIDE + SCAN_SEG_LEN, :]


def _mix_in_scan_kernel(x_ref, m_ref, gpre_ref, win_ref, lng_ref, lnb_ref, ws_ref, bs_ref, gmix_ref,
                        cw_ref, cb_ref, wg_ref, ba_ref, bx_ref, lam_ref,
                        ygm_ref, gact_ref, rgx_ref, ftx_ref, hf_ref, *scratch, nt, d_gm, d_rg):
    scr = _ScanScratch(*scratch)
    g = pl.program_id(0)
    tm = ROW_TILE
    halo = V7X_SUBLANES

    @pl.when(g == 0)
    def _():
        scr.ext[...] = jnp.zeros_like(scr.ext)
        scr.carry[...] = jnp.zeros_like(scr.carry)

    x = x_ref[0]
    sh1 = m_ref[0, 0, 0:1, :]
    sc1 = m_ref[0, 0, 1:2, :]
    h = (_rms(x, gpre_ref[0]) * (1.0 + sc1) + sh1).astype(BF16)
    o_v, o_g, o_x, o_f = d_gm, 2 * d_gm, 2 * d_gm + d_rg, 2 * d_gm + 2 * d_rg

    rgx = _dot(h, win_ref[0, :, o_x:o_f])
    rgx_ref[0] = rgx

    gs = jnp.maximum(g - 1, 0) % nt

    def fill_ext():
        scr.ext[halo + tm:halo + tm + halo, :] = jnp.where(gs == nt - 1, 0.0, rgx[0:halo])

    scan = _rg_scan_tile(fill_ext, gs == 0, cw_ref, cb_ref, wg_ref, ba_ref, bx_ref, lam_ref, scr,
                         reverse=False)
    next(scan)
    u = _gelu(_dot(h, win_ref[0, :, 0:o_v]))
    next(scan)
    scr.ext[0:halo, :] = jnp.where(g % nt == 0, 0.0, scr.ext[tm:tm + halo, :])
    scr.ext[halo:halo + tm, :] = rgx
    v = _gelu(_dot(h, win_ref[0, :, o_v:o_g]))
    next(scan)
    gact_ref[0] = _gelu(_dot(h, win_ref[0, :, o_g:o_x])).astype(BF16)
    next(scan)
    ftx_ref[0] = _dot(h, win_ref[0, :, o_f:o_f + d_gm]).astype(BF16)
    next(scan)

    mu = jnp.mean(v, axis=-1, keepdims=True)
    vc = v - mu
    var = jnp.mean(vc * vc, axis=-1, keepdims=True)
    vn = (vc * lax.rsqrt(var + EPS) * lng_ref[0] + lnb_ref[0]).astype(BF16)
    head_dim = d_gm // GM_HEADS
    head = lax.broadcasted_iota(jnp.int32, (GM_CHUNK, d_gm), 1) // head_dim
    for c in range(tm // GM_CHUNK):
        rows = slice(c * GM_CHUNK, (c + 1) * GM_CHUNK)
        r = _dot(ws_ref[0], vn[rows])
        s = r[0:GM_CHUNK]
        for hh in range(1, GM_HEADS):
            s = jnp.where(head == hh, r[hh * GM_CHUNK:(hh + 1) * GM_CHUNK], s)
        y = u[rows] * (s + bs_ref[0])
        ygm_ref[0, rows, :] = _rms(y, gmix_ref[0, :, 0:d_gm]).astype(BF16)
        if c % 2 == 1:
            next(scan, None)
    for _ in scan:
        pass
    for k in range(d_rg // V7X_LANES):
        for j in range(SCAN_SEGS):
            hf_ref[0, j * SCAN_SEG_LEN:(j + 1) * SCAN_SEG_LEN, k * V7X_LANES:(k + 1) * V7X_LANES] = (
                _scan_out_block(scr, k, j))


def _mix_in_scan(l, x, mod4, g_pre, w_in, ln_g, ln_b, ws, bs, g_mix, conv_w, conv_b, wg, ba, bx, lam,
                 *, d_gm, d_rg):
    bsz, seq, d = x.shape
    d_in = w_in.shape[-1]
    nt = seq // ROW_TILE
    ntot = bsz * nt

    def proj_bt(g):
        gp = jnp.minimum(g, ntot - 1)
        return gp // nt, gp % nt

    def scan_bt(g):
        gs = jnp.maximum(g - 1, 0)
        return gs // nt, gs % nt

    lay = lambda g: (l, 0, 0)
    ldir = lambda g: (l, 0, 0, 0)
    row = lambda g: (*proj_bt(g), 0)
    return pl.pallas_call(
        functools.partial(_mix_in_scan_kernel, nt=nt, d_gm=d_gm, d_rg=d_rg),
        grid=(ntot + 1,),
        in_specs=[
            pl.BlockSpec((1, ROW_TILE, d), row),
            pl.BlockSpec((1, 1, 6, d), lambda g: (l, proj_bt(g)[0], 0, 0)),
            pl.BlockSpec((1, 1, d), lay),
            _resident((1, d, d_in), lay),
            pl.BlockSpec((1, 1, d_gm), lay),
            pl.BlockSpec((1, 1, d_gm), lay),
            pl.BlockSpec((1, GM_HEADS * GM_CHUNK, GM_CHUNK), lay),
            pl.BlockSpec((1, GM_CHUNK, d_gm), lay),
            pl.BlockSpec((1, 1, d), lay),
            pl.BlockSpec((1, 1, RG_CONV, d_rg), ldir),
            pl.BlockSpec((1, 1, d_rg), lay),
            _resident((1, 1, 2, d_rg // 2, d_rg), lambda g: (l, 0, 0, 0, 0)),
            pl.BlockSpec((1, 1, 1, d_rg), ldir),
            pl.BlockSpec((1, 1, 1, d_rg), ldir),
            pl.BlockSpec((1, 1, 1, d_rg), ldir),
        ],
        out_specs=[
            pl.BlockSpec((1, ROW_TILE, d_gm), row),
            pl.BlockSpec((1, ROW_TILE, d_rg), row),
            pl.BlockSpec((1, ROW_TILE, d_rg), row),
            pl.BlockSpec((1, ROW_TILE, d_gm), row),
            pl.BlockSpec((1, ROW_TILE, d_rg), lambda g: (*scan_bt(g), 0)),
        ],
        out_shape=[
            jax.ShapeDtypeStruct((bsz, seq, d_gm), BF16),
            jax.ShapeDtypeStruct((bsz, seq, d_rg), BF16),
            jax.ShapeDtypeStruct((bsz, seq, d_rg), F32),
            jax.ShapeDtypeStruct((bsz, seq, d_gm), BF16),
            jax.ShapeDtypeStruct((bsz, seq, d_rg), F32),
        ],
        scratch_shapes=_scan_scratch_shapes(d_rg),
        compiler_params=_params("arbitrary"),
        name="mix_in_scan",
    )(x, mod4, g_pre, w_in, ln_g, ln_b, ws, bs, g_mix, conv_w, conv_b, wg, ba, bx, lam)


def _mix_out_ffn_kernel(prev_ref, cur_ref, next_ref, cw_ref, cb_ref, wg_ref, ba_ref, bx_ref, lam_ref,
                        hf_ref, x_ref, ygm_ref, gact_ref, yft_ref, m_ref, gmix_ref, ftw_ref, ftb_ref,
                        wout_ref, gpost_ref, gpre2_ref, w1_ref, w2_ref, gpost2_ref,
                        o_ref, *scratch, nt, ntot, ff_chunk):
    scr = _ScanScratch(*scratch)
    g = pl.program_id(0)

    @pl.when(g == 0)
    def _():
        scr.h[...] = jnp.zeros_like(scr.h)
        scr.carry[...] = jnp.zeros_like(scr.carry)

    d_rg = cur_ref.shape[-1]
    d_gm = ygm_ref.shape[-1]
    o_ft = d_gm + d_rg
    d_mix = wout_ref.shape[1]
    h_bwd = jnp.concatenate(
        [jnp.concatenate([_scan_out_block(scr, k, j) for j in range(SCAN_SEGS)], axis=0)
         for k in range(d_rg // V7X_LANES)], axis=1)

    gt1 = m_ref[0, 0, 2:3, :]
    yrg = _rms((hf_ref[0] + h_bwd) * gact_ref[0].astype(F32), gmix_ref[0, :, d_gm:o_ft])
    yft = _dot(yft_ref[0].astype(BF16), ftw_ref[0]) + ftb_ref[0]
    yft = _rms(yft, gmix_ref[0, :, o_ft:d_mix])
    o = _dot(ygm_ref[0], wout_ref[0, 0:d_gm, :])
    o = o + _dot(yrg.astype(BF16), wout_ref[0, d_gm:o_ft, :])
    o = o + _dot(yft.astype(BF16), wout_ref[0, o_ft:d_mix, :])
    x = x_ref[0] + gt1 * _rms(o, gpost_ref[0])

    sh2 = m_ref[0, 0, 3:4, :]
    sc2 = m_ref[0, 0, 4:5, :]
    gt2 = m_ref[0, 0, 5:6, :]
    h = (_rms(x, gpre2_ref[0]) * (1.0 + sc2) + sh2).astype(BF16)

    gs = jnp.minimum(g, ntot - 1) % nt
    ti = nt - 1 - gs

    def fill_ext():
        halo = V7X_SUBLANES
        scr.ext[0:halo, :] = jnp.where(ti == 0, 0.0, prev_ref[0])
        scr.ext[halo:halo + ROW_TILE, :] = cur_ref[0]
        scr.ext[halo + ROW_TILE:halo + ROW_TILE + halo, :] = jnp.where(ti == nt - 1, 0.0, next_ref[0])

    scan = _rg_scan_tile(fill_ext, gs == 0, cw_ref, cb_ref, wg_ref, ba_ref, bx_ref, lam_ref, scr,
                         reverse=True)
    next(scan)
    d_ff = w1_ref.shape[-1]
    acc = None
    for c in range(d_ff // ff_chunk):
        cols = slice(c * ff_chunk, (c + 1) * ff_chunk)
        a = jnp.maximum(_dot(h, w1_ref[0, :, cols]), 0.0)
        part = _dot((a * a).astype(BF16), w2_ref[0, cols, :])
        acc = part if acc is None else acc + part
        next(scan, None)
    for _ in scan:
        pass
    o_ref[0] = x + gt2 * _rms(acc, gpost2_ref[0])


def _mix_out_ffn(l, rgx, conv_w, conv_b, wg, ba, bx, lam, h_fwd, x, ygm, gact, yft, mod4, g_mix, ftw, ftb,
                 w_out, g_post, g_pre2, w1, w2, g_post2):
    bsz, seq, d = x.shape
    d_rg = rgx.shape[-1]
    d_gm = ygm.shape[-1]
    d_ft = yft.shape[-1]
    d_mix = w_out.shape[1]
    d_ff = w1.shape[-1]
    nt = seq // ROW_TILE
    ntot = bsz * nt

    def scan_bt(g):
        gs = jnp.minimum(g, ntot - 1)
        return gs // nt, nt - 1 - gs % nt

    def tail_bt(g):
        gf = jnp.maximum(g - 1, 0)
        return gf // nt, nt - 1 - gf % nt

    lay = lambda g: (l, 0, 0)
    row = lambda g: (*tail_bt(g), 0)
    in_specs = _scan_in_specs(l, 1, scan_bt, seq, d_rg) + [
        pl.BlockSpec((1, ROW_TILE, d_rg), row),
        pl.BlockSpec((1, ROW_TILE, d), row),
        pl.BlockSpec((1, ROW_TILE, d_gm), row),
        pl.BlockSpec((1, ROW_TILE, d_rg), row),
        pl.BlockSpec((1, ROW_TILE, d_ft), row),
        pl.BlockSpec((1, 1, 6, d), lambda g: (l, tail_bt(g)[0], 0, 0)),
        pl.BlockSpec((1, 1, d_mix), lay),
        _resident((1, d_ft, d_ft), lay),
        pl.BlockSpec((1, 1, d_ft), lay),
        _resident((1, d_mix, d), lay),
        pl.BlockSpec((1, 1, d), lay),
        pl.BlockSpec((1, 1, d), lay),
        _resident((1, d, d_ff), lay),
        _resident((1, d_ff, d), lay),
        pl.BlockSpec((1, 1, d), lay),
    ]
    return pl.pallas_call(
        functools.partial(_mix_out_ffn_kernel, nt=nt, ntot=ntot, ff_chunk=d_ff // FF_CHUNKS),
        grid=(ntot + 1,),
        in_specs=in_specs,
        out_specs=pl.BlockSpec((1, ROW_TILE, d), row),
        out_shape=jax.ShapeDtypeStruct((bsz, seq, d), F32),
        scratch_shapes=_scan_scratch_shapes(d_rg),
        compiler_params=_params("arbitrary"),
        name="mix_out_ffn",
    )(rgx, rgx, rgx, conv_w, conv_b, wg, ba, bx, lam, h_fwd, x, ygm, gact, yft, mod4, g_mix, ftw, ftb,
      w_out, g_post, g_pre2, w1, w2, g_post2)


def _block_diag(w, per):
    *lead, n, hd, _ = w.shape
    w = w.reshape(*lead, n // per, per, hd, hd)
    eye = jnp.eye(per, dtype=w.dtype)
    bd = jnp.einsum("...aio,ab->...aibo", w, eye)
    return bd.reshape(*lead, n // per, per * hd, per * hd)


def kernel(x, c, w_ada, b_ada, g_pre_mix, g_post_mix, w_in, gm_ln_g, gm_ln_b, gm_w_s, gm_b_s,
           rg_conv_w, rg_conv_b, rg_w_a, rg_b_a, rg_w_x, rg_b_x, rg_lam, ft_w, ft_b,
           g_mix_out, w_out, g_pre_ff, g_post_ff, w_ff1, w_ff2):
    bsz, seq, d = x.shape
    depth = w_in.shape[0]
    d_gm = gm_ln_g.shape[-1]
    d_rg = rg_conv_b.shape[-1]
    d_ft = ft_w.shape[1] * ft_w.shape[2]
    assert seq % ROW_TILE == 0 and ROW_TILE % GM_CHUNK == 0 and seq % FT_N1 == 0
    assert d_ft % FT_LANES == 0 and ft_w.shape[2] == FT_GROUP_DIM
    assert rg_conv_w.shape[1] == RG_CONV and rg_w_a.shape[2] == RG_HEADS

    vec = lambda a: a.reshape(depth, 1, a.shape[-1])
    w_in_b = w_in.astype(BF16)
    w_out_b = w_out.astype(BF16)
    w_ff1_b = w_ff1.astype(BF16)
    w_ff2_b = w_ff2.astype(BF16)
    ws = gm_w_s.reshape(depth, GM_HEADS * GM_CHUNK, GM_CHUNK).astype(BF16)
    bs = jnp.repeat(jnp.swapaxes(gm_b_s, 1, 2), d_gm // GM_HEADS, axis=2)
    per = RG_HEADS // 2
    wg = (0.5 * jnp.concatenate([_block_diag(rg_w_a, per), _block_diag(rg_w_x, per)], axis=-1)).astype(BF16)
    ba = 0.5 * rg_b_a.reshape(depth, 2, 1, d_rg)
    bx = 0.5 * rg_b_x.reshape(depth, 2, 1, d_rg)
    lam = rg_lam.reshape(depth, 2, 1, d_rg)
    conv_w = rg_conv_w.reshape(depth, 1, RG_CONV, d_rg)
    ftw = _block_diag(ft_w, FT_GROUPS).reshape(depth, d_ft, d_ft).astype(BF16)
    ftb = ft_b.reshape(depth, 1, d_ft)
    tables = _fourier_tables(seq)

    mod4 = _modulation(c, w_ada, b_ada).reshape(depth, bsz, 6, d)

    for l in range(depth):
        rg = (conv_w, vec(rg_conv_b), wg, ba, bx, lam)
        ygm, gact, rgx, ftx, h_fwd = _mix_in_scan(l, x, mod4, vec(g_pre_mix), w_in_b, vec(gm_ln_g),
                                                  vec(gm_ln_b), ws, bs, vec(g_mix_out), *rg,
                                                  d_gm=d_gm, d_rg=d_rg)
        yft = _fourier(ftx, tables)
        x = _mix_out_ffn(l, rgx, *rg, h_fwd, x, ygm, gact, yft, mod4, vec(g_mix_out), ftw, ftb, w_out_b,
                         vec(g_post_mix), vec(g_pre_ff), w_ff1_b, w_ff2_b, vec(g_post_ff))
    return x
```

```python
import functools
import math

import numpy as np
import jax
import jax.numpy as jnp
from jax import lax
from jax.experimental import pallas as pl
from jax.experimental.pallas import tpu as pltpu

F32 = jnp.float32
BF16 = jnp.bfloat16

EPS = 1e-6
RG_C = 8.0
RG_CONV = 4
RG_CONV_LEFT = 2
GM_HEADS = 4
GM_CHUNK = 128
RG_HEADS = 8
FT_GROUPS = 4
FT_GROUP_DIM = 64

V7X_SUBLANES = 8
V7X_LANES = 128
V7X_VMEM_LIMIT_BYTES = 56 * 1024 * 1024

ROW_TILE = 512
FF_CHUNKS = 8
SCAN_ROW_PARTS = 4
TAIL_STAGES_PER_CHUNK = 3
SCAN_SEGS = V7X_SUBLANES
SCAN_SEG_LEN = ROW_TILE // SCAN_SEGS
SCAN_SEG_STRIDE = SCAN_SEG_LEN + V7X_SUBLANES

FT_N1 = 64
FT_PAD = FT_N1 + V7X_SUBLANES
FT_LANES = 128
FT_UNROLL_A = 4
FT_UNROLL_B = 16


def _gelu(x):
    return 0.5 * x * (1.0 + jnp.tanh(math.sqrt(2.0 / math.pi) * (x + 0.044715 * (x * x * x))))


def _sigmoid(x):
    return 0.5 * jnp.tanh(0.5 * x) + 0.5


def _rms(x, g):
    return x * lax.rsqrt(jnp.mean(x * x, axis=-1, keepdims=True) + EPS) * g


def _dot(a, b):
    return jnp.dot(a, b, preferred_element_type=F32)


def _params(*sem):
    return pltpu.CompilerParams(dimension_semantics=sem, vmem_limit_bytes=V7X_VMEM_LIMIT_BYTES)


def _resident(shape, index_map):
    return pl.BlockSpec(shape, index_map, pipeline_mode=pl.Buffered(1))


def _mod_kernel(ct_ref, w_ref, b_ref, o_ref):
    ct = ct_ref[...]
    cond = ct * _sigmoid(ct)
    w = w_ref[0]
    for b in range(ct.shape[1]):
        o_ref[0, b:b + 1, :] = jnp.sum(w * cond[:, b:b + 1], axis=0, keepdims=True) + b_ref[0]


def _modulation(c, w_ada, b_ada):
    depth, d, d6 = w_ada.shape
    bsz = c.shape[0]
    nblk = d6 // d
    return pl.pallas_call(
        _mod_kernel,
        grid=(depth, nblk),
        in_specs=[
            pl.BlockSpec((d, bsz), lambda l, j: (0, 0)),
            pl.BlockSpec((1, d, d), lambda l, j: (l, 0, j)),
            pl.BlockSpec((1, 1, d), lambda l, j: (l, 0, j)),
        ],
        out_specs=pl.BlockSpec((1, bsz, d), lambda l, j: (l, 0, j)),
        out_shape=jax.ShapeDtypeStruct((depth, bsz, d6), F32),
        compiler_params=_params("arbitrary", "arbitrary"),
        name="modulation",
    )(c.T, w_ada, b_ada.reshape(depth, 1, d6))


def _fourier_tables(seq):
    n1 = FT_N1
    n2 = seq // n1
    gd = FT_GROUP_DIM
    j = np.arange(gd)
    ang = 2.0 * np.pi * np.outer(j, j) / gd
    eye = np.eye(FT_LANES // gd)
    w1 = np.concatenate([np.kron(eye, np.cos(ang)), -np.kron(eye, np.sin(ang))], axis=1)
    k2 = np.arange(n2)[None, :, None]
    s2 = np.arange(n2)[None, None, :]
    s1 = np.arange(n1)[:, None, None]
    ang_a = 2.0 * np.pi * ((k2 * (n1 * s2 + s1)) % seq) / seq
    ta = np.concatenate([np.cos(ang_a), np.sin(ang_a)], axis=2)
    i1 = np.arange(n1)
    ang_b = 2.0 * np.pi * np.outer(i1, i1) / n1
    tb = np.concatenate([np.cos(ang_b), np.sin(ang_b)], axis=1)
    return (jnp.asarray(w1, F32), jnp.asarray(ta, F32), jnp.asarray(tb, F32))


def _fourier_kernel(x_ref, w1_ref, ta_ref, tb_ref, o_ref, p_ref, *, seq, scale):
    n1 = FT_N1
    n2 = seq // n1
    ln = FT_LANES

    w1 = w1_ref[...].astype(BF16)
    for c in range(seq // ROW_TILE):
        p = _dot(x_ref[0, c * ROW_TILE:(c + 1) * ROW_TILE, :].astype(BF16), w1)
        for r in range(ROW_TILE // n1):
            dst = pl.ds((c * (ROW_TILE // n1) + r) * FT_PAD, n1)
            p_ref[0, dst, :] = p[r * n1:(r + 1) * n1, 0:ln]
            p_ref[1, dst, :] = p[r * n1:(r + 1) * n1, ln:2 * ln]

    def stage_a(u, carry):
        res = []
        for d in range(FT_UNROLL_A):
            s1 = u * FT_UNROLL_A + d
            rows = pl.ds(s1, n2, stride=FT_PAD)
            g = jnp.concatenate([p_ref[0, rows, :], p_ref[1, rows, :]], axis=0).astype(BF16)
            t = ta_ref[s1]
            t_im = jnp.concatenate([-t[:, n2:2 * n2], t[:, 0:n2]], axis=1)
            res.append((rows, _dot(t.astype(BF16), g), _dot(t_im.astype(BF16), g)))
        for rows, b_re, b_im in res:
            p_ref[0, rows, :] = b_re
            p_ref[1, rows, :] = b_im
        return carry

    lax.fori_loop(0, n1 // FT_UNROLL_A, stage_a, 0)

    tb = tb_ref[...].astype(BF16)

    def stage_b(u, carry):
        for d in range(FT_UNROLL_B):
            k2 = u * FT_UNROLL_B + d
            rows = pl.ds(pl.multiple_of(k2 * FT_PAD, V7X_SUBLANES), n1)
            blk = jnp.concatenate([p_ref[0, rows, :], p_ref[1, rows, :]], axis=0).astype(BF16)
            o_ref[0, pl.ds(k2, n1, stride=n2), :] = _dot(tb, blk) * scale
        return carry

    lax.fori_loop(0, n2 // FT_UNROLL_B, stage_b, 0)


def _fourier(ftx, tables):
    bsz, seq, d_ft = ftx.shape
    w1, ta, tb = tables
    n1 = FT_N1
    n2 = seq // n1
    blk = pl.BlockSpec((1, seq, FT_LANES), lambda b, j: (b, 0, j))
    return pl.pallas_call(
        functools.partial(_fourier_kernel, seq=seq, scale=1.0 / math.sqrt(seq * FT_GROUP_DIM)),
        grid=(bsz, d_ft // FT_LANES),
        in_specs=[
            blk,
            pl.BlockSpec((FT_LANES, 2 * FT_LANES), lambda b, j: (0, 0)),
            _resident((n1, n2, 2 * n2), lambda b, j: (0, 0, 0)),
            pl.BlockSpec((n1, 2 * n1), lambda b, j: (0, 0)),
        ],
        out_specs=blk,
        out_shape=jax.ShapeDtypeStruct((bsz, seq, d_ft), F32),
        scratch_shapes=[pltpu.VMEM((2, n2 * FT_PAD, FT_LANES), F32)],
        compiler_params=_params("arbitrary", "arbitrary"),
        name="fourier",
    )(ftx, w1, ta, tb)


class _ScanScratch:
    def __init__(self, ext, a, b, h, e, p, c, carry):
        self.ext, self.a, self.b, self.h = ext, a, b, h
        self.e, self.p, self.c, self.carry = e, p, c, carry


def _scan_scratch_shapes(d_rg):
    nblk = d_rg // V7X_LANES
    pad_rows = SCAN_SEGS * SCAN_SEG_STRIDE
    return [
        pltpu.VMEM((ROW_TILE + 2 * V7X_SUBLANES, d_rg), F32),
        pltpu.VMEM((nblk, ROW_TILE, V7X_LANES), F32),
        pltpu.VMEM((nblk, ROW_TILE, V7X_LANES), F32),
        pltpu.VMEM((nblk, pad_rows, V7X_LANES), F32),
        pltpu.VMEM((SCAN_SEGS, d_rg), F32),
        pltpu.VMEM((SCAN_SEGS, d_rg), F32),
        pltpu.VMEM((SCAN_SEGS, d_rg), F32),
        pltpu.VMEM((1, d_rg), F32),
    ]


def _scan_in_specs(l, dirn, bt, seq, d_rg):
    halo = V7X_SUBLANES
    per = ROW_TILE // halo
    nh = seq // halo
    lay = lambda *g: (l, 0, 0)
    ldir = lambda *g: (l, dirn, 0, 0)

    def prev_halo(*g):
        b, t = bt(*g)
        return (b, jnp.maximum(t * per - 1, 0), 0)

    def next_halo(*g):
        b, t = bt(*g)
        return (b, jnp.minimum((t + 1) * per, nh - 1), 0)

    return [
        pl.BlockSpec((1, halo, d_rg), prev_halo),
        pl.BlockSpec((1, ROW_TILE, d_rg), lambda *g: (*bt(*g), 0)),
        pl.BlockSpec((1, halo, d_rg), next_halo),
        pl.BlockSpec((1, 1, RG_CONV, d_rg), lambda *g: (l, 0, 0, 0)),
        pl.BlockSpec((1, 1, d_rg), lay),
        _resident((1, 1, 2, d_rg // 2, d_rg), lambda *g: (l, dirn, 0, 0, 0)),
        pl.BlockSpec((1, 1, 1, d_rg), ldir),
        pl.BlockSpec((1, 1, 1, d_rg), ldir),
        pl.BlockSpec((1, 1, 1, d_rg), ldir),
    ]


def _rg_scan_tile(fill_ext, first_step, cw_ref, cb_ref, wg_ref, ba_ref, bx_ref, lam_ref, scr, *, reverse):
    tm = ROW_TILE
    halo = V7X_SUBLANES
    d_rg = scr.ext.shape[-1]
    half = d_rg // 2
    nblk = d_rg // V7X_LANES

    fill_ext()
    rh = tm // SCAN_ROW_PARTS
    xrs = []
    for part in range(SCAN_ROW_PARTS):
        xr = cb_ref[0]
        for k in range(RG_CONV):
            lo = halo - RG_CONV_LEFT + k + part * rh
            xr = xr + scr.ext[lo:lo + rh, :] * cw_ref[0, 0, k:k + 1, :]
        xrs.append((xr, xr.astype(BF16)))
        yield

    nl = -lam_ref[0, 0]
    c8h = (-0.5 * RG_C) * (jnp.maximum(nl, 0.0) + jnp.log1p(jnp.exp(-jnp.abs(nl))))

    segs_per_part = SCAN_SEGS // SCAN_ROW_PARTS
    for hh in range(2):
        cols = slice(hh * half, (hh + 1) * half)
        for part, (xr, xrb) in enumerate(xrs):
            pre = _dot(xrb[:, cols], wg_ref[0, 0, hh])
            t_r = jnp.tanh(pre[:, 0:half] + ba_ref[0, 0, :, cols])
            t_i = jnp.tanh(pre[:, half:2 * half] + bx_ref[0, 0, :, cols])
            log_a = c8h[:, cols] * t_r + c8h[:, cols]
            a = jnp.exp(log_a)
            th = jnp.tanh(log_a)
            nth = -0.5 * th
            sq = jnp.where(nth > 0.0, nth * lax.rsqrt(nth), 0.0) * lax.rsqrt(1.0 - th)
            bq = sq * ((t_i + 1.0) * xr[:, cols])
            for kk in range(half // V7X_LANES):
                k = hh * (half // V7X_LANES) + kk
                lanes = slice(kk * V7X_LANES, (kk + 1) * V7X_LANES)
                for jj in range(segs_per_part):
                    src = slice(jj * SCAN_SEG_LEN, (jj + 1) * SCAN_SEG_LEN)
                    dst = pl.ds(part * segs_per_part + jj, SCAN_SEG_LEN, stride=SCAN_SEGS)
                    scr.a[k, dst, :] = a[src, lanes]
                    scr.b[k, dst, :] = bq[src, lanes]
            yield

    steps = range(SCAN_SEG_LEN - 1, -1, -1) if reverse else range(SCAN_SEG_LEN)

    def at_step(t):
        return slice(t * SCAN_SEGS, (t + 1) * SCAN_SEGS)

    for k in range(nblk):
        lanes = slice(k * V7X_LANES, (k + 1) * V7X_LANES)
        e = jnp.zeros((SCAN_SEGS, V7X_LANES), F32)
        p = e + 1.0
        for t in steps:
            at = scr.a[k, at_step(t), :]
            e = at * e + scr.b[k, at_step(t), :]
            p = at * p
        scr.e[:, lanes] = e
        scr.p[:, lanes] = p
        yield

    c = jnp.where(first_step, 0.0, scr.carry[...])
    order = range(SCAN_SEGS - 1, -1, -1) if reverse else range(SCAN_SEGS)
    for j in order:
        scr.c[j:j + 1, :] = c
        c = scr.p[j:j + 1, :] * c + scr.e[j:j + 1, :]
    scr.carry[...] = c

    for k in range(nblk):
        h = scr.c[:, k * V7X_LANES:(k + 1) * V7X_LANES]
        for t in steps:
            h = scr.a[k, at_step(t), :] * h + scr.b[k, at_step(t), :]
            scr.h[k, pl.ds(t, SCAN_SEGS, stride=SCAN_SEG_STRIDE), :] = h
        if k + 1 < nblk:
            yield


def _advance(stages, n):
    for _ in range(n):
        next(stages, None)


def _scan_out_block(scr, k, j):
    return scr.h[k, j * SCAN_SEG_STRIDE:j * SCAN_SEG_STRIDE + SCAN_SEG_LEN, :]


def _mix_in_scan_kernel(x_ref, m_ref, gpre_ref, win_ref, lng_ref, lnb_ref, ws_ref, bs_ref, gmix_ref,
                        cw_ref, cb_ref, wg_ref, ba_ref, bx_ref, lam_ref,
                        ygm_ref, gact_ref, rgx_ref, ftx_ref, hf_ref, *scratch, nt, d_gm, d_rg):
    scr = _ScanScratch(*scratch)
    g = pl.program_id(0)
    tm = ROW_TILE
    halo = V7X_SUBLANES

    @pl.when(g == 0)
    def _():
        scr.ext[...] = jnp.zeros_like(scr.ext)
        scr.carry[...] = jnp.zeros_like(scr.carry)

    x = x_ref[0]
    sh1 = m_ref[0, 0, 0:1, :]
    sc1 = m_ref[0, 0, 1:2, :]
    h = (_rms(x, gpre_ref[0]) * (1.0 + sc1) + sh1).astype(BF16)
    o_v, o_g, o_x, o_f = d_gm, 2 * d_gm, 2 * d_gm + d_rg, 2 * d_gm + 2 * d_rg

    rgx = _dot(h, win_ref[0, :, o_x:o_f])
    rgx_ref[0] = rgx

    gs = jnp.maximum(g - 1, 0) % nt

    def fill_ext():
        scr.ext[halo + tm:halo + tm + halo, :] = jnp.where(gs == nt - 1, 0.0, rgx[0:halo])

    scan = _rg_scan_tile(fill_ext, gs == 0, cw_ref, cb_ref, wg_ref, ba_ref, bx_ref, lam_ref, scr,
                         reverse=False)
    _advance(scan, SCAN_ROW_PARTS // 2)
    u = _gelu(_dot(h, win_ref[0, :, 0:o_v]))
    _advance(scan, SCAN_ROW_PARTS - SCAN_ROW_PARTS // 2)
    scr.ext[0:halo, :] = jnp.where(g % nt == 0, 0.0, scr.ext[tm:tm + halo, :])
    scr.ext[halo:halo + tm, :] = rgx
    v = _gelu(_dot(h, win_ref[0, :, o_v:o_g]))
    _advance(scan, 2)
    gact_ref[0] = _gelu(_dot(h, win_ref[0, :, o_g:o_x])).astype(BF16)
    _advance(scan, 2)
    ftx_ref[0] = _dot(h, win_ref[0, :, o_f:o_f + d_gm]).astype(BF16)
    _advance(scan, 2)

    mu = jnp.mean(v, axis=-1, keepdims=True)
    vc = v - mu
    var = jnp.mean(vc * vc, axis=-1, keepdims=True)
    vn = (vc * lax.rsqrt(var + EPS) * lng_ref[0] + lnb_ref[0]).astype(BF16)
    head_dim = d_gm // GM_HEADS
    head = lax.broadcasted_iota(jnp.int32, (GM_CHUNK, d_gm), 1) // head_dim
    for c in range(tm // GM_CHUNK):
        rows = slice(c * GM_CHUNK, (c + 1) * GM_CHUNK)
        r = _dot(ws_ref[0], vn[rows])
        s = r[0:GM_CHUNK]
        for hh in range(1, GM_HEADS):
            s = jnp.where(head == hh, r[hh * GM_CHUNK:(hh + 1) * GM_CHUNK], s)
        y = u[rows] * (s + bs_ref[0])
        ygm_ref[0, rows, :] = _rms(y, gmix_ref[0, :, 0:d_gm]).astype(BF16)
        _advance(scan, 2)
    for _ in scan:
        pass
    for k in range(d_rg // V7X_LANES):
        for j in range(SCAN_SEGS):
            hf_ref[0, j * SCAN_SEG_LEN:(j + 1) * SCAN_SEG_LEN, k * V7X_LANES:(k + 1) * V7X_LANES] = (
                _scan_out_block(scr, k, j))


def _mix_in_scan(l, x, mod4, g_pre, w_in, ln_g, ln_b, ws, bs, g_mix, conv_w, conv_b, wg, ba, bx, lam,
                 *, d_gm, d_rg):
    bsz, seq, d = x.shape
    d_in = w_in.shape[-1]
    nt = seq // ROW_TILE
    ntot = bsz * nt

    def proj_bt(g):
        gp = jnp.minimum(g, ntot - 1)
        return gp // nt, gp % nt

    def scan_bt(g):
        gs = jnp.maximum(g - 1, 0)
        return gs // nt, gs % nt

    lay = lambda g: (l, 0, 0)
    ldir = lambda g: (l, 0, 0, 0)
    row = lambda g: (*proj_bt(g), 0)
    return pl.pallas_call(
        functools.partial(_mix_in_scan_kernel, nt=nt, d_gm=d_gm, d_rg=d_rg),
        grid=(ntot + 1,),
        in_specs=[
            pl.BlockSpec((1, ROW_TILE, d), row),
            pl.BlockSpec((1, 1, 6, d), lambda g: (l, proj_bt(g)[0], 0, 0)),
            pl.BlockSpec((1, 1, d), lay),
            _resident((1, d, d_in), lay),
            pl.BlockSpec((1, 1, d_gm), lay),
            pl.BlockSpec((1, 1, d_gm), lay),
            pl.BlockSpec((1, GM_HEADS * GM_CHUNK, GM_CHUNK), lay),
            pl.BlockSpec((1, GM_CHUNK, d_gm), lay),
            pl.BlockSpec((1, 1, d), lay),
            pl.BlockSpec((1, 1, RG_CONV, d_rg), ldir),
            pl.BlockSpec((1, 1, d_rg), lay),
            _resident((1, 1, 2, d_rg // 2, d_rg), lambda g: (l, 0, 0, 0, 0)),
            pl.BlockSpec((1, 1, 1, d_rg), ldir),
            pl.BlockSpec((1, 1, 1, d_rg), ldir),
            pl.BlockSpec((1, 1, 1, d_rg), ldir),
        ],
        out_specs=[
            pl.BlockSpec((1, ROW_TILE, d_gm), row),
            pl.BlockSpec((1, ROW_TILE, d_rg), row),
            pl.BlockSpec((1, ROW_TILE, d_rg), row),
            pl.BlockSpec((1, ROW_TILE, d_gm), row),
            pl.BlockSpec((1, ROW_TILE, d_rg), lambda g: (*scan_bt(g), 0)),
        ],
        out_shape=[
            jax.ShapeDtypeStruct((bsz, seq, d_gm), BF16),
            jax.ShapeDtypeStruct((bsz, seq, d_rg), BF16),
            jax.ShapeDtypeStruct((bsz, seq, d_rg), F32),
            jax.ShapeDtypeStruct((bsz, seq, d_gm), BF16),
            jax.ShapeDtypeStruct((bsz, seq, d_rg), F32),
        ],
        scratch_shapes=_scan_scratch_shapes(d_rg),
        compiler_params=_params("arbitrary"),
        name="mix_in_scan",
    )(x, mod4, g_pre, w_in, ln_g, ln_b, ws, bs, g_mix, conv_w, conv_b, wg, ba, bx, lam)


def _mix_out_ffn_kernel(prev_ref, cur_ref, next_ref, cw_ref, cb_ref, wg_ref, ba_ref, bx_ref, lam_ref,
                        hf_ref, x_ref, ygm_ref, gact_ref, yft_ref, m1_ref, m2_ref, gmix_ref, ftw_ref, ftb_ref,
                        wout_ref, gpost_ref, gpre2_ref, w1_ref, w2_ref, gpost2_ref,
                        o_ref, *scratch, nt, ntot, ff_chunk):
    scr = _ScanScratch(*scratch[:-2])
    x1_ref, h2_ref = scratch[-2:]
    g = pl.program_id(0)

    @pl.when(g == 0)
    def _():
        scr.carry[...] = jnp.zeros_like(scr.carry)
        x1_ref[...] = jnp.zeros_like(x1_ref)
        h2_ref[...] = jnp.zeros_like(h2_ref)

    d_rg = cur_ref.shape[-1]
    d_gm = ygm_ref.shape[-1]
    o_ft = d_gm + d_rg
    d_mix = wout_ref.shape[1]
    gs = jnp.minimum(g, ntot - 1) % nt
    ti = nt - 1 - gs
    handoff = {}

    def fill_ext():
        halo = V7X_SUBLANES
        scr.ext[0:halo, :] = jnp.where(ti == 0, 0.0, prev_ref[0])
        scr.ext[halo:halo + ROW_TILE, :] = cur_ref[0]
        scr.ext[halo + ROW_TILE:halo + ROW_TILE + halo, :] = jnp.where(ti == nt - 1, 0.0, next_ref[0])

    def token_mix_tail():
        yield from _rg_scan_tile(fill_ext, gs == 0, cw_ref, cb_ref, wg_ref, ba_ref, bx_ref, lam_ref,
                                 scr, reverse=True)
        yield
        h_bwd = jnp.concatenate(
            [jnp.concatenate([_scan_out_block(scr, k, j) for j in range(SCAN_SEGS)], axis=0)
             for k in range(d_rg // V7X_LANES)], axis=1)
        yrg = _rms((hf_ref[0] + h_bwd) * gact_ref[0].astype(F32), gmix_ref[0, :, d_gm:o_ft])
        yft = _dot(yft_ref[0].astype(BF16), ftw_ref[0]) + ftb_ref[0]
        yft = _rms(yft, gmix_ref[0, :, o_ft:d_mix])
        yield
        o = _dot(ygm_ref[0], wout_ref[0, 0:d_gm, :])
        o = o + _dot(yrg.astype(BF16), wout_ref[0, d_gm:o_ft, :])
        o = o + _dot(yft.astype(BF16), wout_ref[0, o_ft:d_mix, :])
        x1 = x_ref[0] + m1_ref[0, 0, 2:3, :] * _rms(o, gpost_ref[0])
        yield
        sh2 = m1_ref[0, 0, 3:4, :]
        sc2 = m1_ref[0, 0, 4:5, :]
        handoff["x1"] = x1
        handoff["h2"] = (_rms(x1, gpre2_ref[0]) * (1.0 + sc2) + sh2).astype(BF16)

    tail = token_mix_tail()
    _advance(tail, 2)

    h = h2_ref[...]
    d_ff = w1_ref.shape[-1]
    acc = None
    for c in range(d_ff // ff_chunk):
        cols = slice(c * ff_chunk, (c + 1) * ff_chunk)
        a = jnp.maximum(_dot(h, w1_ref[0, :, cols]), 0.0)
        part = _dot((a * a).astype(BF16), w2_ref[0, cols, :])
        acc = part if acc is None else acc + part
        _advance(tail, TAIL_STAGES_PER_CHUNK)
    for _ in tail:
        pass
    o_ref[0] = x1_ref[...] + m2_ref[0, 0, 5:6, :] * _rms(acc, gpost2_ref[0])
    x1_ref[...] = handoff["x1"]
    h2_ref[...] = handoff["h2"]


def _mix_out_ffn(l, rgx, conv_w, conv_b, wg, ba, bx, lam, h_fwd, x, ygm, gact, yft, mod4, g_mix, ftw, ftb,
                 w_out, g_post, g_pre2, w1, w2, g_post2):
    bsz, seq, d = x.shape
    d_rg = rgx.shape[-1]
    d_gm = ygm.shape[-1]
    d_ft = yft.shape[-1]
    d_mix = w_out.shape[1]
    d_ff = w1.shape[-1]
    nt = seq // ROW_TILE
    ntot = bsz * nt

    def head_bt(g):
        gs = jnp.minimum(g, ntot - 1)
        return gs // nt, nt - 1 - gs % nt

    def mlp_bt(g):
        gf = jnp.maximum(g - 1, 0)
        return gf // nt, nt - 1 - gf % nt

    lay = lambda g: (l, 0, 0)
    head_row = lambda g: (*head_bt(g), 0)
    in_specs = _scan_in_specs(l, 1, head_bt, seq, d_rg) + [
        pl.BlockSpec((1, ROW_TILE, d_rg), head_row),
        pl.BlockSpec((1, ROW_TILE, d), head_row),
        pl.BlockSpec((1, ROW_TILE, d_gm), head_row),
        pl.BlockSpec((1, ROW_TILE, d_rg), head_row),
        pl.BlockSpec((1, ROW_TILE, d_ft), head_row),
        pl.BlockSpec((1, 1, 6, d), lambda g: (l, head_bt(g)[0], 0, 0)),
        pl.BlockSpec((1, 1, 6, d), lambda g: (l, mlp_bt(g)[0], 0, 0)),
        pl.BlockSpec((1, 1, d_mix), lay),
        _resident((1, d_ft, d_ft), lay),
        pl.BlockSpec((1, 1, d_ft), lay),
        _resident((1, d_mix, d), lay),
        pl.BlockSpec((1, 1, d), lay),
        pl.BlockSpec((1, 1, d), lay),
        _resident((1, d, d_ff), lay),
        _resident((1, d_ff, d), lay),
        pl.BlockSpec((1, 1, d), lay),
    ]
    return pl.pallas_call(
        functools.partial(_mix_out_ffn_kernel, nt=nt, ntot=ntot, ff_chunk=d_ff // FF_CHUNKS),
        grid=(ntot + 1,),
        in_specs=in_specs,
        out_specs=pl.BlockSpec((1, ROW_TILE, d), lambda g: (*mlp_bt(g), 0)),
        out_shape=jax.ShapeDtypeStruct((bsz, seq, d), F32),
        scratch_shapes=_scan_scratch_shapes(d_rg) + [
            pltpu.VMEM((ROW_TILE, d), F32),
            pltpu.VMEM((ROW_TILE, d), BF16),
        ],
        compiler_params=_params("arbitrary"),
        name="mix_out_ffn",
    )(rgx, rgx, rgx, conv_w, conv_b, wg, ba, bx, lam, h_fwd, x, ygm, gact, yft, mod4, mod4, g_mix, ftw, ftb,
      w_out, g_post, g_pre2, w1, w2, g_post2)


def _block_diag(w, per):
    *lead, n, hd, _ = w.shape
    w = w.reshape(*lead, n // per, per, hd, hd)
    eye = jnp.eye(per, dtype=w.dtype)
    bd = jnp.einsum("...aio,ab->...aibo", w, eye)
    return bd.reshape(*lead, n // per, per * hd, per * hd)


def kernel(x, c, w_ada, b_ada, g_pre_mix, g_post_mix, w_in, gm_ln_g, gm_ln_b, gm_w_s, gm_b_s,
           rg_conv_w, rg_conv_b, rg_w_a, rg_b_a, rg_w_x, rg_b_x, rg_lam, ft_w, ft_b,
           g_mix_out, w_out, g_pre_ff, g_post_ff, w_ff1, w_ff2):
    bsz, seq, d = x.shape
    depth = w_in.shape[0]
    d_gm = gm_ln_g.shape[-1]
    d_rg = rg_conv_b.shape[-1]
    d_ft = ft_w.shape[1] * ft_w.shape[2]
    assert seq % ROW_TILE == 0 and ROW_TILE % GM_CHUNK == 0 and seq % FT_N1 == 0
    assert d_ft % FT_LANES == 0 and ft_w.shape[2] == FT_GROUP_DIM
    assert rg_conv_w.shape[1] == RG_CONV and rg_w_a.shape[2] == RG_HEADS

    vec = lambda a: a.reshape(depth, 1, a.shape[-1])
    w_in_b = w_in.astype(BF16)
    w_out_b = w_out.astype(BF16)
    w_ff1_b = w_ff1.astype(BF16)
    w_ff2_b = w_ff2.astype(BF16)
    ws = gm_w_s.reshape(depth, GM_HEADS * GM_CHUNK, GM_CHUNK).astype(BF16)
    bs = jnp.repeat(jnp.swapaxes(gm_b_s, 1, 2), d_gm // GM_HEADS, axis=2)
    per = RG_HEADS // 2
    wg = (0.5 * jnp.concatenate([_block_diag(rg_w_a, per), _block_diag(rg_w_x, per)], axis=-1)).astype(BF16)
    ba = 0.5 * rg_b_a.reshape(depth, 2, 1, d_rg)
    bx = 0.5 * rg_b_x.reshape(depth, 2, 1, d_rg)
    lam = rg_lam.reshape(depth, 2, 1, d_rg)
    conv_w = rg_conv_w.reshape(depth, 1, RG_CONV, d_rg)
    ftw = _block_diag(ft_w, FT_GROUPS).reshape(depth, d_ft, d_ft).astype(BF16)
    ftb = ft_b.reshape(depth, 1, d_ft)
    tables = _fourier_tables(seq)

    mod4 = _modulation(c, w_ada, b_ada).reshape(depth, bsz, 6, d)

    for l in range(depth):
        rg = (conv_w, vec(rg_conv_b), wg, ba, bx, lam)
        ygm, gact, rgx, ftx, h_fwd = _mix_in_scan(l, x, mod4, vec(g_pre_mix), w_in_b, vec(gm_ln_g),
                                                  vec(gm_ln_b), ws, bs, vec(g_mix_out), *rg,
                                                  d_gm=d_gm, d_rg=d_rg)
        yft = _fourier(ftx, tables)
        x = _mix_out_ffn(l, rgx, *rg, h_fwd, x, ygm, gact, yft, mod4, vec(g_mix_out), ftw, ftb, w_out_b,
                         vec(g_post_mix), vec(g_pre_ff), w_ff1_b, w_ff2_b, vec(g_post_ff))
    return x
```

```python
import functools
import math

import numpy as np
import jax
import jax.numpy as jnp
from jax import lax
from jax.experimental import pallas as pl
from jax.experimental.pallas import tpu as pltpu

F32 = jnp.float32
BF16 = jnp.bfloat16

EPS = 1e-6
RG_C = 8.0
RG_CONV = 4
RG_CONV_LEFT = 2
GM_HEADS = 4
GM_CHUNK = 128
RG_HEADS = 8
FT_GROUPS = 4
FT_GROUP_DIM = 64

V7X_SUBLANES = 8
V7X_LANES = 128
V7X_VMEM_LIMIT_BYTES = 56 * 1024 * 1024

ROW_TILE = 512
FF_CHUNKS = 8
SCAN_ROW_PARTS = 2
SCAN_SEGS = V7X_SUBLANES
SCAN_SEG_LEN = ROW_TILE // SCAN_SEGS
SCAN_SEG_STRIDE = SCAN_SEG_LEN + V7X_SUBLANES

FT_N1 = 64
FT_PAD = FT_N1 + V7X_SUBLANES
FT_LANES = 128
FT_UNROLL_A = 4
FT_UNROLL_B = 16


def _gelu(x):
    return 0.5 * x * (1.0 + jnp.tanh(math.sqrt(2.0 / math.pi) * (x + 0.044715 * (x * x * x))))


def _sigmoid(x):
    return 0.5 * jnp.tanh(0.5 * x) + 0.5


def _rms(x, g):
    return x * lax.rsqrt(jnp.mean(x * x, axis=-1, keepdims=True) + EPS) * g


def _dot(a, b):
    return jnp.dot(a, b, preferred_element_type=F32)


def _params(*sem):
    return pltpu.CompilerParams(dimension_semantics=sem, vmem_limit_bytes=V7X_VMEM_LIMIT_BYTES)


def _resident(shape, index_map):
    return pl.BlockSpec(shape, index_map, pipeline_mode=pl.Buffered(1))


def _mod_kernel(ct_ref, w_ref, b_ref, o_ref):
    ct = ct_ref[...]
    cond = ct * _sigmoid(ct)
    w = w_ref[0]
    for b in range(ct.shape[1]):
        o_ref[0, b:b + 1, :] = jnp.sum(w * cond[:, b:b + 1], axis=0, keepdims=True) + b_ref[0]


def _modulation(c, w_ada, b_ada):
    depth, d, d6 = w_ada.shape
    bsz = c.shape[0]
    nblk = d6 // d
    return pl.pallas_call(
        _mod_kernel,
        grid=(depth, nblk),
        in_specs=[
            pl.BlockSpec((d, bsz), lambda l, j: (0, 0)),
            pl.BlockSpec((1, d, d), lambda l, j: (l, 0, j)),
            pl.BlockSpec((1, 1, d), lambda l, j: (l, 0, j)),
        ],
        out_specs=pl.BlockSpec((1, bsz, d), lambda l, j: (l, 0, j)),
        out_shape=jax.ShapeDtypeStruct((depth, bsz, d6), F32),
        compiler_params=_params("arbitrary", "arbitrary"),
        name="modulation",
    )(c.T, w_ada, b_ada.reshape(depth, 1, d6))


def _fourier_tables(seq):
    n1 = FT_N1
    n2 = seq // n1
    gd = FT_GROUP_DIM
    j = np.arange(gd)
    ang = 2.0 * np.pi * np.outer(j, j) / gd
    eye = np.eye(FT_LANES // gd)
    w1 = np.concatenate([np.kron(eye, np.cos(ang)), -np.kron(eye, np.sin(ang))], axis=1)
    k2 = np.arange(n2)[None, :, None]
    s2 = np.arange(n2)[None, None, :]
    s1 = np.arange(n1)[:, None, None]
    ang_a = 2.0 * np.pi * ((k2 * (n1 * s2 + s1)) % seq) / seq
    ta = np.concatenate([np.cos(ang_a), np.sin(ang_a)], axis=2)
    i1 = np.arange(n1)
    ang_b = 2.0 * np.pi * np.outer(i1, i1) / n1
    tb = np.concatenate([np.cos(ang_b), np.sin(ang_b)], axis=1)
    return (jnp.asarray(w1, F32), jnp.asarray(ta, F32), jnp.asarray(tb, F32))


def _fourier_kernel(x_ref, w1_ref, ta_ref, tb_ref, o_ref, p_ref, *, seq, scale):
    n1 = FT_N1
    n2 = seq // n1
    ln = FT_LANES

    w1 = w1_ref[...].astype(BF16)
    for c in range(seq // ROW_TILE):
        p = _dot(x_ref[0, c * ROW_TILE:(c + 1) * ROW_TILE, :].astype(BF16), w1)
        for r in range(ROW_TILE // n1):
            dst = pl.ds((c * (ROW_TILE // n1) + r) * FT_PAD, n1)
            p_ref[0, dst, :] = p[r * n1:(r + 1) * n1, 0:ln]
            p_ref[1, dst, :] = p[r * n1:(r + 1) * n1, ln:2 * ln]

    def stage_a(u, carry):
        res = []
        for d in range(FT_UNROLL_A):
            s1 = u * FT_UNROLL_A + d
            rows = pl.ds(s1, n2, stride=FT_PAD)
            g = jnp.concatenate([p_ref[0, rows, :], p_ref[1, rows, :]], axis=0).astype(BF16)
            t = ta_ref[s1]
            t_im = jnp.concatenate([-t[:, n2:2 * n2], t[:, 0:n2]], axis=1)
            res.append((rows, _dot(t.astype(BF16), g), _dot(t_im.astype(BF16), g)))
        for rows, b_re, b_im in res:
            p_ref[0, rows, :] = b_re
            p_ref[1, rows, :] = b_im
        return carry

    lax.fori_loop(0, n1 // FT_UNROLL_A, stage_a, 0)

    tb = tb_ref[...].astype(BF16)

    def stage_b(u, carry):
        for d in range(FT_UNROLL_B):
            k2 = u * FT_UNROLL_B + d
            rows = pl.ds(pl.multiple_of(k2 * FT_PAD, V7X_SUBLANES), n1)
            blk = jnp.concatenate([p_ref[0, rows, :], p_ref[1, rows, :]], axis=0).astype(BF16)
            o_ref[0, pl.ds(k2, n1, stride=n2), :] = _dot(tb, blk) * scale
        return carry

    lax.fori_loop(0, n2 // FT_UNROLL_B, stage_b, 0)


def _fourier(ftx, tables):
    bsz, seq, d_ft = ftx.shape
    w1, ta, tb = tables
    n1 = FT_N1
    n2 = seq // n1
    blk = pl.BlockSpec((1, seq, FT_LANES), lambda b, j: (b, 0, j))
    return pl.pallas_call(
        functools.partial(_fourier_kernel, seq=seq, scale=1.0 / math.sqrt(seq * FT_GROUP_DIM)),
        grid=(bsz, d_ft // FT_LANES),
        in_specs=[
            blk,
            pl.BlockSpec((FT_LANES, 2 * FT_LANES), lambda b, j: (0, 0)),
            _resident((n1, n2, 2 * n2), lambda b, j: (0, 0, 0)),
            pl.BlockSpec((n1, 2 * n1), lambda b, j: (0, 0)),
        ],
        out_specs=blk,
        out_shape=jax.ShapeDtypeStruct((bsz, seq, d_ft), F32),
        scratch_shapes=[pltpu.VMEM((2, n2 * FT_PAD, FT_LANES), F32)],
        compiler_params=_params("arbitrary", "arbitrary"),
        name="fourier",
    )(ftx, w1, ta, tb)


class _ScanScratch:
    def __init__(self, ext, a, b, h, e, p, c, carry):
        self.ext, self.a, self.b, self.h = ext, a, b, h
        self.e, self.p, self.c, self.carry = e, p, c, carry


def _scan_scratch_shapes(d_rg):
    nblk = d_rg // V7X_LANES
    pad_rows = SCAN_SEGS * SCAN_SEG_STRIDE
    return [
        pltpu.VMEM((ROW_TILE + 2 * V7X_SUBLANES, d_rg), F32),
        pltpu.VMEM((nblk, ROW_TILE, V7X_LANES), F32),
        pltpu.VMEM((nblk, ROW_TILE, V7X_LANES), F32),
        pltpu.VMEM((nblk, pad_rows, V7X_LANES), F32),
        pltpu.VMEM((SCAN_SEGS, d_rg), F32),
        pltpu.VMEM((SCAN_SEGS, d_rg), F32),
        pltpu.VMEM((SCAN_SEGS, d_rg), F32),
        pltpu.VMEM((1, d_rg), F32),
    ]


def _scan_in_specs(l, dirn, bt, seq, d_rg):
    halo = V7X_SUBLANES
    per = ROW_TILE // halo
    nh = seq // halo
    lay = lambda *g: (l, 0, 0)
    ldir = lambda *g: (l, dirn, 0, 0)

    def prev_halo(*g):
        b, t = bt(*g)
        return (b, jnp.maximum(t * per - 1, 0), 0)

    def next_halo(*g):
        b, t = bt(*g)
        return (b, jnp.minimum((t + 1) * per, nh - 1), 0)

    return [
        pl.BlockSpec((1, halo, d_rg), prev_halo),
        pl.BlockSpec((1, ROW_TILE, d_rg), lambda *g: (*bt(*g), 0)),
        pl.BlockSpec((1, halo, d_rg), next_halo),
        pl.BlockSpec((1, 1, RG_CONV, d_rg), lambda *g: (l, 0, 0, 0)),
        pl.BlockSpec((1, 1, d_rg), lay),
        _resident((1, 1, 2, d_rg // 2, d_rg), lambda *g: (l, dirn, 0, 0, 0)),
        pl.BlockSpec((1, 1, 1, d_rg), ldir),
        pl.BlockSpec((1, 1, 1, d_rg), ldir),
        pl.BlockSpec((1, 1, 1, d_rg), ldir),
    ]


def _rg_scan_tile(fill_ext, first_step, cw_ref, cb_ref, wg_ref, ba_ref, bx_ref, lam_ref, scr, *, reverse):
    tm = ROW_TILE
    halo = V7X_SUBLANES
    d_rg = scr.ext.shape[-1]
    half = d_rg // 2
    nblk = d_rg // V7X_LANES

    fill_ext()
    nl = -lam_ref[0, 0]
    c8h = (-0.5 * RG_C) * (jnp.maximum(nl, 0.0) + jnp.log1p(jnp.exp(-jnp.abs(nl))))

    rh = tm // SCAN_ROW_PARTS
    segs_per_part = SCAN_SEGS // SCAN_ROW_PARTS
    for part in range(SCAN_ROW_PARTS):
        xr = cb_ref[0]
        for k in range(RG_CONV):
            lo = halo - RG_CONV_LEFT + k + part * rh
            xr = xr + scr.ext[lo:lo + rh, :] * cw_ref[0, 0, k:k + 1, :]
        xrb = xr.astype(BF16)
        yield
        for hh in range(2):
            cols = slice(hh * half, (hh + 1) * half)
            pre = _dot(xrb[:, cols], wg_ref[0, 0, hh])
            t_r = jnp.tanh(pre[:, 0:half] + ba_ref[0, 0, :, cols])
            t_i = jnp.tanh(pre[:, half:2 * half] + bx_ref[0, 0, :, cols])
            log_a = c8h[:, cols] * t_r + c8h[:, cols]
            a = jnp.exp(log_a)
            th = jnp.tanh(log_a)
            nth = -0.5 * th
            sq = jnp.where(nth > 0.0, nth * lax.rsqrt(nth), 0.0) * lax.rsqrt(1.0 - th)
            bq = sq * ((t_i + 1.0) * xr[:, cols])
            for kk in range(half // V7X_LANES):
                k = hh * (half // V7X_LANES) + kk
                lanes = slice(kk * V7X_LANES, (kk + 1) * V7X_LANES)
                for jj in range(segs_per_part):
                    src = slice(jj * SCAN_SEG_LEN, (jj + 1) * SCAN_SEG_LEN)
                    dst = pl.ds(part * segs_per_part + jj, SCAN_SEG_LEN, stride=SCAN_SEGS)
                    scr.a[k, dst, :] = a[src, lanes]
                    scr.b[k, dst, :] = bq[src, lanes]
            yield

    steps = range(SCAN_SEG_LEN - 1, -1, -1) if reverse else range(SCAN_SEG_LEN)

    def at_step(t):
        return slice(t * SCAN_SEGS, (t + 1) * SCAN_SEGS)

    for k in range(nblk):
        lanes = slice(k * V7X_LANES, (k + 1) * V7X_LANES)
        e = jnp.zeros((SCAN_SEGS, V7X_LANES), F32)
        p = e + 1.0
        for t in steps:
            at = scr.a[k, at_step(t), :]
            e = at * e + scr.b[k, at_step(t), :]
            p = at * p
        scr.e[:, lanes] = e
        scr.p[:, lanes] = p
    yield

    c = jnp.where(first_step, 0.0, scr.carry[...])
    order = range(SCAN_SEGS - 1, -1, -1) if reverse else range(SCAN_SEGS)
    for j in order:
        scr.c[j:j + 1, :] = c
        c = scr.p[j:j + 1, :] * c + scr.e[j:j + 1, :]
    scr.carry[...] = c

    for k in range(nblk):
        h = scr.c[:, k * V7X_LANES:(k + 1) * V7X_LANES]
        for t in steps:
            h = scr.a[k, at_step(t), :] * h + scr.b[k, at_step(t), :]
            scr.h[k, pl.ds(t, SCAN_SEGS, stride=SCAN_SEG_STRIDE), :] = h


def _advance(stages, n):
    for _ in range(n):
        next(stages, None)


def _scan_out_block(scr, k, j):
    return scr.h[k, j * SCAN_SEG_STRIDE:j * SCAN_SEG_STRIDE + SCAN_SEG_LEN, :]


def _mix_in_scan_kernel(x_ref, m_ref, gpre_ref, win_ref, lng_ref, lnb_ref, ws_ref, bs_ref, gmix_ref,
                        cw_ref, cb_ref, wg_ref, ba_ref, bx_ref, lam_ref,
                        ygm_ref, gact_ref, rgx_ref, ftx_ref, hf_ref, *scratch, nt, d_gm, d_rg):
    scr = _ScanScratch(*scratch)
    g = pl.program_id(0)
    tm = ROW_TILE
    halo = V7X_SUBLANES

    @pl.when(g == 0)
    def _():
        scr.ext[...] = jnp.zeros_like(scr.ext)
        scr.carry[...] = jnp.zeros_like(scr.carry)

    x = x_ref[0]
    sh1 = m_ref[0, 0, 0:1, :]
    sc1 = m_ref[0, 0, 1:2, :]
    h = (_rms(x, gpre_ref[0]) * (1.0 + sc1) + sh1).astype(BF16)
    o_v, o_g, o_x, o_f = d_gm, 2 * d_gm, 2 * d_gm + d_rg, 2 * d_gm + 2 * d_rg

    rgx = _dot(h, win_ref[0, :, o_x:o_f])
    rgx_ref[0] = rgx

    gs = jnp.maximum(g - 1, 0) % nt

    def fill_ext():
        scr.ext[halo + tm:halo + tm + halo, :] = jnp.where(gs == nt - 1, 0.0, rgx[0:halo])

    scan = _rg_scan_tile(fill_ext, gs == 0, cw_ref, cb_ref, wg_ref, ba_ref, bx_ref, lam_ref, scr,
                         reverse=False)
    _advance(scan, 1)
    v = _gelu(_dot(h, win_ref[0, :, o_v:o_g]))
    mu = jnp.mean(v, axis=-1, keepdims=True)
    vc = v - mu
    var = jnp.mean(vc * vc, axis=-1, keepdims=True)
    vn = (vc * lax.rsqrt(var + EPS) * lng_ref[0] + lnb_ref[0]).astype(BF16)
    _advance(scan, 1)
    gact_ref[0] = _gelu(_dot(h, win_ref[0, :, o_g:o_x])).astype(BF16)
    _advance(scan, 3 * (SCAN_ROW_PARTS - 1) + 1 - 2)
    scr.ext[0:halo, :] = jnp.where(g % nt == 0, 0.0, scr.ext[tm:tm + halo, :])
    scr.ext[halo:halo + tm, :] = rgx
    ftx_ref[0] = _dot(h, win_ref[0, :, o_f:o_f + d_gm]).astype(BF16)
    _advance(scan, 1)
    u = _gelu(_dot(h, win_ref[0, :, 0:o_v]))
    _advance(scan, 1)
    head_dim = d_gm // GM_HEADS
    head = lax.broadcasted_iota(jnp.int32, (GM_CHUNK, d_gm), 1) // head_dim
    for c in range(tm // GM_CHUNK):
        rows = slice(c * GM_CHUNK, (c + 1) * GM_CHUNK)
        r = _dot(ws_ref[0], vn[rows])
        s = r[0:GM_CHUNK]
        for hh in range(1, GM_HEADS):
            s = jnp.where(head == hh, r[hh * GM_CHUNK:(hh + 1) * GM_CHUNK], s)
        y = u[rows] * (s + bs_ref[0])
        ygm_ref[0, rows, :] = _rms(y, gmix_ref[0, :, 0:d_gm]).astype(BF16)
        if c == 1:
            _advance(scan, 1)
    for _ in scan:
        pass
    for k in range(d_rg // V7X_LANES):
        for j in range(SCAN_SEGS):
            hf_ref[0, j * SCAN_SEG_LEN:(j + 1) * SCAN_SEG_LEN, k * V7X_LANES:(k + 1) * V7X_LANES] = (
                _scan_out_block(scr, k, j))


def _mix_in_scan(l, x, mod4, g_pre, w_in, ln_g, ln_b, ws, bs, g_mix, conv_w, conv_b, wg, ba, bx, lam,
                 *, d_gm, d_rg):
    bsz, seq, d = x.shape
    d_in = w_in.shape[-1]
    nt = seq // ROW_TILE
    ntot = bsz * nt

    def proj_bt(g):
        gp = jnp.minimum(g, ntot - 1)
        return gp // nt, gp % nt

    def scan_bt(g):
        gs = jnp.maximum(g - 1, 0)
        return gs // nt, gs % nt

    lay = lambda g: (l, 0, 0)
    ldir = lambda g: (l, 0, 0, 0)
    row = lambda g: (*proj_bt(g), 0)
    return pl.pallas_call(
        functools.partial(_mix_in_scan_kernel, nt=nt, d_gm=d_gm, d_rg=d_rg),
        grid=(ntot + 1,),
        in_specs=[
            pl.BlockSpec((1, ROW_TILE, d), row),
            pl.BlockSpec((1, 1, 6, d), lambda g: (l, proj_bt(g)[0], 0, 0)),
            pl.BlockSpec((1, 1, d), lay),
            _resident((1, d, d_in), lay),
            pl.BlockSpec((1, 1, d_gm), lay),
            pl.BlockSpec((1, 1, d_gm), lay),
            pl.BlockSpec((1, GM_HEADS * GM_CHUNK, GM_CHUNK), lay),
            pl.BlockSpec((1, GM_CHUNK, d_gm), lay),
            pl.BlockSpec((1, 1, d), lay),
            pl.BlockSpec((1, 1, RG_CONV, d_rg), ldir),
            pl.BlockSpec((1, 1, d_rg), lay),
            _resident((1, 1, 2, d_rg // 2, d_rg), lambda g: (l, 0, 0, 0, 0)),
            pl.BlockSpec((1, 1, 1, d_rg), ldir),
            pl.BlockSpec((1, 1, 1, d_rg), ldir),
            pl.BlockSpec((1, 1, 1, d_rg), ldir),
        ],
        out_specs=[
            pl.BlockSpec((1, ROW_TILE, d_gm), row),
            pl.BlockSpec((1, ROW_TILE, d_rg), row),
            pl.BlockSpec((1, ROW_TILE, d_rg), row),
            pl.BlockSpec((1, ROW_TILE, d_gm), row),
            pl.BlockSpec((1, ROW_TILE, d_rg), lambda g: (*scan_bt(g), 0)),
        ],
        out_shape=[
            jax.ShapeDtypeStruct((bsz, seq, d_gm), BF16),
            jax.ShapeDtypeStruct((bsz, seq, d_rg), BF16),
            jax.ShapeDtypeStruct((bsz, seq, d_rg), F32),
            jax.ShapeDtypeStruct((bsz, seq, d_gm), BF16),
            jax.ShapeDtypeStruct((bsz, seq, d_rg), F32),
        ],
        scratch_shapes=_scan_scratch_shapes(d_rg),
        compiler_params=_params("arbitrary"),
        name="mix_in_scan",
    )(x, mod4, g_pre, w_in, ln_g, ln_b, ws, bs, g_mix, conv_w, conv_b, wg, ba, bx, lam)


def _mix_out_ffn_kernel(prev_ref, cur_ref, next_ref, cw_ref, cb_ref, wg_ref, ba_ref, bx_ref, lam_ref,
                        hf_ref, x_ref, ygm_ref, gact_ref, yft_ref, m_ref, gmix_ref, ftw_ref, ftb_ref,
                        wout_ref, gpost_ref, gpre2_ref, w1_ref, w2_ref, gpost2_ref,
                        o_ref, *scratch, nt, ntot, ff_chunk):
    scr = _ScanScratch(*scratch)
    g = pl.program_id(0)

    @pl.when(g == 0)
    def _():
        scr.h[...] = jnp.zeros_like(scr.h)
        scr.carry[...] = jnp.zeros_like(scr.carry)

    d_rg = cur_ref.shape[-1]
    d_gm = ygm_ref.shape[-1]
    o_ft = d_gm + d_rg
    d_mix = wout_ref.shape[1]
    h_bwd = jnp.concatenate(
        [jnp.concatenate([_scan_out_block(scr, k, j) for j in range(SCAN_SEGS)], axis=0)
         for k in range(d_rg // V7X_LANES)], axis=1)

    gt1 = m_ref[0, 0, 2:3, :]
    yrg = _rms((hf_ref[0] + h_bwd) * gact_ref[0].astype(F32), gmix_ref[0, :, d_gm:o_ft])
    yft = _dot(yft_ref[0].astype(BF16), ftw_ref[0]) + ftb_ref[0]
    yft = _rms(yft, gmix_ref[0, :, o_ft:d_mix])
    o = _dot(ygm_ref[0], wout_ref[0, 0:d_gm, :])
    o = o + _dot(yrg.astype(BF16), wout_ref[0, d_gm:o_ft, :])
    o = o + _dot(yft.astype(BF16), wout_ref[0, o_ft:d_mix, :])
    x = x_ref[0] + gt1 * _rms(o, gpost_ref[0])

    sh2 = m_ref[0, 0, 3:4, :]
    sc2 = m_ref[0, 0, 4:5, :]
    gt2 = m_ref[0, 0, 5:6, :]
    h = (_rms(x, gpre2_ref[0]) * (1.0 + sc2) + sh2).astype(BF16)

    gs = jnp.minimum(g, ntot - 1) % nt
    ti = nt - 1 - gs

    def fill_ext():
        halo = V7X_SUBLANES
        scr.ext[0:halo, :] = jnp.where(ti == 0, 0.0, prev_ref[0])
        scr.ext[halo:halo + ROW_TILE, :] = cur_ref[0]
        scr.ext[halo + ROW_TILE:halo + ROW_TILE + halo, :] = jnp.where(ti == nt - 1, 0.0, next_ref[0])

    scan = _rg_scan_tile(fill_ext, gs == 0, cw_ref, cb_ref, wg_ref, ba_ref, bx_ref, lam_ref, scr,
                         reverse=True)
    next(scan)
    d_ff = w1_ref.shape[-1]
    acc = None
    for c in range(d_ff // ff_chunk):
        cols = slice(c * ff_chunk, (c + 1) * ff_chunk)
        a = jnp.maximum(_dot(h, w1_ref[0, :, cols]), 0.0)
        part = _dot((a * a).astype(BF16), w2_ref[0, cols, :])
        acc = part if acc is None else acc + part
        next(scan, None)
    for _ in scan:
        pass
    o_ref[0] = x + gt2 * _rms(acc, gpost2_ref[0])


def _mix_out_ffn(l, rgx, conv_w, conv_b, wg, ba, bx, lam, h_fwd, x, ygm, gact, yft, mod4, g_mix, ftw, ftb,
                 w_out, g_post, g_pre2, w1, w2, g_post2):
    bsz, seq, d = x.shape
    d_rg = rgx.shape[-1]
    d_gm = ygm.shape[-1]
    d_ft = yft.shape[-1]
    d_mix = w_out.shape[1]
    d_ff = w1.shape[-1]
    nt = seq // ROW_TILE
    ntot = bsz * nt

    def scan_bt(g):
        gs = jnp.minimum(g, ntot - 1)
        return gs // nt, nt - 1 - gs % nt

    def tail_bt(g):
        gf = jnp.maximum(g - 1, 0)
        return gf // nt, nt - 1 - gf % nt

    lay = lambda g: (l, 0, 0)
    row = lambda g: (*tail_bt(g), 0)
    in_specs = _scan_in_specs(l, 1, scan_bt, seq, d_rg) + [
        pl.BlockSpec((1, ROW_TILE, d_rg), row),
        pl.BlockSpec((1, ROW_TILE, d), row),
        pl.BlockSpec((1, ROW_TILE, d_gm), row),
        pl.BlockSpec((1, ROW_TILE, d_rg), row),
        pl.BlockSpec((1, ROW_TILE, d_ft), row),
        pl.BlockSpec((1, 1, 6, d), lambda g: (l, tail_bt(g)[0], 0, 0)),
        pl.BlockSpec((1, 1, d_mix), lay),
        _resident((1, d_ft, d_ft), lay),
        pl.BlockSpec((1, 1, d_ft), lay),
        _resident((1, d_mix, d), lay),
        pl.BlockSpec((1, 1, d), lay),
        pl.BlockSpec((1, 1, d), lay),
        _resident((1, d, d_ff), lay),
        _resident((1, d_ff, d), lay),
        pl.BlockSpec((1, 1, d), lay),
    ]
    return pl.pallas_call(
        functools.partial(_mix_out_ffn_kernel, nt=nt, ntot=ntot, ff_chunk=d_ff // FF_CHUNKS),
        grid=(ntot + 1,),
        in_specs=in_specs,
        out_specs=pl.BlockSpec((1, ROW_TILE, d), row),
        out_shape=jax.ShapeDtypeStruct((bsz, seq, d), F32),
        scratch_shapes=_scan_scratch_shapes(d_rg),
        compiler_params=_params("arbitrary"),
        name="mix_out_ffn",
    )(rgx, rgx, rgx, conv_w, conv_b, wg, ba, bx, lam, h_fwd, x, ygm, gact, yft, mod4, g_mix, ftw, ftb,
      w_out, g_post, g_pre2, w1, w2, g_post2)


def _block_diag(w, per):
    *lead, n, hd, _ = w.shape
    w = w.reshape(*lead, n // per, per, hd, hd)
    keep = [(0, 0)] * (len(lead) + 2)
    rows = [jnp.pad(w[..., a, :, :], keep + [(a * hd, (per - 1 - a) * hd)]) for a in range(per)]
    return jnp.concatenate(rows, axis=-2)


def kernel(x, c, w_ada, b_ada, g_pre_mix, g_post_mix, w_in, gm_ln_g, gm_ln_b, gm_w_s, gm_b_s,
           rg_conv_w, rg_conv_b, rg_w_a, rg_b_a, rg_w_x, rg_b_x, rg_lam, ft_w, ft_b,
           g_mix_out, w_out, g_pre_ff, g_post_ff, w_ff1, w_ff2):
    bsz, seq, d = x.shape
    depth = w_in.shape[0]
    d_gm = gm_ln_g.shape[-1]
    d_rg = rg_conv_b.shape[-1]
    d_ft = ft_w.shape[1] * ft_w.shape[2]
    assert seq % ROW_TILE == 0 and ROW_TILE % GM_CHUNK == 0 and seq % FT_N1 == 0
    assert d_ft % FT_LANES == 0 and ft_w.shape[2] == FT_GROUP_DIM
    assert rg_conv_w.shape[1] == RG_CONV and rg_w_a.shape[2] == RG_HEADS

    vec = lambda a: a.reshape(depth, 1, a.shape[-1])
    w_in_b = w_in.astype(BF16)
    w_out_b = w_out.astype(BF16)
    w_ff1_b = w_ff1.astype(BF16)
    w_ff2_b = w_ff2.astype(BF16)
    ws = gm_w_s.reshape(depth, GM_HEADS * GM_CHUNK, GM_CHUNK).astype(BF16)
    bs = jnp.repeat(jnp.swapaxes(gm_b_s, 1, 2), d_gm // GM_HEADS, axis=2)
    per = RG_HEADS // 2
    wg = (0.5 * jnp.concatenate([_block_diag(rg_w_a, per), _block_diag(rg_w_x, per)], axis=-1)).astype(BF16)
    ba = 0.5 * rg_b_a.reshape(depth, 2, 1, d_rg)
    bx = 0.5 * rg_b_x.reshape(depth, 2, 1, d_rg)
    lam = rg_lam.reshape(depth, 2, 1, d_rg)
    conv_w = rg_conv_w.reshape(depth, 1, RG_CONV, d_rg)
    ftw = _block_diag(ft_w, FT_GROUPS).reshape(depth, d_ft, d_ft).astype(BF16)
    ftb = ft_b.reshape(depth, 1, d_ft)
    tables = _fourier_tables(seq)

    mod4 = _modulation(c, w_ada, b_ada).reshape(depth, bsz, 6, d)

    for l in range(depth):
        rg = (conv_w, vec(rg_conv_b), wg, ba, bx, lam)
        ygm, gact, rgx, ftx, h_fwd = _mix_in_scan(l, x, mod4, vec(g_pre_mix), w_in_b, vec(gm_ln_g),
                                                  vec(gm_ln_b), ws, bs, vec(g_mix_out), *rg,
                                                  d_gm=d_gm, d_rg=d_rg)
        yft = _fourier(ftx, tables)
        x = _mix_out_ffn(l, rgx, *rg, h_fwd, x, ygm, gact, yft, mod4, vec(g_mix_out), ftw, ftb, w_out_b,
                         vec(g_post_mix), vec(g_pre_ff), w_ff1_b, w_ff2_b, vec(g_post_ff))
    return x
```

```python
import functools
import math

import numpy as np
import jax
import jax.numpy as jnp
from jax import lax
from jax.experimental import pallas as pl
from jax.experimental.pallas import tpu as pltpu

F32 = jnp.float32
BF16 = jnp.bfloat16

EPS = 1e-6
RG_C = 8.0
RG_CONV = 4
RG_CONV_LEFT = 2
GM_HEADS = 4
GM_CHUNK = 128
RG_HEADS = 8
FT_GROUPS = 4
FT_GROUP_DIM = 64

V7X_SUBLANES = 8
V7X_LANES = 128
V7X_VMEM_LIMIT_BYTES = 56 * 1024 * 1024

ROW_TILE = 512
FF_CHUNKS = 8
SCAN_ROW_PARTS = 2
SCAN_STAGES = 3 * SCAN_ROW_PARTS + 2
SCAN_SEGS = V7X_SUBLANES
SCAN_SEG_LEN = ROW_TILE // SCAN_SEGS
SCAN_SEG_STRIDE = SCAN_SEG_LEN + V7X_SUBLANES

FT_N1 = 64
FT_PAD = FT_N1 + V7X_SUBLANES
FT_LANES = 128
FT_UNROLL_A = 4
FT_UNROLL_B = 16


def _gelu(x):
    return 0.5 * x * (1.0 + jnp.tanh(math.sqrt(2.0 / math.pi) * (x + 0.044715 * (x * x * x))))


def _sigmoid(x):
    return 0.5 * jnp.tanh(0.5 * x) + 0.5


def _rms(x, g):
    return x * lax.rsqrt(jnp.mean(x * x, axis=-1, keepdims=True) + EPS) * g


def _dot(a, b):
    return jnp.dot(a, b, preferred_element_type=F32)


def _params(*sem):
    return pltpu.CompilerParams(dimension_semantics=sem, vmem_limit_bytes=V7X_VMEM_LIMIT_BYTES)


def _resident(shape, index_map):
    return pl.BlockSpec(shape, index_map, pipeline_mode=pl.Buffered(1))


def _mod_kernel(ct_ref, w_ref, b_ref, o_ref):
    ct = ct_ref[...]
    cond = ct * _sigmoid(ct)
    w = w_ref[0]
    for b in range(ct.shape[1]):
        o_ref[0, b:b + 1, :] = jnp.sum(w * cond[:, b:b + 1], axis=0, keepdims=True) + b_ref[0]


def _modulation(c, w_ada, b_ada):
    depth, d, d6 = w_ada.shape
    bsz = c.shape[0]
    nblk = d6 // d
    return pl.pallas_call(
        _mod_kernel,
        grid=(depth, nblk),
        in_specs=[
            pl.BlockSpec((d, bsz), lambda l, j: (0, 0)),
            pl.BlockSpec((1, d, d), lambda l, j: (l, 0, j)),
            pl.BlockSpec((1, 1, d), lambda l, j: (l, 0, j)),
        ],
        out_specs=pl.BlockSpec((1, bsz, d), lambda l, j: (l, 0, j)),
        out_shape=jax.ShapeDtypeStruct((depth, bsz, d6), F32),
        compiler_params=_params("arbitrary", "arbitrary"),
        name="modulation",
    )(c.T, w_ada, b_ada.reshape(depth, 1, d6))


def _fourier_tables(seq):
    n1 = FT_N1
    n2 = seq // n1
    gd = FT_GROUP_DIM
    j = np.arange(gd)
    ang = 2.0 * np.pi * np.outer(j, j) / gd
    eye = np.eye(FT_LANES // gd)
    w1 = np.concatenate([np.kron(eye, np.cos(ang)), -np.kron(eye, np.sin(ang))], axis=1)
    k2 = np.arange(n2)[None, :, None]
    s2 = np.arange(n2)[None, None, :]
    s1 = np.arange(n1)[:, None, None]
    ang_a = 2.0 * np.pi * ((k2 * (n1 * s2 + s1)) % seq) / seq
    ta = np.concatenate([np.cos(ang_a), np.sin(ang_a)], axis=2)
    i1 = np.arange(n1)
    ang_b = 2.0 * np.pi * np.outer(i1, i1) / n1
    tb = np.concatenate([np.cos(ang_b), np.sin(ang_b)], axis=1)
    return (jnp.asarray(w1, F32), jnp.asarray(ta, F32), jnp.asarray(tb, F32))


def _fourier_kernel(x_ref, w1_ref, ta_ref, tb_ref, o_ref, p_ref, *, seq, scale):
    n1 = FT_N1
    n2 = seq // n1
    ln = FT_LANES

    w1 = w1_ref[...].astype(BF16)
    for c in range(seq // ROW_TILE):
        p = _dot(x_ref[0, c * ROW_TILE:(c + 1) * ROW_TILE, :].astype(BF16), w1)
        for r in range(ROW_TILE // n1):
            dst = pl.ds((c * (ROW_TILE // n1) + r) * FT_PAD, n1)
            p_ref[0, dst, :] = p[r * n1:(r + 1) * n1, 0:ln]
            p_ref[1, dst, :] = p[r * n1:(r + 1) * n1, ln:2 * ln]

    def stage_a(u, carry):
        res = []
        for d in range(FT_UNROLL_A):
            s1 = u * FT_UNROLL_A + d
            rows = pl.ds(s1, n2, stride=FT_PAD)
            g = jnp.concatenate([p_ref[0, rows, :], p_ref[1, rows, :]], axis=0).astype(BF16)
            t = ta_ref[s1]
            t_im = jnp.concatenate([-t[:, n2:2 * n2], t[:, 0:n2]], axis=1)
            res.append((rows, _dot(t.astype(BF16), g), _dot(t_im.astype(BF16), g)))
        for rows, b_re, b_im in res:
            p_ref[0, rows, :] = b_re
            p_ref[1, rows, :] = b_im
        return carry

    lax.fori_loop(0, n1 // FT_UNROLL_A, stage_a, 0)

    tb = tb_ref[...].astype(BF16)

    def stage_b(u, carry):
        for d in range(FT_UNROLL_B):
            k2 = u * FT_UNROLL_B + d
            rows = pl.ds(pl.multiple_of(k2 * FT_PAD, V7X_SUBLANES), n1)
            blk = jnp.concatenate([p_ref[0, rows, :], p_ref[1, rows, :]], axis=0).astype(BF16)
            o_ref[0, pl.ds(k2, n1, stride=n2), :] = _dot(tb, blk) * scale
        return carry

    lax.fori_loop(0, n2 // FT_UNROLL_B, stage_b, 0)


def _fourier(ftx, tables):
    bsz, seq, d_ft = ftx.shape
    w1, ta, tb = tables
    n1 = FT_N1
    n2 = seq // n1
    blk = pl.BlockSpec((1, seq, FT_LANES), lambda b, j: (b, 0, j))
    return pl.pallas_call(
        functools.partial(_fourier_kernel, seq=seq, scale=1.0 / math.sqrt(seq * FT_GROUP_DIM)),
        grid=(bsz, d_ft // FT_LANES),
        in_specs=[
            blk,
            pl.BlockSpec((FT_LANES, 2 * FT_LANES), lambda b, j: (0, 0)),
            _resident((n1, n2, 2 * n2), lambda b, j: (0, 0, 0)),
            pl.BlockSpec((n1, 2 * n1), lambda b, j: (0, 0)),
        ],
        out_specs=blk,
        out_shape=jax.ShapeDtypeStruct((bsz, seq, d_ft), F32),
        scratch_shapes=[pltpu.VMEM((2, n2 * FT_PAD, FT_LANES), F32)],
        compiler_params=_params("arbitrary", "arbitrary"),
        name="fourier",
    )(ftx, w1, ta, tb)


class _ScanScratch:
    def __init__(self, ext, a, b, h, e, p, c, carry):
        self.ext, self.a, self.b, self.h = ext, a, b, h
        self.e, self.p, self.c, self.carry = e, p, c, carry


def _scan_scratch_shapes(d_rg):
    nblk = d_rg // V7X_LANES
    pad_rows = SCAN_SEGS * SCAN_SEG_STRIDE
    return [
        pltpu.VMEM((ROW_TILE + 2 * V7X_SUBLANES, d_rg), F32),
        pltpu.VMEM((nblk, ROW_TILE, V7X_LANES), F32),
        pltpu.VMEM((nblk, ROW_TILE, V7X_LANES), F32),
        pltpu.VMEM((nblk, pad_rows, V7X_LANES), F32),
        pltpu.VMEM((SCAN_SEGS, d_rg), F32),
        pltpu.VMEM((SCAN_SEGS, d_rg), F32),
        pltpu.VMEM((SCAN_SEGS, d_rg), F32),
        pltpu.VMEM((1, d_rg), F32),
    ]


def _scan_in_specs(l, dirn, bt, seq, d_rg):
    halo = V7X_SUBLANES
    per = ROW_TILE // halo
    nh = seq // halo
    lay = lambda *g: (l, 0, 0)
    ldir = lambda *g: (l, dirn, 0, 0)

    def prev_halo(*g):
        b, t = bt(*g)
        return (b, jnp.maximum(t * per - 1, 0), 0)

    def next_halo(*g):
        b, t = bt(*g)
        return (b, jnp.minimum((t + 1) * per, nh - 1), 0)

    return [
        pl.BlockSpec((1, halo, d_rg), prev_halo),
        pl.BlockSpec((1, ROW_TILE, d_rg), lambda *g: (*bt(*g), 0)),
        pl.BlockSpec((1, halo, d_rg), next_halo),
        pl.BlockSpec((1, 1, RG_CONV, d_rg), lambda *g: (l, 0, 0, 0)),
        pl.BlockSpec((1, 1, d_rg), lay),
        _resident((1, 1, 2, d_rg // 2, d_rg), lambda *g: (l, dirn, 0, 0, 0)),
        pl.BlockSpec((1, 1, 1, d_rg), ldir),
        pl.BlockSpec((1, 1, 1, d_rg), ldir),
        pl.BlockSpec((1, 1, 1, d_rg), ldir),
    ]


def _rg_scan_tile(fill_ext, first_step, cw_ref, cb_ref, wg_ref, ba_ref, bx_ref, lam_ref, scr, *, reverse):
    tm = ROW_TILE
    halo = V7X_SUBLANES
    d_rg = scr.ext.shape[-1]
    half = d_rg // 2
    nblk = d_rg // V7X_LANES

    fill_ext()
    nl = -lam_ref[0, 0]
    c8h = (-0.5 * RG_C) * (jnp.maximum(nl, 0.0) + jnp.log1p(jnp.exp(-jnp.abs(nl))))

    rh = tm // SCAN_ROW_PARTS
    segs_per_part = SCAN_SEGS // SCAN_ROW_PARTS
    for part in range(SCAN_ROW_PARTS):
        xr = cb_ref[0]
        for k in range(RG_CONV):
            lo = halo - RG_CONV_LEFT + k + part * rh
            xr = xr + scr.ext[lo:lo + rh, :] * cw_ref[0, 0, k:k + 1, :]
        xrb = xr.astype(BF16)
        yield
        for hh in range(2):
            cols = slice(hh * half, (hh + 1) * half)
            pre = _dot(xrb[:, cols], wg_ref[0, 0, hh])
            t_r = jnp.tanh(pre[:, 0:half] + ba_ref[0, 0, :, cols])
            t_i = jnp.tanh(pre[:, half:2 * half] + bx_ref[0, 0, :, cols])
            log_a = c8h[:, cols] * t_r + c8h[:, cols]
            a = jnp.exp(log_a)
            th = jnp.tanh(log_a)
            nth = -0.5 * th
            sq = jnp.where(nth > 0.0, nth * lax.rsqrt(nth), 0.0) * lax.rsqrt(1.0 - th)
            bq = sq * ((t_i + 1.0) * xr[:, cols])
            for kk in range(half // V7X_LANES):
                k = hh * (half // V7X_LANES) + kk
                lanes = slice(kk * V7X_LANES, (kk + 1) * V7X_LANES)
                for jj in range(segs_per_part):
                    src = slice(jj * SCAN_SEG_LEN, (jj + 1) * SCAN_SEG_LEN)
                    dst = pl.ds(part * segs_per_part + jj, SCAN_SEG_LEN, stride=SCAN_SEGS)
                    scr.a[k, dst, :] = a[src, lanes]
                    scr.b[k, dst, :] = bq[src, lanes]
            yield

    steps = range(SCAN_SEG_LEN - 1, -1, -1) if reverse else range(SCAN_SEG_LEN)

    def at_step(t):
        return slice(t * SCAN_SEGS, (t + 1) * SCAN_SEGS)

    for k in range(nblk):
        lanes = slice(k * V7X_LANES, (k + 1) * V7X_LANES)
        e = jnp.zeros((SCAN_SEGS, V7X_LANES), F32)
        p = e + 1.0
        for t in steps:
            at = scr.a[k, at_step(t), :]
            e = at * e + scr.b[k, at_step(t), :]
            p = at * p
        scr.e[:, lanes] = e
        scr.p[:, lanes] = p
    yield

    c = jnp.where(first_step, 0.0, scr.carry[...])
    order = range(SCAN_SEGS - 1, -1, -1) if reverse else range(SCAN_SEGS)
    for j in order:
        scr.c[j:j + 1, :] = c
        c = scr.p[j:j + 1, :] * c + scr.e[j:j + 1, :]
    scr.carry[...] = c

    for k in range(nblk):
        h = scr.c[:, k * V7X_LANES:(k + 1) * V7X_LANES]
        for t in steps:
            h = scr.a[k, at_step(t), :] * h + scr.b[k, at_step(t), :]
            scr.h[k, pl.ds(t, SCAN_SEGS, stride=SCAN_SEG_STRIDE), :] = h


def _advance(stages, n):
    for _ in range(n):
        next(stages, None)


def _scan_out_block(scr, k, j):
    return scr.h[k, j * SCAN_SEG_STRIDE:j * SCAN_SEG_STRIDE + SCAN_SEG_LEN, :]


def _mix_in_scan_kernel(x_ref, m_ref, gpre_ref, win_ref, lng_ref, lnb_ref, ws_ref, bs_ref, gmix_ref,
                        cw_ref, cb_ref, wg_ref, ba_ref, bx_ref, lam_ref,
                        ygm_ref, gact_ref, rgx_ref, ftx_ref, hf_ref, *scratch, nt, ntot, d_gm, d_rg):
    scr = _ScanScratch(*scratch)
    g = pl.program_id(0)
    tm = ROW_TILE
    halo = V7X_SUBLANES
    o_v, o_g, o_x, o_f = d_gm, 2 * d_gm, 2 * d_gm + d_rg, 2 * d_gm + 2 * d_rg

    def step(project, scan_on):
        gs = (g - 1) % nt
        if project:
            x = x_ref[0]
            sh1 = m_ref[0, 0, 0:1, :]
            sc1 = m_ref[0, 0, 1:2, :]
            h = (_rms(x, gpre_ref[0]) * (1.0 + sc1) + sh1).astype(BF16)
            rgx = _dot(h, win_ref[0, :, o_x:o_f])
            rgx_ref[0] = rgx

        def fill_ext():
            nxt = jnp.where(gs == nt - 1, 0.0, rgx[0:halo]) if project else jnp.zeros((halo, d_rg), F32)
            scr.ext[halo + tm:halo + tm + halo, :] = nxt

        if scan_on:
            scan = _rg_scan_tile(fill_ext, gs == 0, cw_ref, cb_ref, wg_ref, ba_ref, bx_ref, lam_ref, scr,
                                 reverse=False)
        else:
            scr.carry[...] = jnp.zeros_like(scr.carry)
            scan = iter(())
        if not project:
            for _ in scan:
                pass
        else:
            _advance(scan, 1)
            v = _gelu(_dot(h, win_ref[0, :, o_v:o_g]))
            mu = jnp.mean(v, axis=-1, keepdims=True)
            vc = v - mu
            var = jnp.mean(vc * vc, axis=-1, keepdims=True)
            vn = (vc * lax.rsqrt(var + EPS) * lng_ref[0] + lnb_ref[0]).astype(BF16)
            _advance(scan, 1)
            gact_ref[0] = _gelu(_dot(h, win_ref[0, :, o_g:o_x])).astype(BF16)
            _advance(scan, 3 * (SCAN_ROW_PARTS - 1) + 1 - 2)
            if scan_on:
                scr.ext[0:halo, :] = jnp.where(g % nt == 0, 0.0, scr.ext[tm:tm + halo, :])
            else:
                scr.ext[0:halo, :] = jnp.zeros((halo, d_rg), F32)
            scr.ext[halo:halo + tm, :] = rgx
            ftx_ref[0] = _dot(h, win_ref[0, :, o_f:o_f + d_gm]).astype(BF16)
            _advance(scan, 1)
            u = _gelu(_dot(h, win_ref[0, :, 0:o_v]))
            _advance(scan, 1)
            head_dim = d_gm // GM_HEADS
            head = lax.broadcasted_iota(jnp.int32, (GM_CHUNK, d_gm), 1) // head_dim
            for c in range(tm // GM_CHUNK):
                rows = slice(c * GM_CHUNK, (c + 1) * GM_CHUNK)
                r = _dot(ws_ref[0], vn[rows])
                s = r[0:GM_CHUNK]
                for hh in range(1, GM_HEADS):
                    s = jnp.where(head == hh, r[hh * GM_CHUNK:(hh + 1) * GM_CHUNK], s)
                y = u[rows] * (s + bs_ref[0])
                ygm_ref[0, rows, :] = _rms(y, gmix_ref[0, :, 0:d_gm]).astype(BF16)
                if c == 1:
                    _advance(scan, 1)
            for _ in scan:
                pass
        if scan_on:
            for k in range(d_rg // V7X_LANES):
                for j in range(SCAN_SEGS):
                    hf_ref[0, j * SCAN_SEG_LEN:(j + 1) * SCAN_SEG_LEN,
                           k * V7X_LANES:(k + 1) * V7X_LANES] = _scan_out_block(scr, k, j)

    pl.when(g == 0)(functools.partial(step, True, False))
    pl.when((g > 0) & (g < ntot))(functools.partial(step, True, True))
    pl.when(g == ntot)(functools.partial(step, False, True))


def _mix_in_scan(l, x, mod4, g_pre, w_in, ln_g, ln_b, ws, bs, g_mix, conv_w, conv_b, wg, ba, bx, lam,
                 *, d_gm, d_rg):
    bsz, seq, d = x.shape
    d_in = w_in.shape[-1]
    nt = seq // ROW_TILE
    ntot = bsz * nt

    def proj_bt(g):
        gp = jnp.minimum(g, ntot - 1)
        return gp // nt, gp % nt

    def scan_bt(g):
        gs = jnp.maximum(g - 1, 0)
        return gs // nt, gs % nt

    lay = lambda g: (l, 0, 0)
    ldir = lambda g: (l, 0, 0, 0)
    row = lambda g: (*proj_bt(g), 0)
    return pl.pallas_call(
        functools.partial(_mix_in_scan_kernel, nt=nt, ntot=ntot, d_gm=d_gm, d_rg=d_rg),
        grid=(ntot + 1,),
        in_specs=[
            pl.BlockSpec((1, ROW_TILE, d), row),
            pl.BlockSpec((1, 1, 6, d), lambda g: (l, proj_bt(g)[0], 0, 0)),
            pl.BlockSpec((1, 1, d), lay),
            _resident((1, d, d_in), lay),
            pl.BlockSpec((1, 1, d_gm), lay),
            pl.BlockSpec((1, 1, d_gm), lay),
            pl.BlockSpec((1, GM_HEADS * GM_CHUNK, GM_CHUNK), lay),
            pl.BlockSpec((1, GM_CHUNK, d_gm), lay),
            pl.BlockSpec((1, 1, d), lay),
            pl.BlockSpec((1, 1, RG_CONV, d_rg), ldir),
            pl.BlockSpec((1, 1, d_rg), lay),
            _resident((1, 1, 2, d_rg // 2, d_rg), lambda g: (l, 0, 0, 0, 0)),
            pl.BlockSpec((1, 1, 1, d_rg), ldir),
            pl.BlockSpec((1, 1, 1, d_rg), ldir),
            pl.BlockSpec((1, 1, 1, d_rg), ldir),
        ],
        out_specs=[
            pl.BlockSpec((1, ROW_TILE, d_gm), row),
            pl.BlockSpec((1, ROW_TILE, d_rg), row),
            pl.BlockSpec((1, ROW_TILE, d_rg), row),
            pl.BlockSpec((1, ROW_TILE, d_gm), row),
            pl.BlockSpec((1, ROW_TILE, d_rg), lambda g: (*scan_bt(g), 0)),
        ],
        out_shape=[
            jax.ShapeDtypeStruct((bsz, seq, d_gm), BF16),
            jax.ShapeDtypeStruct((bsz, seq, d_rg), BF16),
            jax.ShapeDtypeStruct((bsz, seq, d_rg), F32),
            jax.ShapeDtypeStruct((bsz, seq, d_gm), BF16),
            jax.ShapeDtypeStruct((bsz, seq, d_rg), F32),
        ],
        scratch_shapes=_scan_scratch_shapes(d_rg),
        compiler_params=_params("arbitrary"),
        name="mix_in_scan",
    )(x, mod4, g_pre, w_in, ln_g, ln_b, ws, bs, g_mix, conv_w, conv_b, wg, ba, bx, lam)


def _mix_out_ffn_kernel(prev_ref, cur_ref, next_ref, cw_ref, cb_ref, wg_ref, ba_ref, bx_ref, lam_ref,
                        hf_ref, x_ref, ygm_ref, gact_ref, yft_ref, m_ref, gmix_ref, ftw_ref, ftb_ref,
                        wout_ref, gpost_ref, gpre2_ref, w1_ref, w2_ref, gpost2_ref,
                        o_ref, *scratch, nt, ntot, ff_chunk):
    scr = _ScanScratch(*scratch)
    g = pl.program_id(0)
    d_rg = cur_ref.shape[-1]
    d_gm = ygm_ref.shape[-1]
    o_ft = d_gm + d_rg
    d_mix = wout_ref.shape[1]
    gs = jnp.minimum(g, ntot - 1) % nt
    ti = nt - 1 - gs

    def fill_ext():
        halo = V7X_SUBLANES
        scr.ext[0:halo, :] = jnp.where(ti == 0, 0.0, prev_ref[0])
        scr.ext[halo:halo + ROW_TILE, :] = cur_ref[0]
        scr.ext[halo + ROW_TILE:halo + ROW_TILE + halo, :] = jnp.where(ti == nt - 1, 0.0, next_ref[0])

    def scan_stages():
        return _rg_scan_tile(fill_ext, gs == 0, cw_ref, cb_ref, wg_ref, ba_ref, bx_ref, lam_ref, scr,
                             reverse=True)

    @pl.when(g == 0)
    def _():
        scr.carry[...] = jnp.zeros_like(scr.carry)
        for _ in scan_stages():
            pass

    @pl.when(g > 0)
    def _():
        h_bwd = jnp.concatenate(
            [jnp.concatenate([_scan_out_block(scr, k, j) for j in range(SCAN_SEGS)], axis=0)
             for k in range(d_rg // V7X_LANES)], axis=1)

        gt1 = m_ref[0, 0, 2:3, :]
        yrg = _rms((hf_ref[0] + h_bwd) * gact_ref[0].astype(F32), gmix_ref[0, :, d_gm:o_ft])
        yft = _dot(yft_ref[0].astype(BF16), ftw_ref[0]) + ftb_ref[0]
        yft = _rms(yft, gmix_ref[0, :, o_ft:d_mix])
        o = _dot(ygm_ref[0], wout_ref[0, 0:d_gm, :])
        o = o + _dot(yrg.astype(BF16), wout_ref[0, d_gm:o_ft, :])
        o = o + _dot(yft.astype(BF16), wout_ref[0, o_ft:d_mix, :])
        x = x_ref[0] + gt1 * _rms(o, gpost_ref[0])

        sh2 = m_ref[0, 0, 3:4, :]
        sc2 = m_ref[0, 0, 4:5, :]
        gt2 = m_ref[0, 0, 5:6, :]
        h = (_rms(x, gpre2_ref[0]) * (1.0 + sc2) + sh2).astype(BF16)

        scan = scan_stages()
        next(scan)
        d_ff = w1_ref.shape[-1]
        acc = None
        for c in range(d_ff // ff_chunk):
            cols = slice(c * ff_chunk, (c + 1) * ff_chunk)
            a = jnp.maximum(_dot(h, w1_ref[0, :, cols]), 0.0)
            part = _dot((a * a).astype(BF16), w2_ref[0, cols, :])
            acc = part if acc is None else acc + part
            _advance(scan, pl.cdiv(SCAN_STAGES - 1, d_ff // ff_chunk))
        for _ in scan:
            pass
        o_ref[0] = x + gt2 * _rms(acc, gpost2_ref[0])


def _mix_out_ffn(l, rgx, conv_w, conv_b, wg, ba, bx, lam, h_fwd, x, ygm, gact, yft, mod4, g_mix, ftw, ftb,
                 w_out, g_post, g_pre2, w1, w2, g_post2):
    bsz, seq, d = x.shape
    d_rg = rgx.shape[-1]
    d_gm = ygm.shape[-1]
    d_ft = yft.shape[-1]
    d_mix = w_out.shape[1]
    d_ff = w1.shape[-1]
    nt = seq // ROW_TILE
    ntot = bsz * nt

    def scan_bt(g):
        gs = jnp.minimum(g, ntot - 1)
        return gs // nt, nt - 1 - gs % nt

    def tail_bt(g):
        gf = jnp.maximum(g - 1, 0)
        return gf // nt, nt - 1 - gf % nt

    lay = lambda g: (l, 0, 0)
    row = lambda g: (*tail_bt(g), 0)
    in_specs = _scan_in_specs(l, 1, scan_bt, seq, d_rg) + [
        pl.BlockSpec((1, ROW_TILE, d_rg), row),
        pl.BlockSpec((1, ROW_TILE, d), row),
        pl.BlockSpec((1, ROW_TILE, d_gm), row),
        pl.BlockSpec((1, ROW_TILE, d_rg), row),
        pl.BlockSpec((1, ROW_TILE, d_ft), row),
        pl.BlockSpec((1, 1, 6, d), lambda g: (l, tail_bt(g)[0], 0, 0)),
        pl.BlockSpec((1, 1, d_mix), lay),
        _resident((1, d_ft, d_ft), lay),
        pl.BlockSpec((1, 1, d_ft), lay),
        _resident((1, d_mix, d), lay),
        pl.BlockSpec((1, 1, d), lay),
        pl.BlockSpec((1, 1, d), lay),
        _resident((1, d, d_ff), lay),
        _resident((1, d_ff, d), lay),
        pl.BlockSpec((1, 1, d), lay),
    ]
    return pl.pallas_call(
        functools.partial(_mix_out_ffn_kernel, nt=nt, ntot=ntot, ff_chunk=d_ff // FF_CHUNKS),
        grid=(ntot + 1,),
        in_specs=in_specs,
        out_specs=pl.BlockSpec((1, ROW_TILE, d), row),
        out_shape=jax.ShapeDtypeStruct((bsz, seq, d), F32),
        scratch_shapes=_scan_scratch_shapes(d_rg),
        compiler_params=_params("arbitrary"),
        name="mix_out_ffn",
    )(rgx, rgx, rgx, conv_w, conv_b, wg, ba, bx, lam, h_fwd, x, ygm, gact, yft, mod4, g_mix, ftw, ftb,
      w_out, g_post, g_pre2, w1, w2, g_post2)


def _block_diag(w, per):
    *lead, n, hd, _ = w.shape
    w = w.reshape(*lead, n // per, per, hd, hd)
    keep = [(0, 0)] * (len(lead) + 2)
    rows = [jnp.pad(w[..., a, :, :], keep + [(a * hd, (per - 1 - a) * hd)]) for a in range(per)]
    return jnp.concatenate(rows, axis=-2)


def kernel(x, c, w_ada, b_ada, g_pre_mix, g_post_mix, w_in, gm_ln_g, gm_ln_b, gm_w_s, gm_b_s,
           rg_conv_w, rg_conv_b, rg_w_a, rg_b_a, rg_w_x, rg_b_x, rg_lam, ft_w, ft_b,
           g_mix_out, w_out, g_pre_ff, g_post_ff, w_ff1, w_ff2):
    bsz, seq, d = x.shape
    depth = w_in.shape[0]
    d_gm = gm_ln_g.shape[-1]
    d_rg = rg_conv_b.shape[-1]
    d_ft = ft_w.shape[1] * ft_w.shape[2]
    assert seq % ROW_TILE == 0 and ROW_TILE % GM_CHUNK == 0 and seq % FT_N1 == 0
    assert d_ft % FT_LANES == 0 and ft_w.shape[2] == FT_GROUP_DIM
    assert rg_conv_w.shape[1] == RG_CONV and rg_w_a.shape[2] == RG_HEADS

    vec = lambda a: a.reshape(depth, 1, a.shape[-1])
    w_in_b = w_in.astype(BF16)
    w_out_b = w_out.astype(BF16)
    w_ff1_b = w_ff1.astype(BF16)
    w_ff2_b = w_ff2.astype(BF16)
    ws = gm_w_s.reshape(depth, GM_HEADS * GM_CHUNK, GM_CHUNK).astype(BF16)
    bs = jnp.repeat(jnp.swapaxes(gm_b_s, 1, 2), d_gm // GM_HEADS, axis=2)
    per = RG_HEADS // 2
    wg = (0.5 * jnp.concatenate([_block_diag(rg_w_a, per), _block_diag(rg_w_x, per)], axis=-1)).astype(BF16)
    ba = 0.5 * rg_b_a.reshape(depth, 2, 1, d_rg)
    bx = 0.5 * rg_b_x.reshape(depth, 2, 1, d_rg)
    lam = rg_lam.reshape(depth, 2, 1, d_rg)
    conv_w = rg_conv_w.reshape(depth, 1, RG_CONV, d_rg)
    ftw = _block_diag(ft_w, FT_GROUPS).reshape(depth, d_ft, d_ft).astype(BF16)
    ftb = ft_b.reshape(depth, 1, d_ft)
    tables = _fourier_tables(seq)

    mod4 = _modulation(c, w_ada, b_ada).reshape(depth, bsz, 6, d)

    for l in range(depth):
        rg = (conv_w, vec(rg_conv_b), wg, ba, bx, lam)
        ygm, gact, rgx, ftx, h_fwd = _mix_in_scan(l, x, mod4, vec(g_pre_mix), w_in_b, vec(gm_ln_g),
                                                  vec(gm_ln_b), ws, bs, vec(g_mix_out), *rg,
                                                  d_gm=d_gm, d_rg=d_rg)
        yft = _fourier(ftx, tables)
        x = _mix_out_ffn(l, rgx, *rg, h_fwd, x, ygm, gact, yft, mod4, vec(g_mix_out), ftw, ftb, w_out_b,
                         vec(g_post_mix), vec(g_pre_ff), w_ff1_b, w_ff2_b, vec(g_post_ff))
    return x
```

```python
import functools
import math

import numpy as np
import jax
import jax.numpy as jnp
from jax import lax
from jax.experimental import pallas as pl
from jax.experimental.pallas import tpu as pltpu

F32 = jnp.float32
BF16 = jnp.bfloat16

EPS = 1e-6
RG_C = 8.0
RG_CONV = 4
RG_CONV_LEFT = 2
GM_HEADS = 4
GM_CHUNK = 128
RG_HEADS = 8
FT_GROUPS = 4
FT_GROUP_DIM = 64

V7X_SUBLANES = 8
V7X_LANES = 128
V7X_VMEM_LIMIT_BYTES = 56 * 1024 * 1024

ROW_TILE = 512
FF_CHUNKS = 8
SCAN_ROW_PARTS = 2
SCAN_STAGES = 3 * SCAN_ROW_PARTS + 2
SCAN_SEGS = V7X_SUBLANES
SCAN_SEG_LEN = ROW_TILE // SCAN_SEGS
SCAN_SEG_STRIDE = SCAN_SEG_LEN + V7X_SUBLANES

FT_N1 = 64
FT_PAD = FT_N1 + V7X_SUBLANES
FT_LANES = 128
FT_OUT_PAD = V7X_SUBLANES
FT_UNROLL_A = 4
FT_UNROLL_B = 16


def _gelu(x):
    c = math.sqrt(2.0 / math.pi)
    t = jnp.tanh(x * ((c * 0.044715) * (x * x) + c))
    return (0.5 * x) * (t + 1.0)


def _sigmoid(x):
    return 0.5 * jnp.tanh(0.5 * x) + 0.5


def _rms(x, g):
    return x * lax.rsqrt(jnp.mean(x * x, axis=-1, keepdims=True) + EPS) * g


def _dot(a, b):
    return jnp.dot(a, b, preferred_element_type=F32)


def _params(*sem):
    return pltpu.CompilerParams(dimension_semantics=sem, vmem_limit_bytes=V7X_VMEM_LIMIT_BYTES)


def _resident(shape, index_map):
    return pl.BlockSpec(shape, index_map, pipeline_mode=pl.Buffered(1))


def _mod_kernel(ct_ref, w_ref, b_ref, o_ref):
    ct = ct_ref[...]
    cond = ct * _sigmoid(ct)
    w = w_ref[0]
    for b in range(ct.shape[1]):
        o_ref[0, b:b + 1, :] = jnp.sum(w * cond[:, b:b + 1], axis=0, keepdims=True) + b_ref[0]


def _modulation(c, w_ada, b_ada):
    depth, d, d6 = w_ada.shape
    bsz = c.shape[0]
    nblk = d6 // d
    return pl.pallas_call(
        _mod_kernel,
        grid=(depth, nblk),
        in_specs=[
            pl.BlockSpec((d, bsz), lambda l, j: (0, 0)),
            pl.BlockSpec((1, d, d), lambda l, j: (l, 0, j)),
            pl.BlockSpec((1, 1, d), lambda l, j: (l, 0, j)),
        ],
        out_specs=pl.BlockSpec((1, bsz, d), lambda l, j: (l, 0, j)),
        out_shape=jax.ShapeDtypeStruct((depth, bsz, d6), F32),
        compiler_params=_params("arbitrary", "arbitrary"),
        name="modulation",
    )(c.T, w_ada, b_ada.reshape(depth, 1, d6))


def _fourier_tables(seq):
    n1 = FT_N1
    n2 = seq // n1
    gd = FT_GROUP_DIM
    j = np.arange(gd)
    ang = 2.0 * np.pi * np.outer(j, j) / gd
    eye = np.eye(FT_LANES // gd)
    w1 = np.concatenate([np.kron(eye, np.cos(ang)), -np.kron(eye, np.sin(ang))], axis=1)
    k2 = np.arange(n2)[None, :, None]
    s2 = np.arange(n2)[None, None, :]
    s1 = np.arange(n1)[:, None, None]
    ang_a = 2.0 * np.pi * ((k2 * (n1 * s2 + s1)) % seq) / seq
    ta = np.concatenate([np.cos(ang_a), np.sin(ang_a)], axis=2)
    i1 = np.arange(n1)
    ang_b = 2.0 * np.pi * np.outer(i1, i1) / n1
    tb = np.concatenate([np.cos(ang_b), np.sin(ang_b)], axis=1)
    return (jnp.asarray(w1, F32), jnp.asarray(ta, F32), jnp.asarray(tb, F32))


def _fourier_kernel(x_ref, w1_ref, ta_ref, tb_ref, o_ref, p_ref, *, seq, scale):
    n1 = FT_N1
    n2 = seq // n1
    ln = FT_LANES

    w1 = w1_ref[...].astype(BF16)
    for c in range(seq // ROW_TILE):
        p = _dot(x_ref[0, c * ROW_TILE:(c + 1) * ROW_TILE, :].astype(BF16), w1)
        for r in range(ROW_TILE // n1):
            dst = pl.ds((c * (ROW_TILE // n1) + r) * FT_PAD, n1)
            p_ref[0, dst, :] = p[r * n1:(r + 1) * n1, 0:ln]
            p_ref[1, dst, :] = p[r * n1:(r + 1) * n1, ln:2 * ln]

    def stage_a(u, carry):
        res = []
        for d in range(FT_UNROLL_A):
            s1 = u * FT_UNROLL_A + d
            rows = pl.ds(s1, n2, stride=FT_PAD)
            g = jnp.concatenate([p_ref[0, rows, :], p_ref[1, rows, :]], axis=0).astype(BF16)
            t = ta_ref[s1]
            t_im = jnp.concatenate([-t[:, n2:2 * n2], t[:, 0:n2]], axis=1)
            res.append((rows, _dot(t.astype(BF16), g), _dot(t_im.astype(BF16), g)))
        for rows, b_re, b_im in res:
            p_ref[0, rows, :] = b_re
            p_ref[1, rows, :] = b_im
        return carry

    lax.fori_loop(0, n1 // FT_UNROLL_A, stage_a, 0)

    tb = tb_ref[...].astype(BF16)
    pitch = n2 + FT_OUT_PAD
    for k1 in range(n1):
        o_ref[0, k1 * pitch + n2:(k1 + 1) * pitch, :] = jnp.zeros((FT_OUT_PAD, ln), F32)

    def stage_b(u, carry):
        for d in range(FT_UNROLL_B):
            k2 = u * FT_UNROLL_B + d
            rows = pl.ds(pl.multiple_of(k2 * FT_PAD, V7X_SUBLANES), n1)
            blk = jnp.concatenate([p_ref[0, rows, :], p_ref[1, rows, :]], axis=0).astype(BF16)
            o_ref[0, pl.ds(k2, n1, stride=pitch), :] = _dot(tb, blk) * scale
        return carry

    lax.fori_loop(0, n2 // FT_UNROLL_B, stage_b, 0)


def _fourier(ftx, tables):
    bsz, seq, d_ft = ftx.shape
    w1, ta, tb = tables
    n1 = FT_N1
    n2 = seq // n1
    blk = pl.BlockSpec((1, seq, FT_LANES), lambda b, j: (b, 0, j))
    return pl.pallas_call(
        functools.partial(_fourier_kernel, seq=seq, scale=1.0 / math.sqrt(seq * FT_GROUP_DIM)),
        grid=(bsz, d_ft // FT_LANES),
        in_specs=[
            blk,
            pl.BlockSpec((FT_LANES, 2 * FT_LANES), lambda b, j: (0, 0)),
            _resident((n1, n2, 2 * n2), lambda b, j: (0, 0, 0)),
            pl.BlockSpec((n1, 2 * n1), lambda b, j: (0, 0)),
        ],
        out_specs=pl.BlockSpec((1, n1 * (n2 + FT_OUT_PAD), FT_LANES), lambda b, j: (b, 0, j)),
        out_shape=jax.ShapeDtypeStruct((bsz, n1 * (n2 + FT_OUT_PAD), d_ft), F32),
        scratch_shapes=[pltpu.VMEM((2, n2 * FT_PAD, FT_LANES), F32)],
        compiler_params=_params("arbitrary", "arbitrary"),
        name="fourier",
    )(ftx, w1, ta, tb)


class _ScanScratch:
    def __init__(self, ext, a, b, h, e, p, c, carry):
        self.ext, self.a, self.b, self.h = ext, a, b, h
        self.e, self.p, self.c, self.carry = e, p, c, carry


def _scan_scratch_shapes(d_rg):
    nblk = d_rg // V7X_LANES
    pad_rows = SCAN_SEGS * SCAN_SEG_STRIDE
    return [
        pltpu.VMEM((ROW_TILE + 2 * V7X_SUBLANES, d_rg), F32),
        pltpu.VMEM((nblk, ROW_TILE, V7X_LANES), F32),
        pltpu.VMEM((nblk, ROW_TILE, V7X_LANES), F32),
        pltpu.VMEM((nblk, pad_rows, V7X_LANES), F32),
        pltpu.VMEM((SCAN_SEGS, d_rg), F32),
        pltpu.VMEM((SCAN_SEGS, d_rg), F32),
        pltpu.VMEM((SCAN_SEGS, d_rg), F32),
        pltpu.VMEM((1, d_rg), F32),
    ]


def _scan_in_specs(l, dirn, bt, seq, d_rg):
    halo = V7X_SUBLANES
    per = ROW_TILE // halo
    nh = seq // halo
    lay = lambda *g: (l, 0, 0)
    ldir = lambda *g: (l, dirn, 0, 0)

    def prev_halo(*g):
        b, t = bt(*g)
        return (b, jnp.maximum(t * per - 1, 0), 0)

    def next_halo(*g):
        b, t = bt(*g)
        return (b, jnp.minimum((t + 1) * per, nh - 1), 0)

    return [
        pl.BlockSpec((1, halo, d_rg), prev_halo),
        pl.BlockSpec((1, ROW_TILE, d_rg), lambda *g: (*bt(*g), 0)),
        pl.BlockSpec((1, halo, d_rg), next_halo),
        pl.BlockSpec((1, 1, RG_CONV, d_rg), lambda *g: (l, 0, 0, 0)),
        pl.BlockSpec((1, 1, d_rg), lay),
        _resident((1, 1, 2, d_rg // 2, d_rg), lambda *g: (l, dirn, 0, 0, 0)),
        pl.BlockSpec((1, 1, 1, d_rg), ldir),
        pl.BlockSpec((1, 1, 1, d_rg), ldir),
        pl.BlockSpec((1, 1, 1, d_rg), ldir),
    ]


def _rg_scan_tile(fill_ext, first_step, cw_ref, cb_ref, wg_ref, ba_ref, bx_ref, lam_ref, scr, *, reverse):
    tm = ROW_TILE
    halo = V7X_SUBLANES
    d_rg = scr.ext.shape[-1]
    half = d_rg // 2
    nblk = d_rg // V7X_LANES

    fill_ext()
    nl = -lam_ref[0, 0]
    c8h = (-0.5 * RG_C) * (jnp.maximum(nl, 0.0) + jnp.log1p(jnp.exp(-jnp.abs(nl))))

    rh = tm // SCAN_ROW_PARTS
    segs_per_part = SCAN_SEGS // SCAN_ROW_PARTS
    for part in range(SCAN_ROW_PARTS):
        xr = cb_ref[0]
        for k in range(RG_CONV):
            lo = halo - RG_CONV_LEFT + k + part * rh
            xr = xr + scr.ext[lo:lo + rh, :] * cw_ref[0, 0, k:k + 1, :]
        xrb = xr.astype(BF16)
        yield
        for hh in range(2):
            cols = slice(hh * half, (hh + 1) * half)
            pre = _dot(xrb[:, cols], wg_ref[0, 0, hh])
            t_r = jnp.tanh(pre[:, 0:half] + ba_ref[0, 0, :, cols])
            t_i = jnp.tanh(pre[:, half:2 * half] + bx_ref[0, 0, :, cols])
            log_a = c8h[:, cols] * t_r + c8h[:, cols]
            a = jnp.exp(log_a)
            th = jnp.tanh(log_a)
            nth = -0.5 * th
            sq = jnp.where(nth > 0.0, nth * lax.rsqrt(nth), 0.0) * lax.rsqrt(1.0 - th)
            bq = sq * ((t_i + 1.0) * xr[:, cols])
            for kk in range(half // V7X_LANES):
                k = hh * (half // V7X_LANES) + kk
                lanes = slice(kk * V7X_LANES, (kk + 1) * V7X_LANES)
                for jj in range(segs_per_part):
                    src = slice(jj * SCAN_SEG_LEN, (jj + 1) * SCAN_SEG_LEN)
                    dst = pl.ds(part * segs_per_part + jj, SCAN_SEG_LEN, stride=SCAN_SEGS)
                    scr.a[k, dst, :] = a[src, lanes]
                    scr.b[k, dst, :] = bq[src, lanes]
            yield

    steps = range(SCAN_SEG_LEN - 1, -1, -1) if reverse else range(SCAN_SEG_LEN)

    def at_step(t):
        return slice(t * SCAN_SEGS, (t + 1) * SCAN_SEGS)

    for k in range(nblk):
        lanes = slice(k * V7X_LANES, (k + 1) * V7X_LANES)
        e = jnp.zeros((SCAN_SEGS, V7X_LANES), F32)
        p = e + 1.0
        for t in steps:
            at = scr.a[k, at_step(t), :]
            e = at * e + scr.b[k, at_step(t), :]
            p = at * p
        scr.e[:, lanes] = e
        scr.p[:, lanes] = p
    yield

    c = jnp.where(first_step, 0.0, scr.carry[...])
    order = range(SCAN_SEGS - 1, -1, -1) if reverse else range(SCAN_SEGS)
    for j in order:
        scr.c[j:j + 1, :] = c
        c = scr.p[j:j + 1, :] * c + scr.e[j:j + 1, :]
    scr.carry[...] = c

    for k in range(nblk):
        h = scr.c[:, k * V7X_LANES:(k + 1) * V7X_LANES]
        for t in steps:
            h = scr.a[k, at_step(t), :] * h + scr.b[k, at_step(t), :]
            scr.h[k, pl.ds(t, SCAN_SEGS, stride=SCAN_SEG_STRIDE), :] = h


def _advance(stages, n):
    for _ in range(n):
        next(stages, None)


def _scan_out_block(scr, k, j):
    return scr.h[k, j * SCAN_SEG_STRIDE:j * SCAN_SEG_STRIDE + SCAN_SEG_LEN, :]


def _mix_in_scan_kernel(x_ref, m_ref, gpre_ref, win_ref, lng_ref, lnb_ref, ws_ref, bs_ref, gmix_ref,
                        cw_ref, cb_ref, wg_ref, ba_ref, bx_ref, lam_ref,
                        ygm_ref, gact_ref, rgx_ref, ftx_ref, hf_ref, *scratch, nt, ntot, d_gm, d_rg):
    scr = _ScanScratch(*scratch)
    g = pl.program_id(0)
    tm = ROW_TILE
    halo = V7X_SUBLANES
    o_v, o_g, o_x, o_f = d_gm, 2 * d_gm, 2 * d_gm + d_rg, 2 * d_gm + 2 * d_rg

    def step(project, scan_on):
        gs = (g - 1) % nt
        if project:
            x = x_ref[0]
            sh1 = m_ref[0, 0, 0:1, :]
            sc1 = m_ref[0, 0, 1:2, :]
            h = (_rms(x, gpre_ref[0] * (1.0 + sc1)) + sh1).astype(BF16)
            rgx = _dot(h, win_ref[0, :, o_x:o_f])
            rgx_ref[0] = rgx

        def fill_ext():
            nxt = jnp.where(gs == nt - 1, 0.0, rgx[0:halo]) if project else jnp.zeros((halo, d_rg), F32)
            scr.ext[halo + tm:halo + tm + halo, :] = nxt

        if scan_on:
            scan = _rg_scan_tile(fill_ext, gs == 0, cw_ref, cb_ref, wg_ref, ba_ref, bx_ref, lam_ref, scr,
                                 reverse=False)
        else:
            scr.carry[...] = jnp.zeros_like(scr.carry)
            scan = iter(())
        if not project:
            for _ in scan:
                pass
        else:
            _advance(scan, 1)
            v = _gelu(_dot(h, win_ref[0, :, o_v:o_g]))
            mu = jnp.mean(v, axis=-1, keepdims=True)
            vc = v - mu
            var = jnp.mean(vc * vc, axis=-1, keepdims=True)
            vn = (vc * lax.rsqrt(var + EPS) * lng_ref[0] + lnb_ref[0]).astype(BF16)
            _advance(scan, 1)
            gact_ref[0] = _gelu(_dot(h, win_ref[0, :, o_g:o_x])).astype(BF16)
            _advance(scan, 3 * (SCAN_ROW_PARTS - 1) + 1 - 2)
            if scan_on:
                scr.ext[0:halo, :] = jnp.where(g % nt == 0, 0.0, scr.ext[tm:tm + halo, :])
            else:
                scr.ext[0:halo, :] = jnp.zeros((halo, d_rg), F32)
            scr.ext[halo:halo + tm, :] = rgx
            ftx_ref[0] = _dot(h, win_ref[0, :, o_f:o_f + d_gm]).astype(BF16)
            _advance(scan, 1)
            u = _gelu(_dot(h, win_ref[0, :, 0:o_v]))
            _advance(scan, 1)
            head_dim = d_gm // GM_HEADS
            head = lax.broadcasted_iota(jnp.int32, (GM_CHUNK, d_gm), 1) // head_dim
            for c in range(tm // GM_CHUNK):
                rows = slice(c * GM_CHUNK, (c + 1) * GM_CHUNK)
                r = _dot(ws_ref[0], vn[rows])
                s = r[0:GM_CHUNK]
                for hh in range(1, GM_HEADS):
                    s = jnp.where(head == hh, r[hh * GM_CHUNK:(hh + 1) * GM_CHUNK], s)
                y = u[rows] * (s + bs_ref[0])
                ygm_ref[0, rows, :] = _rms(y, gmix_ref[0, :, 0:d_gm]).astype(BF16)
                if c == 1:
                    _advance(scan, 1)
            for _ in scan:
                pass
        if scan_on:
            for k in range(d_rg // V7X_LANES):
                for j in range(SCAN_SEGS):
                    hf_ref[0, j * SCAN_SEG_LEN:(j + 1) * SCAN_SEG_LEN,
                           k * V7X_LANES:(k + 1) * V7X_LANES] = _scan_out_block(scr, k, j)

    pl.when(g == 0)(functools.partial(step, True, False))
    pl.when((g > 0) & (g < ntot))(functools.partial(step, True, True))
    pl.when(g == ntot)(functools.partial(step, False, True))


def _mix_in_scan(l, x, mod4, g_pre, w_in, ln_g, ln_b, ws, bs, g_mix, conv_w, conv_b, wg, ba, bx, lam,
                 *, d_gm, d_rg):
    bsz, seq, d = x.shape
    d_in = w_in.shape[-1]
    nt = seq // ROW_TILE
    ntot = bsz * nt

    def proj_bt(g):
        gp = jnp.minimum(g, ntot - 1)
        return gp // nt, gp % nt

    def scan_bt(g):
        gs = jnp.maximum(g - 1, 0)
        return gs // nt, gs % nt

    lay = lambda g: (l, 0, 0)
    ldir = lambda g: (l, 0, 0, 0)
    row = lambda g: (*proj_bt(g), 0)
    return pl.pallas_call(
        functools.partial(_mix_in_scan_kernel, nt=nt, ntot=ntot, d_gm=d_gm, d_rg=d_rg),
        grid=(ntot + 1,),
        in_specs=[
            pl.BlockSpec((1, ROW_TILE, d), row),
            pl.BlockSpec((1, 1, 6, d), lambda g: (l, proj_bt(g)[0], 0, 0)),
            pl.BlockSpec((1, 1, d), lay),
            _resident((1, d, d_in), lay),
            pl.BlockSpec((1, 1, d_gm), lay),
            pl.BlockSpec((1, 1, d_gm), lay),
            pl.BlockSpec((1, GM_HEADS * GM_CHUNK, GM_CHUNK), lay),
            pl.BlockSpec((1, GM_CHUNK, d_gm), lay),
            pl.BlockSpec((1, 1, d), lay),
            pl.BlockSpec((1, 1, RG_CONV, d_rg), ldir),
            pl.BlockSpec((1, 1, d_rg), lay),
            _resident((1, 1, 2, d_rg // 2, d_rg), lambda g: (l, 0, 0, 0, 0)),
            pl.BlockSpec((1, 1, 1, d_rg), ldir),
            pl.BlockSpec((1, 1, 1, d_rg), ldir),
            pl.BlockSpec((1, 1, 1, d_rg), ldir),
        ],
        out_specs=[
            pl.BlockSpec((1, ROW_TILE, d_gm), row),
            pl.BlockSpec((1, ROW_TILE, d_rg), row),
            pl.BlockSpec((1, ROW_TILE, d_rg), row),
            pl.BlockSpec((1, ROW_TILE, d_gm), row),
            pl.BlockSpec((1, ROW_TILE, d_rg), lambda g: (*scan_bt(g), 0)),
        ],
        out_shape=[
            jax.ShapeDtypeStruct((bsz, seq, d_gm), BF16),
            jax.ShapeDtypeStruct((bsz, seq, d_rg), BF16),
            jax.ShapeDtypeStruct((bsz, seq, d_rg), F32),
            jax.ShapeDtypeStruct((bsz, seq, d_gm), BF16),
            jax.ShapeDtypeStruct((bsz, seq, d_rg), F32),
        ],
        scratch_shapes=_scan_scratch_shapes(d_rg),
        compiler_params=_params("arbitrary"),
        name="mix_in_scan",
    )(x, mod4, g_pre, w_in, ln_g, ln_b, ws, bs, g_mix, conv_w, conv_b, wg, ba, bx, lam)


def _mix_out_ffn_kernel(prev_ref, cur_ref, next_ref, cw_ref, cb_ref, wg_ref, ba_ref, bx_ref, lam_ref,
                        hf_ref, x_ref, ygm_ref, gact_ref, yft_ref, m_ref, gmix_ref, ftw_ref, ftb_ref,
                        wout_ref, gpost_ref, gpre2_ref, w1_ref, w2_ref, gpost2_ref,
                        o_ref, *scratch, nt, ntot, ff_chunk, ft_groups):
    scr = _ScanScratch(*scratch)
    g = pl.program_id(0)
    d_rg = cur_ref.shape[-1]
    d_gm = ygm_ref.shape[-1]
    o_ft = d_gm + d_rg
    d_mix = wout_ref.shape[1]
    gs = jnp.minimum(g, ntot - 1) % nt
    ti = nt - 1 - gs

    def fill_ext():
        halo = V7X_SUBLANES
        scr.ext[0:halo, :] = jnp.where(ti == 0, 0.0, prev_ref[0])
        scr.ext[halo:halo + ROW_TILE, :] = cur_ref[0]
        scr.ext[halo + ROW_TILE:halo + ROW_TILE + halo, :] = jnp.where(ti == nt - 1, 0.0, next_ref[0])

    def scan_stages():
        return _rg_scan_tile(fill_ext, gs == 0, cw_ref, cb_ref, wg_ref, ba_ref, bx_ref, lam_ref, scr,
                             reverse=True)

    @pl.when(g == 0)
    def _():
        scr.carry[...] = jnp.zeros_like(scr.carry)
        for _ in scan_stages():
            pass

    @pl.when(g > 0)
    def _():
        h_bwd = jnp.concatenate(
            [jnp.concatenate([_scan_out_block(scr, k, j) for j in range(SCAN_SEGS)], axis=0)
             for k in range(d_rg // V7X_LANES)], axis=1)

        gt1 = m_ref[0, 0, 2:3, :]
        yrg = _rms((hf_ref[0] + h_bwd) * gact_ref[0].astype(F32), gmix_ref[0, :, d_gm:o_ft])
        ft_pitch = yft_ref.shape[1] // ft_groups
        ft_rows = ROW_TILE // ft_groups
        yft_raw = jnp.concatenate(
            [yft_ref[0, a * ft_pitch:a * ft_pitch + ft_rows, :] for a in range(ft_groups)], axis=0)
        yft = _dot(yft_raw.astype(BF16), ftw_ref[0]) + ftb_ref[0]
        yft = _rms(yft, gmix_ref[0, :, o_ft:d_mix])
        o = _dot(ygm_ref[0], wout_ref[0, 0:d_gm, :])
        o = o + _dot(yrg.astype(BF16), wout_ref[0, d_gm:o_ft, :])
        o = o + _dot(yft.astype(BF16), wout_ref[0, o_ft:d_mix, :])
        x = x_ref[0] + _rms(o, gt1 * gpost_ref[0])

        sh2 = m_ref[0, 0, 3:4, :]
        sc2 = m_ref[0, 0, 4:5, :]
        gt2 = m_ref[0, 0, 5:6, :]
        h = (_rms(x, gpre2_ref[0] * (1.0 + sc2)) + sh2).astype(BF16)

        scan = scan_stages()
        next(scan)
        d_ff = w1_ref.shape[-1]
        acc = None
        for c in range(d_ff // ff_chunk):
            cols = slice(c * ff_chunk, (c + 1) * ff_chunk)
            a = jnp.maximum(_dot(h, w1_ref[0, :, cols]), 0.0)
            part = _dot((a * a).astype(BF16), w2_ref[0, cols, :])
            acc = part if acc is None else acc + part
            _advance(scan, pl.cdiv(SCAN_STAGES - 1, d_ff // ff_chunk))
        for _ in scan:
            pass
        o_ref[0] = x + _rms(acc, gt2 * gpost2_ref[0])


def _mix_out_ffn(l, rgx, conv_w, conv_b, wg, ba, bx, lam, h_fwd, x, ygm, gact, yft, mod4, g_mix, ftw, ftb,
                 w_out, g_post, g_pre2, w1, w2, g_post2):
    bsz, seq, d = x.shape
    d_rg = rgx.shape[-1]
    d_gm = ygm.shape[-1]
    d_ft = yft.shape[-1]
    d_mix = w_out.shape[1]
    d_ff = w1.shape[-1]
    nt = seq // ROW_TILE
    ntot = bsz * nt

    def scan_bt(g):
        gs = jnp.minimum(g, ntot - 1)
        return gs // nt, nt - 1 - gs % nt

    def tail_bt(g):
        gf = jnp.maximum(g - 1, 0)
        return gf // nt, nt - 1 - gf % nt

    lay = lambda g: (l, 0, 0)
    row = lambda g: (*tail_bt(g), 0)
    in_specs = _scan_in_specs(l, 1, scan_bt, seq, d_rg) + [
        pl.BlockSpec((1, ROW_TILE, d_rg), row),
        pl.BlockSpec((1, ROW_TILE, d), row),
        pl.BlockSpec((1, ROW_TILE, d_gm), row),
        pl.BlockSpec((1, ROW_TILE, d_rg), row),
        pl.BlockSpec((1, yft.shape[1] // (seq // ROW_TILE), d_ft), row),
        pl.BlockSpec((1, 1, 6, d), lambda g: (l, tail_bt(g)[0], 0, 0)),
        pl.BlockSpec((1, 1, d_mix), lay),
        _resident((1, d_ft, d_ft), lay),
        pl.BlockSpec((1, 1, d_ft), lay),
        _resident((1, d_mix, d), lay),
        pl.BlockSpec((1, 1, d), lay),
        pl.BlockSpec((1, 1, d), lay),
        _resident((1, d, d_ff), lay),
        _resident((1, d_ff, d), lay),
        pl.BlockSpec((1, 1, d), lay),
    ]
    return pl.pallas_call(
        functools.partial(_mix_out_ffn_kernel, nt=nt, ntot=ntot, ff_chunk=d_ff // FF_CHUNKS,
                          ft_groups=ROW_TILE // (seq // FT_N1)),
        grid=(ntot + 1,),
        in_specs=in_specs,
        out_specs=pl.BlockSpec((1, ROW_TILE, d), row),
        out_shape=jax.ShapeDtypeStruct((bsz, seq, d), F32),
        scratch_shapes=_scan_scratch_shapes(d_rg),
        compiler_params=_params("arbitrary"),
        name="mix_out_ffn",
    )(rgx, rgx, rgx, conv_w, conv_b, wg, ba, bx, lam, h_fwd, x, ygm, gact, yft, mod4, g_mix, ftw, ftb,
      w_out, g_post, g_pre2, w1, w2, g_post2)


def _block_diag(w, per):
    *lead, n, hd, _ = w.shape
    w = w.reshape(*lead, n // per, per, hd, hd)
    keep = [(0, 0)] * (len(lead) + 2)
    rows = [jnp.pad(w[..., a, :, :], keep + [(a * hd, (per - 1 - a) * hd)]) for a in range(per)]
    return jnp.concatenate(rows, axis=-2)


def kernel(x, c, w_ada, b_ada, g_pre_mix, g_post_mix, w_in, gm_ln_g, gm_ln_b, gm_w_s, gm_b_s,
           rg_conv_w, rg_conv_b, rg_w_a, rg_b_a, rg_w_x, rg_b_x, rg_lam, ft_w, ft_b,
           g_mix_out, w_out, g_pre_ff, g_post_ff, w_ff1, w_ff2):
    bsz, seq, d = x.shape
    depth = w_in.shape[0]
    d_gm = gm_ln_g.shape[-1]
    d_rg = rg_conv_b.shape[-1]
    d_ft = ft_w.shape[1] * ft_w.shape[2]
    assert seq % ROW_TILE == 0 and ROW_TILE % GM_CHUNK == 0 and seq % FT_N1 == 0
    assert d_ft % FT_LANES == 0 and ft_w.shape[2] == FT_GROUP_DIM
    assert rg_conv_w.shape[1] == RG_CONV and rg_w_a.shape[2] == RG_HEADS

    vec = lambda a: a.reshape(depth, 1, a.shape[-1])
    w_in_b = w_in.astype(BF16)
    w_out_b = w_out.astype(BF16)
    w_ff1_b = w_ff1.astype(BF16)
    w_ff2_b = w_ff2.astype(BF16)
    ws = gm_w_s.reshape(depth, GM_HEADS * GM_CHUNK, GM_CHUNK).astype(BF16)
    bs = jnp.repeat(jnp.swapaxes(gm_b_s, 1, 2), d_gm // GM_HEADS, axis=2)
    per = RG_HEADS // 2
    wg = (0.5 * jnp.concatenate([_block_diag(rg_w_a, per), _block_diag(rg_w_x, per)], axis=-1)).astype(BF16)
    ba = 0.5 * rg_b_a.reshape(depth, 2, 1, d_rg)
    bx = 0.5 * rg_b_x.reshape(depth, 2, 1, d_rg)
    lam = rg_lam.reshape(depth, 2, 1, d_rg)
    conv_w = rg_conv_w.reshape(depth, 1, RG_CONV, d_rg)
    ftw = _block_diag(ft_w, FT_GROUPS).reshape(depth, d_ft, d_ft).astype(BF16)
    ftb = ft_b.reshape(depth, 1, d_ft)
    tables = _fourier_tables(seq)

    mod4 = _modulation(c, w_ada, b_ada).reshape(depth, bsz, 6, d)

    for l in range(depth):
        rg = (conv_w, vec(rg_conv_b), wg, ba, bx, lam)
        ygm, gact, rgx, ftx, h_fwd = _mix_in_scan(l, x, mod4, vec(g_pre_mix), w_in_b, vec(gm_ln_g),
                                                  vec(gm_ln_b), ws, bs, vec(g_mix_out), *rg,
                                                  d_gm=d_gm, d_rg=d_rg)
        yft = _fourier(ftx, tables)
        x = _mix_out_ffn(l, rgx, *rg, h_fwd, x, ygm, gact, yft, mod4, vec(g_mix_out), ftw, ftb, w_out_b,
                         vec(g_post_mix), vec(g_pre_ff), w_ff1_b, w_ff2_b, vec(g_post_ff))
    return x
```

```python
import functools
import math

import numpy as np
import jax
import jax.numpy as jnp
from jax import lax
from jax.experimental import pallas as pl
from jax.experimental.pallas import tpu as pltpu

F32 = jnp.float32
BF16 = jnp.bfloat16

EPS = 1e-6
RG_C = 8.0
RG_CONV = 4
RG_CONV_LEFT = 2
GM_HEADS = 4
GM_CHUNK = 128
RG_HEADS = 8
FT_GROUPS = 4
FT_GROUP_DIM = 64

V7X_SUBLANES = 8
V7X_LANES = 128
V7X_VMEM_LIMIT_BYTES = 56 * 1024 * 1024

MOD_COL_BLOCKS = 2
ROW_TILE = 512
FF_CHUNKS = 8
SCAN_ROW_PARTS = 2
SCAN_STAGES = 3 * SCAN_ROW_PARTS + 2
SCAN_SEGS = V7X_SUBLANES
SCAN_SEG_LEN = ROW_TILE // SCAN_SEGS
SCAN_SEG_STRIDE = SCAN_SEG_LEN + V7X_SUBLANES

FT_N1 = 64
FT_PAD = FT_N1 + V7X_SUBLANES
FT_LANES = 128
FT_OUT_PAD = V7X_SUBLANES
FT_UNROLL_A = 8
FT_UNROLL_B = 16


def _gelu(x):
    c = math.sqrt(2.0 / math.pi)
    t = jnp.tanh(x * ((c * 0.044715) * (x * x) + c))
    return (0.5 * x) * (t + 1.0)


def _sigmoid(x):
    return 0.5 * jnp.tanh(0.5 * x) + 0.5


def _rms(x, g):
    return x * lax.rsqrt(jnp.mean(x * x, axis=-1, keepdims=True) + EPS) * g


def _dot(a, b):
    return jnp.dot(a, b, preferred_element_type=F32)


def _params(*sem):
    return pltpu.CompilerParams(dimension_semantics=sem, vmem_limit_bytes=V7X_VMEM_LIMIT_BYTES)


def _resident(shape, index_map):
    return pl.BlockSpec(shape, index_map, pipeline_mode=pl.Buffered(1))


def _mod_kernel(ct_ref, w_ref, b_ref, o_ref):
    ct = ct_ref[...]
    cond = ct * _sigmoid(ct)
    d = ct.shape[0]
    for j in range(w_ref.shape[-1] // d):
        cols = slice(j * d, (j + 1) * d)
        w = w_ref[0, :, cols]
        for b in range(ct.shape[1]):
            o_ref[0, b:b + 1, cols] = jnp.sum(w * cond[:, b:b + 1], axis=0, keepdims=True) + b_ref[0, :, cols]


def _modulation(c, w_ada, b_ada):
    depth, d, d6 = w_ada.shape
    bsz = c.shape[0]
    wcols = d6 // MOD_COL_BLOCKS
    return pl.pallas_call(
        _mod_kernel,
        grid=(depth, MOD_COL_BLOCKS),
        in_specs=[
            pl.BlockSpec((d, bsz), lambda l, j: (0, 0)),
            pl.BlockSpec((1, d, wcols), lambda l, j: (l, 0, j)),
            pl.BlockSpec((1, 1, wcols), lambda l, j: (l, 0, j)),
        ],
        out_specs=pl.BlockSpec((1, bsz, wcols), lambda l, j: (l, 0, j)),
        out_shape=jax.ShapeDtypeStruct((depth, bsz, d6), F32),
        compiler_params=_params("arbitrary", "arbitrary"),
        name="modulation",
    )(c.T, w_ada, b_ada.reshape(depth, 1, d6))


def _fourier_tables(seq):
    n1 = FT_N1
    n2 = seq // n1
    gd = FT_GROUP_DIM
    j = np.arange(gd)
    ang = 2.0 * np.pi * np.outer(j, j) / gd
    eye = np.eye(FT_LANES // gd)
    w1 = np.concatenate([np.kron(eye, np.cos(ang)), -np.kron(eye, np.sin(ang))], axis=1)
    k2 = np.arange(n2)[None, :, None]
    s2 = np.arange(n2)[None, None, :]
    s1 = np.arange(n1)[:, None, None]
    ang_a = 2.0 * np.pi * ((k2 * (n1 * s2 + s1)) % seq) / seq
    ta = np.concatenate([np.cos(ang_a), np.sin(ang_a)], axis=2)
    i1 = np.arange(n1)
    ang_b = 2.0 * np.pi * np.outer(i1, i1) / n1
    tb = np.concatenate([np.cos(ang_b), np.sin(ang_b)], axis=1)
    return (jnp.asarray(w1, F32), jnp.asarray(ta, F32), jnp.asarray(tb, F32))


def _fourier_kernel(x_ref, w1_ref, ta_ref, tb_ref, o_ref, p_ref, *, seq, scale):
    n1 = FT_N1
    n2 = seq // n1
    ln = FT_LANES

    w1 = w1_ref[...].astype(BF16)
    for c in range(seq // ROW_TILE):
        p = _dot(x_ref[0, c * ROW_TILE:(c + 1) * ROW_TILE, :].astype(BF16), w1)
        for r in range(ROW_TILE // n1):
            dst = pl.ds((c * (ROW_TILE // n1) + r) * FT_PAD, n1)
            p_ref[0, dst, :] = p[r * n1:(r + 1) * n1, 0:ln]
            p_ref[1, dst, :] = p[r * n1:(r + 1) * n1, ln:2 * ln]

    def stage_a(u, carry):
        res = []
        for d in range(FT_UNROLL_A):
            s1 = u * FT_UNROLL_A + d
            rows = pl.ds(s1, n2, stride=FT_PAD)
            g = jnp.concatenate([p_ref[0, rows, :], p_ref[1, rows, :]], axis=0).astype(BF16)
            t = ta_ref[s1]
            t_im = jnp.concatenate([-t[:, n2:2 * n2], t[:, 0:n2]], axis=1)
            res.append((rows, _dot(t.astype(BF16), g), _dot(t_im.astype(BF16), g)))
        for rows, b_re, b_im in res:
            p_ref[0, rows, :] = b_re
            p_ref[1, rows, :] = b_im
        return carry

    lax.fori_loop(0, n1 // FT_UNROLL_A, stage_a, 0)

    tb = tb_ref[...].astype(BF16)
    pitch = n2 + FT_OUT_PAD
    for k1 in range(n1):
        o_ref[0, k1 * pitch + n2:(k1 + 1) * pitch, :] = jnp.zeros((FT_OUT_PAD, ln), F32)

    def stage_b(u, carry):
        for d in range(FT_UNROLL_B):
            k2 = u * FT_UNROLL_B + d
            rows = pl.ds(pl.multiple_of(k2 * FT_PAD, V7X_SUBLANES), n1)
            blk = jnp.concatenate([p_ref[0, rows, :], p_ref[1, rows, :]], axis=0).astype(BF16)
            o_ref[0, pl.ds(k2, n1, stride=pitch), :] = _dot(tb, blk) * scale
        return carry

    lax.fori_loop(0, n2 // FT_UNROLL_B, stage_b, 0)


def _fourier(ftx, tables):
    bsz, seq, d_ft = ftx.shape
    w1, ta, tb = tables
    n1 = FT_N1
    n2 = seq // n1
    blk = pl.BlockSpec((1, seq, FT_LANES), lambda b, j: (b, 0, j))
    return pl.pallas_call(
        functools.partial(_fourier_kernel, seq=seq, scale=1.0 / math.sqrt(seq * FT_GROUP_DIM)),
        grid=(bsz, d_ft // FT_LANES),
        in_specs=[
            blk,
            pl.BlockSpec((FT_LANES, 2 * FT_LANES), lambda b, j: (0, 0)),
            _resident((n1, n2, 2 * n2), lambda b, j: (0, 0, 0)),
            pl.BlockSpec((n1, 2 * n1), lambda b, j: (0, 0)),
        ],
        out_specs=pl.BlockSpec((1, n1 * (n2 + FT_OUT_PAD), FT_LANES), lambda b, j: (b, 0, j)),
        out_shape=jax.ShapeDtypeStruct((bsz, n1 * (n2 + FT_OUT_PAD), d_ft), F32),
        scratch_shapes=[pltpu.VMEM((2, n2 * FT_PAD, FT_LANES), F32)],
        compiler_params=_params("arbitrary", "arbitrary"),
        name="fourier",
    )(ftx, w1, ta, tb)


class _ScanScratch:
    def __init__(self, a, b, h, e, p, c, carry):
        self.a, self.b, self.h = a, b, h
        self.e, self.p, self.c, self.carry = e, p, c, carry


def _scan_scratch_shapes(d_rg):
    nblk = d_rg // V7X_LANES
    pad_rows = SCAN_SEGS * SCAN_SEG_STRIDE
    return [
        pltpu.VMEM((nblk, ROW_TILE, V7X_LANES), F32),
        pltpu.VMEM((nblk, ROW_TILE, V7X_LANES), F32),
        pltpu.VMEM((nblk, pad_rows, V7X_LANES), F32),
        pltpu.VMEM((SCAN_SEGS, d_rg), F32),
        pltpu.VMEM((SCAN_SEGS, d_rg), F32),
        pltpu.VMEM((SCAN_SEGS, d_rg), F32),
        pltpu.VMEM((1, d_rg), F32),
    ]


def _rg_scan_tile(conv_part, first_step, wg_ref, ba_ref, bx_ref, lam_ref, scr, *, reverse):
    tm = ROW_TILE
    d_rg = scr.e.shape[-1]
    half = d_rg // 2
    nblk = d_rg // V7X_LANES

    nl = -lam_ref[0, 0]
    c8h = (-0.5 * RG_C) * (jnp.maximum(nl, 0.0) + jnp.log1p(jnp.exp(-jnp.abs(nl))))

    rh = tm // SCAN_ROW_PARTS
    segs_per_part = SCAN_SEGS // SCAN_ROW_PARTS
    for part in range(SCAN_ROW_PARTS):
        xr = conv_part(part)
        xrb = xr.astype(BF16)
        yield
        for hh in range(2):
            cols = slice(hh * half, (hh + 1) * half)
            pre = _dot(xrb[:, cols], wg_ref[0, 0, hh])
            t_r = jnp.tanh(pre[:, 0:half] + ba_ref[0, 0, :, cols])
            t_i = jnp.tanh(pre[:, half:2 * half] + bx_ref[0, 0, :, cols])
            log_a = c8h[:, cols] * t_r + c8h[:, cols]
            a = jnp.exp(log_a)
            th = jnp.tanh(log_a)
            nth = -0.5 * th
            sq = jnp.where(nth > 0.0, nth * lax.rsqrt(nth), 0.0) * lax.rsqrt(1.0 - th)
            bq = sq * ((t_i + 1.0) * xr[:, cols])
            for kk in range(half // V7X_LANES):
                k = hh * (half // V7X_LANES) + kk
                lanes = slice(kk * V7X_LANES, (kk + 1) * V7X_LANES)
                for jj in range(segs_per_part):
                    src = slice(jj * SCAN_SEG_LEN, (jj + 1) * SCAN_SEG_LEN)
                    dst = pl.ds(part * segs_per_part + jj, SCAN_SEG_LEN, stride=SCAN_SEGS)
                    scr.a[k, dst, :] = a[src, lanes]
                    scr.b[k, dst, :] = bq[src, lanes]
            yield

    steps = range(SCAN_SEG_LEN - 1, -1, -1) if reverse else range(SCAN_SEG_LEN)

    def at_step(t):
        return slice(t * SCAN_SEGS, (t + 1) * SCAN_SEGS)

    for k in range(nblk):
        lanes = slice(k * V7X_LANES, (k + 1) * V7X_LANES)
        e = jnp.zeros((SCAN_SEGS, V7X_LANES), F32)
        p = e + 1.0
        for t in steps:
            at = scr.a[k, at_step(t), :]
            e = at * e + scr.b[k, at_step(t), :]
            p = at * p
        scr.e[:, lanes] = e
        scr.p[:, lanes] = p
    yield

    c = jnp.where(first_step, 0.0, scr.carry[...])
    order = range(SCAN_SEGS - 1, -1, -1) if reverse else range(SCAN_SEGS)
    for j in order:
        scr.c[j:j + 1, :] = c
        c = scr.p[j:j + 1, :] * c + scr.e[j:j + 1, :]
    scr.carry[...] = c

    for k in range(nblk):
        h = scr.c[:, k * V7X_LANES:(k + 1) * V7X_LANES]
        for t in steps:
            h = scr.a[k, at_step(t), :] * h + scr.b[k, at_step(t), :]
            scr.h[k, pl.ds(t, SCAN_SEGS, stride=SCAN_SEG_STRIDE), :] = h


def _advance(stages, n):
    for _ in range(n):
        next(stages, None)


def _scan_out_block(scr, k, j):
    return scr.h[k, j * SCAN_SEG_STRIDE:j * SCAN_SEG_STRIDE + SCAN_SEG_LEN, :]


def _mix_in_scan_kernel(x_ref, m_ref, gpre_ref, win_ref, lng_ref, lnb_ref, ws_ref, bs_ref, gmix_ref,
                        cw_ref, cb_ref, wg_ref, ba_ref, bx_ref, lam_ref,
                        ygm_ref, gact_ref, xr_ref, ftx_ref, hf_ref, ext_ref, *scratch, nt, ntot, d_gm, d_rg):
    scr = _ScanScratch(*scratch)
    g = pl.program_id(0)
    tm = ROW_TILE
    halo = V7X_SUBLANES
    o_v, o_g, o_x, o_f = d_gm, 2 * d_gm, 2 * d_gm + d_rg, 2 * d_gm + 2 * d_rg

    def step(project, scan_on):
        gs = (g - 1) % nt
        if project:
            x = x_ref[0]
            sh1 = m_ref[0, 0, 0:1, :]
            sc1 = m_ref[0, 0, 1:2, :]
            h = (_rms(x, gpre_ref[0] * (1.0 + sc1)) + sh1).astype(BF16)
            rgx = _dot(h, win_ref[0, :, o_x:o_f])

        def conv_part(part):
            rh = tm // SCAN_ROW_PARTS
            if part == 0:
                nxt = (jnp.where(gs == nt - 1, 0.0, rgx[0:halo]) if project
                       else jnp.zeros((halo, d_rg), F32))
                ext_ref[halo + tm:halo + tm + halo, :] = nxt
            xr = cb_ref[0]
            for k in range(RG_CONV):
                lo = halo - RG_CONV_LEFT + k + part * rh
                xr = xr + ext_ref[lo:lo + rh, :] * cw_ref[0, 0, k:k + 1, :]
            xr_ref[0, part * rh:(part + 1) * rh, :] = xr
            return xr

        if scan_on:
            scan = _rg_scan_tile(conv_part, gs == 0, wg_ref, ba_ref, bx_ref, lam_ref, scr, reverse=False)
        else:
            scr.carry[...] = jnp.zeros_like(scr.carry)
            scan = iter(())
        if not project:
            for _ in scan:
                pass
        else:
            _advance(scan, 1)
            v = _gelu(_dot(h, win_ref[0, :, o_v:o_g]))
            mu = jnp.mean(v, axis=-1, keepdims=True)
            vc = v - mu
            var = jnp.mean(vc * vc, axis=-1, keepdims=True)
            vn = (vc * lax.rsqrt(var + EPS) * lng_ref[0] + lnb_ref[0]).astype(BF16)
            _advance(scan, 1)
            gact_ref[0] = _gelu(_dot(h, win_ref[0, :, o_g:o_x])).astype(BF16)
            _advance(scan, 3 * (SCAN_ROW_PARTS - 1) + 1 - 2)
            if scan_on:
                ext_ref[0:halo, :] = jnp.where(g % nt == 0, 0.0, ext_ref[tm:tm + halo, :])
            else:
                ext_ref[0:halo, :] = jnp.zeros((halo, d_rg), F32)
            ext_ref[halo:halo + tm, :] = rgx
            ftx_ref[0] = _dot(h, win_ref[0, :, o_f:o_f + d_gm]).astype(BF16)
            _advance(scan, 1)
            u = _gelu(_dot(h, win_ref[0, :, 0:o_v]))
            _advance(scan, 1)
            head_dim = d_gm // GM_HEADS
            head = lax.broadcasted_iota(jnp.int32, (GM_CHUNK, d_gm), 1) // head_dim
            for c in range(tm // GM_CHUNK):
                rows = slice(c * GM_CHUNK, (c + 1) * GM_CHUNK)
                r = _dot(ws_ref[0], vn[rows])
                s = r[0:GM_CHUNK]
                for hh in range(1, GM_HEADS):
                    s = jnp.where(head == hh, r[hh * GM_CHUNK:(hh + 1) * GM_CHUNK], s)
                y = u[rows] * (s + bs_ref[0])
                ygm_ref[0, rows, :] = _rms(y, gmix_ref[0, :, 0:d_gm]).astype(BF16)
                if c == 1:
                    _advance(scan, 1)
            for _ in scan:
                pass
        if scan_on:
            for k in range(d_rg // V7X_LANES):
                for j in range(SCAN_SEGS):
                    hf_ref[0, j * SCAN_SEG_LEN:(j + 1) * SCAN_SEG_LEN,
                           k * V7X_LANES:(k + 1) * V7X_LANES] = _scan_out_block(scr, k, j)

    pl.when(g == 0)(functools.partial(step, True, False))
    pl.when((g > 0) & (g < ntot))(functools.partial(step, True, True))
    pl.when(g == ntot)(functools.partial(step, False, True))


def _mix_in_scan(l, x, mod4, g_pre, w_in, ln_g, ln_b, ws, bs, g_mix, conv_w, conv_b, wg, ba, bx, lam,
                 *, d_gm, d_rg):
    bsz, seq, d = x.shape
    d_in = w_in.shape[-1]
    nt = seq // ROW_TILE
    ntot = bsz * nt

    def proj_bt(g):
        gp = jnp.minimum(g, ntot - 1)
        return gp // nt, gp % nt

    def scan_bt(g):
        gs = jnp.maximum(g - 1, 0)
        return gs // nt, gs % nt

    lay = lambda g: (l, 0, 0)
    ldir = lambda g: (l, 0, 0, 0)
    row = lambda g: (*proj_bt(g), 0)
    return pl.pallas_call(
        functools.partial(_mix_in_scan_kernel, nt=nt, ntot=ntot, d_gm=d_gm, d_rg=d_rg),
        grid=(ntot + 1,),
        in_specs=[
            pl.BlockSpec((1, ROW_TILE, d), row),
            pl.BlockSpec((1, 1, 6, d), lambda g: (l, proj_bt(g)[0], 0, 0)),
            pl.BlockSpec((1, 1, d), lay),
            _resident((1, d, d_in), lay),
            pl.BlockSpec((1, 1, d_gm), lay),
            pl.BlockSpec((1, 1, d_gm), lay),
            pl.BlockSpec((1, GM_HEADS * GM_CHUNK, GM_CHUNK), lay),
            pl.BlockSpec((1, GM_CHUNK, d_gm), lay),
            pl.BlockSpec((1, 1, d), lay),
            pl.BlockSpec((1, 1, RG_CONV, d_rg), ldir),
            pl.BlockSpec((1, 1, d_rg), lay),
            _resident((1, 1, 2, d_rg // 2, d_rg), lambda g: (l, 0, 0, 0, 0)),
            pl.BlockSpec((1, 1, 1, d_rg), ldir),
            pl.BlockSpec((1, 1, 1, d_rg), ldir),
            pl.BlockSpec((1, 1, 1, d_rg), ldir),
        ],
        out_specs=[
            pl.BlockSpec((1, ROW_TILE, d_gm), row),
            pl.BlockSpec((1, ROW_TILE, d_rg), row),
            pl.BlockSpec((1, ROW_TILE, d_rg), lambda g: (*scan_bt(g), 0)),
            pl.BlockSpec((1, ROW_TILE, d_gm), row),
            pl.BlockSpec((1, ROW_TILE, d_rg), lambda g: (*scan_bt(g), 0)),
        ],
        out_shape=[
            jax.ShapeDtypeStruct((bsz, seq, d_gm), BF16),
            jax.ShapeDtypeStruct((bsz, seq, d_rg), BF16),
            jax.ShapeDtypeStruct((bsz, seq, d_rg), F32),
            jax.ShapeDtypeStruct((bsz, seq, d_gm), BF16),
            jax.ShapeDtypeStruct((bsz, seq, d_rg), F32),
        ],
        scratch_shapes=[pltpu.VMEM((ROW_TILE + 2 * V7X_SUBLANES, d_rg), F32)] + _scan_scratch_shapes(d_rg),
        compiler_params=_params("arbitrary"),
        name="mix_in_scan",
    )(x, mod4, g_pre, w_in, ln_g, ln_b, ws, bs, g_mix, conv_w, conv_b, wg, ba, bx, lam)


def _mix_out_ffn_kernel(xr_ref, wg_ref, ba_ref, bx_ref, lam_ref,
                        hf_ref, x_ref, ygm_ref, gact_ref, yft_ref, m_ref, gmix_ref, ftw_ref, ftb_ref,
                        wout_ref, gpost_ref, gpre2_ref, w1_ref, w2_ref, gpost2_ref,
                        o_ref, *scratch, nt, ntot, ff_chunk, ft_groups):
    scr = _ScanScratch(*scratch)
    g = pl.program_id(0)
    d_rg = xr_ref.shape[-1]
    d_gm = ygm_ref.shape[-1]
    o_ft = d_gm + d_rg
    d_mix = wout_ref.shape[1]
    gs = jnp.minimum(g, ntot - 1) % nt

    def conv_part(part):
        rh = ROW_TILE // SCAN_ROW_PARTS
        return xr_ref[0, part * rh:(part + 1) * rh, :]

    def scan_stages():
        return _rg_scan_tile(conv_part, gs == 0, wg_ref, ba_ref, bx_ref, lam_ref, scr, reverse=True)

    @pl.when(g == 0)
    def _():
        scr.carry[...] = jnp.zeros_like(scr.carry)
        for _ in scan_stages():
            pass

    @pl.when(g > 0)
    def _():
        h_bwd = jnp.concatenate(
            [jnp.concatenate([_scan_out_block(scr, k, j) for j in range(SCAN_SEGS)], axis=0)
             for k in range(d_rg // V7X_LANES)], axis=1)

        gt1 = m_ref[0, 0, 2:3, :]
        yrg = _rms((hf_ref[0] + h_bwd) * gact_ref[0].astype(F32), gmix_ref[0, :, d_gm:o_ft])
        ft_pitch = yft_ref.shape[1] // ft_groups
        ft_rows = ROW_TILE // ft_groups
        yft_raw = jnp.concatenate(
            [yft_ref[0, a * ft_pitch:a * ft_pitch + ft_rows, :] for a in range(ft_groups)], axis=0)
        yft = _dot(yft_raw.astype(BF16), ftw_ref[0]) + ftb_ref[0]
        yft = _rms(yft, gmix_ref[0, :, o_ft:d_mix])
        o = _dot(ygm_ref[0], wout_ref[0, 0:d_gm, :])
        o = o + _dot(yrg.astype(BF16), wout_ref[0, d_gm:o_ft, :])
        o = o + _dot(yft.astype(BF16), wout_ref[0, o_ft:d_mix, :])
        x = x_ref[0] + _rms(o, gt1 * gpost_ref[0])

        sh2 = m_ref[0, 0, 3:4, :]
        sc2 = m_ref[0, 0, 4:5, :]
        gt2 = m_ref[0, 0, 5:6, :]
        h = (_rms(x, gpre2_ref[0] * (1.0 + sc2)) + sh2).astype(BF16)

        scan = scan_stages()
        next(scan)
        d_ff = w1_ref.shape[-1]
        acc = None
        for c in range(d_ff // ff_chunk):
            cols = slice(c * ff_chunk, (c + 1) * ff_chunk)
            a = jnp.maximum(_dot(h, w1_ref[0, :, cols]), 0.0)
            part = _dot((a * a).astype(BF16), w2_ref[0, cols, :])
            acc = part if acc is None else acc + part
            _advance(scan, pl.cdiv(SCAN_STAGES - 1, d_ff // ff_chunk))
        for _ in scan:
            pass
        o_ref[0] = x + _rms(acc, gt2 * gpost2_ref[0])


def _mix_out_ffn(l, xr, wg, ba, bx, lam, h_fwd, x, ygm, gact, yft, mod4, g_mix, ftw, ftb,
                 w_out, g_post, g_pre2, w1, w2, g_post2):
    bsz, seq, d = x.shape
    d_rg = xr.shape[-1]
    d_gm = ygm.shape[-1]
    d_ft = yft.shape[-1]
    d_mix = w_out.shape[1]
    d_ff = w1.shape[-1]
    nt = seq // ROW_TILE
    ntot = bsz * nt

    def scan_bt(g):
        gs = jnp.minimum(g, ntot - 1)
        return gs // nt, nt - 1 - gs % nt

    def tail_bt(g):
        gf = jnp.maximum(g - 1, 0)
        return gf // nt, nt - 1 - gf % nt

    lay = lambda g: (l, 0, 0)
    ldir = lambda g: (l, 1, 0, 0)
    row = lambda g: (*tail_bt(g), 0)
    in_specs = [
        pl.BlockSpec((1, ROW_TILE, d_rg), lambda g: (*scan_bt(g), 0)),
        _resident((1, 1, 2, d_rg // 2, d_rg), lambda g: (l, 1, 0, 0, 0)),
        pl.BlockSpec((1, 1, 1, d_rg), ldir),
        pl.BlockSpec((1, 1, 1, d_rg), ldir),
        pl.BlockSpec((1, 1, 1, d_rg), ldir),
        pl.BlockSpec((1, ROW_TILE, d_rg), row),
        pl.BlockSpec((1, ROW_TILE, d), row),
        pl.BlockSpec((1, ROW_TILE, d_gm), row),
        pl.BlockSpec((1, ROW_TILE, d_rg), row),
        pl.BlockSpec((1, yft.shape[1] // (seq // ROW_TILE), d_ft), row),
        pl.BlockSpec((1, 1, 6, d), lambda g: (l, tail_bt(g)[0], 0, 0)),
        pl.BlockSpec((1, 1, d_mix), lay),
        _resident((1, d_ft, d_ft), lay),
        pl.BlockSpec((1, 1, d_ft), lay),
        _resident((1, d_mix, d), lay),
        pl.BlockSpec((1, 1, d), lay),
        pl.BlockSpec((1, 1, d), lay),
        _resident((1, d, d_ff), lay),
        _resident((1, d_ff, d), lay),
        pl.BlockSpec((1, 1, d), lay),
    ]
    return pl.pallas_call(
        functools.partial(_mix_out_ffn_kernel, nt=nt, ntot=ntot, ff_chunk=d_ff // FF_CHUNKS,
                          ft_groups=ROW_TILE // (seq // FT_N1)),
        grid=(ntot + 1,),
        in_specs=in_specs,
        out_specs=pl.BlockSpec((1, ROW_TILE, d), row),
        out_shape=jax.ShapeDtypeStruct((bsz, seq, d), F32),
        scratch_shapes=_scan_scratch_shapes(d_rg),
        compiler_params=_params("arbitrary"),
        name="mix_out_ffn",
    )(xr, wg, ba, bx, lam, h_fwd, x, ygm, gact, yft, mod4, g_mix, ftw, ftb,
      w_out, g_post, g_pre2, w1, w2, g_post2)


def _block_diag(w, per):
    *lead, n, hd, _ = w.shape
    w = w.reshape(*lead, n // per, per, hd, hd)
    keep = [(0, 0)] * (len(lead) + 2)
    rows = [jnp.pad(w[..., a, :, :], keep + [(a * hd, (per - 1 - a) * hd)]) for a in range(per)]
    return jnp.concatenate(rows, axis=-2)


def kernel(x, c, w_ada, b_ada, g_pre_mix, g_post_mix, w_in, gm_ln_g, gm_ln_b, gm_w_s, gm_b_s,
           rg_conv_w, rg_conv_b, rg_w_a, rg_b_a, rg_w_x, rg_b_x, rg_lam, ft_w, ft_b,
           g_mix_out, w_out, g_pre_ff, g_post_ff, w_ff1, w_ff2):
    bsz, seq, d = x.shape
    depth = w_in.shape[0]
    d_gm = gm_ln_g.shape[-1]
    d_rg = rg_conv_b.shape[-1]
    d_ft = ft_w.shape[1] * ft_w.shape[2]
    assert seq % ROW_TILE == 0 and ROW_TILE % GM_CHUNK == 0 and seq % FT_N1 == 0
    assert d_ft % FT_LANES == 0 and ft_w.shape[2] == FT_GROUP_DIM
    assert rg_conv_w.shape[1] == RG_CONV and rg_w_a.shape[2] == RG_HEADS

    vec = lambda a: a.reshape(depth, 1, a.shape[-1])
    w_in_b = w_in.astype(BF16)
    w_out_b = w_out.astype(BF16)
    w_ff1_b = w_ff1.astype(BF16)
    w_ff2_b = w_ff2.astype(BF16)
    ws = gm_w_s.reshape(depth, GM_HEADS * GM_CHUNK, GM_CHUNK).astype(BF16)
    bs = jnp.repeat(jnp.swapaxes(gm_b_s, 1, 2), d_gm // GM_HEADS, axis=2)
    per = RG_HEADS // 2
    wg = (0.5 * jnp.concatenate([_block_diag(rg_w_a, per), _block_diag(rg_w_x, per)], axis=-1)).astype(BF16)
    ba = 0.5 * rg_b_a.reshape(depth, 2, 1, d_rg)
    bx = 0.5 * rg_b_x.reshape(depth, 2, 1, d_rg)
    lam = rg_lam.reshape(depth, 2, 1, d_rg)
    conv_w = rg_conv_w.reshape(depth, 1, RG_CONV, d_rg)
    ftw = _block_diag(ft_w, FT_GROUPS).reshape(depth, d_ft, d_ft).astype(BF16)
    ftb = ft_b.reshape(depth, 1, d_ft)
    tables = _fourier_tables(seq)

    mod4 = _modulation(c, w_ada, b_ada).reshape(depth, bsz, 6, d)

    for l in range(depth):
        gates = (wg, ba, bx, lam)
        ygm, gact, xr, ftx, h_fwd = _mix_in_scan(l, x, mod4, vec(g_pre_mix), w_in_b, vec(gm_ln_g),
                                                 vec(gm_ln_b), ws, bs, vec(g_mix_out), conv_w,
                                                 vec(rg_conv_b), *gates, d_gm=d_gm, d_rg=d_rg)
        yft = _fourier(ftx, tables)
        x = _mix_out_ffn(l, xr, *gates, h_fwd, x, ygm, gact, yft, mod4, vec(g_mix_out), ftw, ftb, w_out_b,
                         vec(g_post_mix), vec(g_pre_ff), w_ff1_b, w_ff2_b, vec(g_post_ff))
    return x
```

```python
import functools
import math

import numpy as np
import jax
import jax.numpy as jnp
from jax import lax
from jax.experimental import pallas as pl
from jax.experimental.pallas import tpu as pltpu

F32 = jnp.float32
BF16 = jnp.bfloat16

EPS = 1e-6
RG_C = 8.0
RG_CONV = 4
RG_CONV_LEFT = 2
GM_HEADS = 4
GM_CHUNK = 128
RG_HEADS = 8
FT_GROUPS = 4
FT_GROUP_DIM = 64

V7X_SUBLANES = 8
V7X_LANES = 128
V7X_VMEM_LIMIT_BYTES = 56 * 1024 * 1024

MOD_COL_BLOCKS = 2
ROW_TILE = 512
FF_CHUNKS = 8
SCAN_ROW_PARTS = 2
SCAN_STAGES = 3 * SCAN_ROW_PARTS + 2
SCAN_SEGS = V7X_SUBLANES
SCAN_SEG_LEN = ROW_TILE // SCAN_SEGS
SCAN_SEG_STRIDE = SCAN_SEG_LEN + V7X_SUBLANES
SEG_PITCH = SCAN_SEG_LEN + 3 * V7X_SUBLANES

FT_N1 = 64
FT_PAD = FT_N1 + V7X_SUBLANES
FT_LANES = 128
FT_OUT_PAD = V7X_SUBLANES
FT_UNROLL_A = 8
FT_UNROLL_B = 16


def _gelu(x):
    c = math.sqrt(2.0 / math.pi)
    t = jnp.tanh(x * ((c * 0.044715) * (x * x) + c))
    return (0.5 * x) * (t + 1.0)


def _sigmoid(x):
    return 0.5 * jnp.tanh(0.5 * x) + 0.5


def _rms(x, g):
    return x * lax.rsqrt(jnp.mean(x * x, axis=-1, keepdims=True) + EPS) * g


def _dot(a, b):
    return jnp.dot(a, b, preferred_element_type=F32)


def _params(*sem):
    return pltpu.CompilerParams(dimension_semantics=sem, vmem_limit_bytes=V7X_VMEM_LIMIT_BYTES)


def _resident(shape, index_map):
    return pl.BlockSpec(shape, index_map, pipeline_mode=pl.Buffered(1))


def _mod_kernel(ct_ref, w_ref, b_ref, o_ref):
    ct = ct_ref[...]
    cond = ct * _sigmoid(ct)
    d = ct.shape[0]
    for j in range(w_ref.shape[-1] // d):
        cols = slice(j * d, (j + 1) * d)
        w = w_ref[0, :, cols]
        for b in range(ct.shape[1]):
            o_ref[0, b:b + 1, cols] = jnp.sum(w * cond[:, b:b + 1], axis=0, keepdims=True) + b_ref[0, :, cols]


def _modulation(c, w_ada, b_ada):
    depth, d, d6 = w_ada.shape
    bsz = c.shape[0]
    wcols = d6 // MOD_COL_BLOCKS
    return pl.pallas_call(
        _mod_kernel,
        grid=(depth, MOD_COL_BLOCKS),
        in_specs=[
            pl.BlockSpec((d, bsz), lambda l, j: (0, 0)),
            pl.BlockSpec((1, d, wcols), lambda l, j: (l, 0, j)),
            pl.BlockSpec((1, 1, wcols), lambda l, j: (l, 0, j)),
        ],
        out_specs=pl.BlockSpec((1, bsz, wcols), lambda l, j: (l, 0, j)),
        out_shape=jax.ShapeDtypeStruct((depth, bsz, d6), F32),
        compiler_params=_params("arbitrary", "arbitrary"),
        name="modulation",
    )(c.T, w_ada, b_ada.reshape(depth, 1, d6))


def _fourier_tables(seq):
    n1 = FT_N1
    n2 = seq // n1
    gd = FT_GROUP_DIM
    j = np.arange(gd)
    ang = 2.0 * np.pi * np.outer(j, j) / gd
    eye = np.eye(FT_LANES // gd)
    w1 = np.concatenate([np.kron(eye, np.cos(ang)), -np.kron(eye, np.sin(ang))], axis=1)
    k2 = np.arange(n2)[None, :, None]
    s2 = np.arange(n2)[None, None, :]
    s1 = np.arange(n1)[:, None, None]
    ang_a = 2.0 * np.pi * ((k2 * (n1 * s2 + s1)) % seq) / seq
    ta = np.concatenate([np.cos(ang_a), np.sin(ang_a)], axis=2)
    i1 = np.arange(n1)
    ang_b = 2.0 * np.pi * np.outer(i1, i1) / n1
    tb = np.concatenate([np.cos(ang_b), np.sin(ang_b)], axis=1)
    return (jnp.asarray(w1, F32), jnp.asarray(ta, F32), jnp.asarray(tb, F32))


def _fourier_kernel(x_ref, w1_ref, ta_ref, tb_ref, o_ref, p_ref, *, seq, scale):
    n1 = FT_N1
    n2 = seq // n1
    ln = FT_LANES

    w1 = w1_ref[...].astype(BF16)
    for c in range(seq // ROW_TILE):
        p = _dot(x_ref[0, c * ROW_TILE:(c + 1) * ROW_TILE, :].astype(BF16), w1)
        for r in range(ROW_TILE // n1):
            dst = pl.ds((c * (ROW_TILE // n1) + r) * FT_PAD, n1)
            p_ref[0, dst, :] = p[r * n1:(r + 1) * n1, 0:ln]
            p_ref[1, dst, :] = p[r * n1:(r + 1) * n1, ln:2 * ln]

    def stage_a(u, carry):
        res = []
        for d in range(FT_UNROLL_A):
            s1 = u * FT_UNROLL_A + d
            rows = pl.ds(s1, n2, stride=FT_PAD)
            g = jnp.concatenate([p_ref[0, rows, :], p_ref[1, rows, :]], axis=0).astype(BF16)
            t = ta_ref[s1]
            t_im = jnp.concatenate([-t[:, n2:2 * n2], t[:, 0:n2]], axis=1)
            res.append((rows, _dot(t.astype(BF16), g), _dot(t_im.astype(BF16), g)))
        for rows, b_re, b_im in res:
            p_ref[0, rows, :] = b_re
            p_ref[1, rows, :] = b_im
        return carry

    lax.fori_loop(0, n1 // FT_UNROLL_A, stage_a, 0)

    tb = tb_ref[...].astype(BF16)
    pitch = n2 + FT_OUT_PAD
    for k1 in range(n1):
        o_ref[0, k1 * pitch + n2:(k1 + 1) * pitch, :] = jnp.zeros((FT_OUT_PAD, ln), F32)

    def stage_b(u, carry):
        for d in range(FT_UNROLL_B):
            k2 = u * FT_UNROLL_B + d
            rows = pl.ds(pl.multiple_of(k2 * FT_PAD, V7X_SUBLANES), n1)
            blk = jnp.concatenate([p_ref[0, rows, :], p_ref[1, rows, :]], axis=0).astype(BF16)
            o_ref[0, pl.ds(k2, n1, stride=pitch), :] = _dot(tb, blk) * scale
        return carry

    lax.fori_loop(0, n2 // FT_UNROLL_B, stage_b, 0)


def _fourier(ftx, tables):
    bsz, seq, d_ft = ftx.shape
    w1, ta, tb = tables
    n1 = FT_N1
    n2 = seq // n1
    blk = pl.BlockSpec((1, seq, FT_LANES), lambda b, j: (b, 0, j))
    return pl.pallas_call(
        functools.partial(_fourier_kernel, seq=seq, scale=1.0 / math.sqrt(seq * FT_GROUP_DIM)),
        grid=(bsz, d_ft // FT_LANES),
        in_specs=[
            blk,
            pl.BlockSpec((FT_LANES, 2 * FT_LANES), lambda b, j: (0, 0)),
            _resident((n1, n2, 2 * n2), lambda b, j: (0, 0, 0)),
            pl.BlockSpec((n1, 2 * n1), lambda b, j: (0, 0)),
        ],
        out_specs=pl.BlockSpec((1, n1 * (n2 + FT_OUT_PAD), FT_LANES), lambda b, j: (b, 0, j)),
        out_shape=jax.ShapeDtypeStruct((bsz, n1 * (n2 + FT_OUT_PAD), d_ft), F32),
        scratch_shapes=[pltpu.VMEM((2, n2 * FT_PAD, FT_LANES), F32)],
        compiler_params=_params("arbitrary", "arbitrary"),
        name="fourier",
    )(ftx, w1, ta, tb)


class _ScanScratch:
    def __init__(self, a, b, h, e, p, c, carry):
        self.a, self.b, self.h = a, b, h
        self.e, self.p, self.c, self.carry = e, p, c, carry


def _scan_scratch_shapes(d_rg):
    nblk = d_rg // V7X_LANES
    pad_rows = SCAN_SEGS * SCAN_SEG_STRIDE
    return [
        pltpu.VMEM((nblk, ROW_TILE, V7X_LANES), F32),
        pltpu.VMEM((nblk, ROW_TILE, V7X_LANES), F32),
        pltpu.VMEM((nblk, pad_rows, V7X_LANES), F32),
        pltpu.VMEM((SCAN_SEGS, d_rg), F32),
        pltpu.VMEM((SCAN_SEGS, d_rg), F32),
        pltpu.VMEM((SCAN_SEGS, d_rg), F32),
        pltpu.VMEM((1, d_rg), F32),
    ]


def _rg_scan_tile(conv_part, first_step, wg_ref, ba_ref, bx_ref, lam_ref, scr, *, reverse):
    tm = ROW_TILE
    d_rg = scr.e.shape[-1]
    half = d_rg // 2
    nblk = d_rg // V7X_LANES

    nl = -lam_ref[0, 0]
    c8h = (-0.5 * RG_C) * (jnp.maximum(nl, 0.0) + jnp.log1p(jnp.exp(-jnp.abs(nl))))

    rh = tm // SCAN_ROW_PARTS
    for part in range(SCAN_ROW_PARTS):
        xr = conv_part(part)
        xrb = xr.astype(BF16)
        yield
        for hh in range(2):
            cols = slice(hh * half, (hh + 1) * half)
            pre = _dot(xrb[:, cols], wg_ref[0, 0, hh])
            t_r = jnp.tanh(pre[:, 0:half] + ba_ref[0, 0, :, cols])
            t_i = jnp.tanh(pre[:, half:2 * half] + bx_ref[0, 0, :, cols])
            log_a = c8h[:, cols] * t_r + c8h[:, cols]
            a = jnp.exp(log_a)
            th = jnp.tanh(log_a)
            nth = -0.5 * th
            sq = jnp.where(nth > 0.0, nth * lax.rsqrt(nth), 0.0) * lax.rsqrt(1.0 - th)
            bq = sq * ((t_i + 1.0) * xr[:, cols])
            for kk in range(half // V7X_LANES):
                k = hh * (half // V7X_LANES) + kk
                lanes = slice(kk * V7X_LANES, (kk + 1) * V7X_LANES)
                scr.a[k, part * rh:(part + 1) * rh, :] = a[:, lanes]
                scr.b[k, part * rh:(part + 1) * rh, :] = bq[:, lanes]
            yield

    steps = range(SCAN_SEG_LEN - 1, -1, -1) if reverse else range(SCAN_SEG_LEN)

    def at_step(t):
        return slice(t * SCAN_SEGS, (t + 1) * SCAN_SEGS)

    for k in range(nblk):
        lanes = slice(k * V7X_LANES, (k + 1) * V7X_LANES)
        e = jnp.zeros((SCAN_SEGS, V7X_LANES), F32)
        p = e + 1.0
        for t in steps:
            at = scr.a[k, at_step(t), :]
            e = at * e + scr.b[k, at_step(t), :]
            p = at * p
        scr.e[:, lanes] = e
        scr.p[:, lanes] = p
    yield

    c = jnp.where(first_step, 0.0, scr.carry[...])
    order = range(SCAN_SEGS - 1, -1, -1) if reverse else range(SCAN_SEGS)
    for j in order:
        scr.c[j:j + 1, :] = c
        c = scr.p[j:j + 1, :] * c + scr.e[j:j + 1, :]
    scr.carry[...] = c

    for k in range(nblk):
        h = scr.c[:, k * V7X_LANES:(k + 1) * V7X_LANES]
        for t in steps:
            h = scr.a[k, at_step(t), :] * h + scr.b[k, at_step(t), :]
            scr.h[k, pl.ds(t, SCAN_SEGS, stride=SCAN_SEG_STRIDE), :] = h


def _advance(stages, n):
    for _ in range(n):
        next(stages, None)


def _scan_out_block(scr, k, j):
    return scr.h[k, j * SCAN_SEG_STRIDE:j * SCAN_SEG_STRIDE + SCAN_SEG_LEN, :]


def _mix_in_scan_kernel(x_ref, m_ref, gpre_ref, win_ref, lng_ref, lnb_ref, ws_ref, bs_ref, gmix_ref,
                        cw_ref, cb_ref, wg_ref, ba_ref, bx_ref, lam_ref,
                        ygm_ref, gact_ref, xr_ref, ftx_ref, hf_ref, ext_ref, segp_ref, *scratch,
                        nt, ntot, d_gm, d_rg):
    scr = _ScanScratch(*scratch)
    g = pl.program_id(0)
    tm = ROW_TILE
    halo = V7X_SUBLANES
    o_v, o_g, o_x, o_f = d_gm, 2 * d_gm, 2 * d_gm + d_rg, 2 * d_gm + 2 * d_rg

    def step(project, scan_on):
        gs = (g - 1) % nt
        if project:
            x = x_ref[0]
            sh1 = m_ref[0, 0, 0:1, :]
            sc1 = m_ref[0, 0, 1:2, :]
            h = (_rms(x, gpre_ref[0] * (1.0 + sc1)) + sh1).astype(BF16)
            rgx = _dot(h, win_ref[0, :, o_x:o_f])

        def conv_part(part):
            nblk = d_rg // V7X_LANES
            seg_rows = SCAN_SEG_LEN + 2 * halo
            if part == 0:
                nxt = (jnp.where(gs == nt - 1, 0.0, rgx[0:halo]) if project
                       else jnp.zeros((halo, d_rg), F32))
                ext_ref[halo + tm:halo + tm + halo, :] = nxt
                for k in range(nblk):
                    for j in range(SCAN_SEGS):
                        segp_ref[k, j * SEG_PITCH:j * SEG_PITCH + seg_rows, :] = (
                            ext_ref[j * SCAN_SEG_LEN:j * SCAN_SEG_LEN + seg_rows,
                                    k * V7X_LANES:(k + 1) * V7X_LANES])
            steps = SCAN_SEG_LEN // SCAN_ROW_PARTS
            cols = []
            for k in range(nblk):
                lanes = slice(k * V7X_LANES, (k + 1) * V7X_LANES)
                taps = [jnp.broadcast_to(cw_ref[0, 0, i:i + 1, lanes], (SCAN_SEGS, V7X_LANES))
                        for i in range(RG_CONV)]
                bias = jnp.broadcast_to(cb_ref[0, :, lanes], (SCAN_SEGS, V7X_LANES))
                slabs = [segp_ref[k, pl.ds(halo - RG_CONV_LEFT + part * steps + i, SCAN_SEGS, stride=SEG_PITCH), :]
                         for i in range(steps + RG_CONV - 1)]
                out = []
                for t in range(steps):
                    acc = bias
                    for i in range(RG_CONV):
                        acc = acc + slabs[t + i] * taps[i]
                    out.append(acc)
                cols.append(jnp.concatenate(out, axis=0))
            xr = jnp.concatenate(cols, axis=1)
            rh = tm // SCAN_ROW_PARTS
            xr_ref[0, part * rh:(part + 1) * rh, :] = xr
            return xr

        if scan_on:
            scan = _rg_scan_tile(conv_part, gs == 0, wg_ref, ba_ref, bx_ref, lam_ref, scr, reverse=False)
        else:
            scr.carry[...] = jnp.zeros_like(scr.carry)
            scan = iter(())
        if not project:
            for _ in scan:
                pass
        else:
            _advance(scan, 1)
            v = _gelu(_dot(h, win_ref[0, :, o_v:o_g]))
            mu = jnp.mean(v, axis=-1, keepdims=True)
            vc = v - mu
            var = jnp.mean(vc * vc, axis=-1, keepdims=True)
            vn = (vc * lax.rsqrt(var + EPS) * lng_ref[0] + lnb_ref[0]).astype(BF16)
            _advance(scan, 1)
            gact_ref[0] = _gelu(_dot(h, win_ref[0, :, o_g:o_x])).astype(BF16)
            _advance(scan, 3 * (SCAN_ROW_PARTS - 1) + 1 - 2)
            if scan_on:
                ext_ref[0:halo, :] = jnp.where(g % nt == 0, 0.0, ext_ref[tm:tm + halo, :])
            else:
                ext_ref[0:halo, :] = jnp.zeros((halo, d_rg), F32)
            ext_ref[halo:halo + tm, :] = rgx
            ftx_ref[0] = _dot(h, win_ref[0, :, o_f:o_f + d_gm]).astype(BF16)
            _advance(scan, 1)
            u = _gelu(_dot(h, win_ref[0, :, 0:o_v]))
            _advance(scan, 1)
            head_dim = d_gm // GM_HEADS
            head = lax.broadcasted_iota(jnp.int32, (GM_CHUNK, d_gm), 1) // head_dim
            for c in range(tm // GM_CHUNK):
                rows = slice(c * GM_CHUNK, (c + 1) * GM_CHUNK)
                r = _dot(ws_ref[0], vn[rows])
                s = r[0:GM_CHUNK]
                for hh in range(1, GM_HEADS):
                    s = jnp.where(head == hh, r[hh * GM_CHUNK:(hh + 1) * GM_CHUNK], s)
                y = u[rows] * (s + bs_ref[0])
                ygm_ref[0, rows, :] = _rms(y, gmix_ref[0, :, 0:d_gm]).astype(BF16)
                if c == 1:
                    _advance(scan, 1)
            for _ in scan:
                pass
        if scan_on:
            for k in range(d_rg // V7X_LANES):
                for j in range(SCAN_SEGS):
                    hf_ref[0, j * SCAN_SEG_LEN:(j + 1) * SCAN_SEG_LEN,
                           k * V7X_LANES:(k + 1) * V7X_LANES] = _scan_out_block(scr, k, j)

    pl.when(g == 0)(functools.partial(step, True, False))
    pl.when((g > 0) & (g < ntot))(functools.partial(step, True, True))
    pl.when(g == ntot)(functools.partial(step, False, True))


def _mix_in_scan(l, x, mod4, g_pre, w_in, ln_g, ln_b, ws, bs, g_mix, conv_w, conv_b, wg, ba, bx, lam,
                 *, d_gm, d_rg):
    bsz, seq, d = x.shape
    d_in = w_in.shape[-1]
    nt = seq // ROW_TILE
    ntot = bsz * nt

    def proj_bt(g):
        gp = jnp.minimum(g, ntot - 1)
        return gp // nt, gp % nt

    def scan_bt(g):
        gs = jnp.maximum(g - 1, 0)
        return gs // nt, gs % nt

    lay = lambda g: (l, 0, 0)
    ldir = lambda g: (l, 0, 0, 0)
    row = lambda g: (*proj_bt(g), 0)
    return pl.pallas_call(
        functools.partial(_mix_in_scan_kernel, nt=nt, ntot=ntot, d_gm=d_gm, d_rg=d_rg),
        grid=(ntot + 1,),
        in_specs=[
            pl.BlockSpec((1, ROW_TILE, d), row),
            pl.BlockSpec((1, 1, 6, d), lambda g: (l, proj_bt(g)[0], 0, 0)),
            pl.BlockSpec((1, 1, d), lay),
            _resident((1, d, d_in), lay),
            pl.BlockSpec((1, 1, d_gm), lay),
            pl.BlockSpec((1, 1, d_gm), lay),
            pl.BlockSpec((1, GM_HEADS * GM_CHUNK, GM_CHUNK), lay),
            pl.BlockSpec((1, GM_CHUNK, d_gm), lay),
            pl.BlockSpec((1, 1, d), lay),
            pl.BlockSpec((1, 1, RG_CONV, d_rg), ldir),
            pl.BlockSpec((1, 1, d_rg), lay),
            _resident((1, 1, 2, d_rg // 2, d_rg), lambda g: (l, 0, 0, 0, 0)),
            pl.BlockSpec((1, 1, 1, d_rg), ldir),
            pl.BlockSpec((1, 1, 1, d_rg), ldir),
            pl.BlockSpec((1, 1, 1, d_rg), ldir),
        ],
        out_specs=[
            pl.BlockSpec((1, ROW_TILE, d_gm), row),
            pl.BlockSpec((1, ROW_TILE, d_rg), row),
            pl.BlockSpec((1, ROW_TILE, d_rg), lambda g: (*scan_bt(g), 0)),
            pl.BlockSpec((1, ROW_TILE, d_gm), row),
            pl.BlockSpec((1, ROW_TILE, d_rg), lambda g: (*scan_bt(g), 0)),
        ],
        out_shape=[
            jax.ShapeDtypeStruct((bsz, seq, d_gm), BF16),
            jax.ShapeDtypeStruct((bsz, seq, d_rg), BF16),
            jax.ShapeDtypeStruct((bsz, seq, d_rg), F32),
            jax.ShapeDtypeStruct((bsz, seq, d_gm), BF16),
            jax.ShapeDtypeStruct((bsz, seq, d_rg), F32),
        ],
        scratch_shapes=[
            pltpu.VMEM((ROW_TILE + 2 * V7X_SUBLANES, d_rg), F32),
            pltpu.VMEM((d_rg // V7X_LANES, SCAN_SEGS * SEG_PITCH, V7X_LANES), F32),
        ] + _scan_scratch_shapes(d_rg),
        compiler_params=_params("arbitrary"),
        name="mix_in_scan",
    )(x, mod4, g_pre, w_in, ln_g, ln_b, ws, bs, g_mix, conv_w, conv_b, wg, ba, bx, lam)


def _mix_out_ffn_kernel(xr_ref, wg_ref, ba_ref, bx_ref, lam_ref,
                        hf_ref, x_ref, ygm_ref, gact_ref, yft_ref, m_ref, gmix_ref, ftw_ref, ftb_ref,
                        wout_ref, gpost_ref, gpre2_ref, w1_ref, w2_ref, gpost2_ref,
                        o_ref, *scratch, nt, ntot, ff_chunk, ft_groups):
    scr = _ScanScratch(*scratch)
    g = pl.program_id(0)
    d_rg = xr_ref.shape[-1]
    d_gm = ygm_ref.shape[-1]
    o_ft = d_gm + d_rg
    d_mix = wout_ref.shape[1]
    gs = jnp.minimum(g, ntot - 1) % nt

    def conv_part(part):
        rh = ROW_TILE // SCAN_ROW_PARTS
        return xr_ref[0, part * rh:(part + 1) * rh, :]

    def scan_stages():
        return _rg_scan_tile(conv_part, gs == 0, wg_ref, ba_ref, bx_ref, lam_ref, scr, reverse=True)

    @pl.when(g == 0)
    def _():
        scr.carry[...] = jnp.zeros_like(scr.carry)
        for _ in scan_stages():
            pass

    @pl.when(g > 0)
    def _():
        h_bwd = jnp.concatenate(
            [jnp.concatenate([_scan_out_block(scr, k, j) for j in range(SCAN_SEGS)], axis=0)
             for k in range(d_rg // V7X_LANES)], axis=1)

        gt1 = m_ref[0, 0, 2:3, :]
        yrg = _rms((hf_ref[0] + h_bwd) * gact_ref[0].astype(F32), gmix_ref[0, :, d_gm:o_ft])
        ft_pitch = yft_ref.shape[1] // ft_groups
        ft_rows = ROW_TILE // ft_groups
        yft_raw = jnp.concatenate(
            [yft_ref[0, a * ft_pitch:a * ft_pitch + ft_rows, :] for a in range(ft_groups)], axis=0)
        yft = _dot(yft_raw.astype(BF16), ftw_ref[0]) + ftb_ref[0]
        yft = _rms(yft, gmix_ref[0, :, o_ft:d_mix])
        o = _dot(ygm_ref[0], wout_ref[0, 0:d_gm, :])
        o = o + _dot(yrg.astype(BF16), wout_ref[0, d_gm:o_ft, :])
        o = o + _dot(yft.astype(BF16), wout_ref[0, o_ft:d_mix, :])
        x = x_ref[0] + _rms(o, gt1 * gpost_ref[0])

        sh2 = m_ref[0, 0, 3:4, :]
        sc2 = m_ref[0, 0, 4:5, :]
        gt2 = m_ref[0, 0, 5:6, :]
        h = (_rms(x, gpre2_ref[0] * (1.0 + sc2)) + sh2).astype(BF16)

        scan = scan_stages()
        next(scan)
        d_ff = w1_ref.shape[-1]
        acc = None
        for c in range(d_ff // ff_chunk):
            cols = slice(c * ff_chunk, (c + 1) * ff_chunk)
            a = jnp.maximum(_dot(h, w1_ref[0, :, cols]), 0.0)
            part = _dot((a * a).astype(BF16), w2_ref[0, cols, :])
            acc = part if acc is None else acc + part
            _advance(scan, pl.cdiv(SCAN_STAGES - 1, d_ff // ff_chunk))
        for _ in scan:
            pass
        o_ref[0] = x + _rms(acc, gt2 * gpost2_ref[0])


def _mix_out_ffn(l, xr, wg, ba, bx, lam, h_fwd, x, ygm, gact, yft, mod4, g_mix, ftw, ftb,
                 w_out, g_post, g_pre2, w1, w2, g_post2):
    bsz, seq, d = x.shape
    d_rg = xr.shape[-1]
    d_gm = ygm.shape[-1]
    d_ft = yft.shape[-1]
    d_mix = w_out.shape[1]
    d_ff = w1.shape[-1]
    nt = seq // ROW_TILE
    ntot = bsz * nt

    def scan_bt(g):
        gs = jnp.minimum(g, ntot - 1)
        return gs // nt, nt - 1 - gs % nt

    def tail_bt(g):
        gf = jnp.maximum(g - 1, 0)
        return gf // nt, nt - 1 - gf % nt

    lay = lambda g: (l, 0, 0)
    ldir = lambda g: (l, 1, 0, 0)
    row = lambda g: (*tail_bt(g), 0)
    in_specs = [
        pl.BlockSpec((1, ROW_TILE, d_rg), lambda g: (*scan_bt(g), 0)),
        _resident((1, 1, 2, d_rg // 2, d_rg), lambda g: (l, 1, 0, 0, 0)),
        pl.BlockSpec((1, 1, 1, d_rg), ldir),
        pl.BlockSpec((1, 1, 1, d_rg), ldir),
        pl.BlockSpec((1, 1, 1, d_rg), ldir),
        pl.BlockSpec((1, ROW_TILE, d_rg), row),
        pl.BlockSpec((1, ROW_TILE, d), row),
        pl.BlockSpec((1, ROW_TILE, d_gm), row),
        pl.BlockSpec((1, ROW_TILE, d_rg), row),
        pl.BlockSpec((1, yft.shape[1] // (seq // ROW_TILE), d_ft), row),
        pl.BlockSpec((1, 1, 6, d), lambda g: (l, tail_bt(g)[0], 0, 0)),
        pl.BlockSpec((1, 1, d_mix), lay),
        _resident((1, d_ft, d_ft), lay),
        pl.BlockSpec((1, 1, d_ft), lay),
        _resident((1, d_mix, d), lay),
        pl.BlockSpec((1, 1, d), lay),
        pl.BlockSpec((1, 1, d), lay),
        _resident((1, d, d_ff), lay),
        _resident((1, d_ff, d), lay),
        pl.BlockSpec((1, 1, d), lay),
    ]
    return pl.pallas_call(
        functools.partial(_mix_out_ffn_kernel, nt=nt, ntot=ntot, ff_chunk=d_ff // FF_CHUNKS,
                          ft_groups=ROW_TILE // (seq // FT_N1)),
        grid=(ntot + 1,),
        in_specs=in_specs,
        out_specs=pl.BlockSpec((1, ROW_TILE, d), row),
        out_shape=jax.ShapeDtypeStruct((bsz, seq, d), F32),
        scratch_shapes=_scan_scratch_shapes(d_rg),
        compiler_params=_params("arbitrary"),
        name="mix_out_ffn",
    )(xr, wg, ba, bx, lam, h_fwd, x, ygm, gact, yft, mod4, g_mix, ftw, ftb,
      w_out, g_post, g_pre2, w1, w2, g_post2)


def _block_diag(w, per):
    *lead, n, hd, _ = w.shape
    w = w.reshape(*lead, n // per, per, hd, hd)
    keep = [(0, 0)] * (len(lead) + 2)
    rows = [jnp.pad(w[..., a, :, :], keep + [(a * hd, (per - 1 - a) * hd)]) for a in range(per)]
    return jnp.concatenate(rows, axis=-2)


def kernel(x, c, w_ada, b_ada, g_pre_mix, g_post_mix, w_in, gm_ln_g, gm_ln_b, gm_w_s, gm_b_s,
           rg_conv_w, rg_conv_b, rg_w_a, rg_b_a, rg_w_x, rg_b_x, rg_lam, ft_w, ft_b,
           g_mix_out, w_out, g_pre_ff, g_post_ff, w_ff1, w_ff2):
    bsz, seq, d = x.shape
    depth = w_in.shape[0]
    d_gm = gm_ln_g.shape[-1]
    d_rg = rg_conv_b.shape[-1]
    d_ft = ft_w.shape[1] * ft_w.shape[2]
    assert seq % ROW_TILE == 0 and ROW_TILE % GM_CHUNK == 0 and seq % FT_N1 == 0
    assert d_ft % FT_LANES == 0 and ft_w.shape[2] == FT_GROUP_DIM
    assert rg_conv_w.shape[1] == RG_CONV and rg_w_a.shape[2] == RG_HEADS

    vec = lambda a: a.reshape(depth, 1, a.shape[-1])
    w_in_b = w_in.astype(BF16)
    w_out_b = w_out.astype(BF16)
    w_ff1_b = w_ff1.astype(BF16)
    w_ff2_b = w_ff2.astype(BF16)
    ws = gm_w_s.reshape(depth, GM_HEADS * GM_CHUNK, GM_CHUNK).astype(BF16)
    bs = jnp.repeat(jnp.swapaxes(gm_b_s, 1, 2), d_gm // GM_HEADS, axis=2)
    per = RG_HEADS // 2
    wg = (0.5 * jnp.concatenate([_block_diag(rg_w_a, per), _block_diag(rg_w_x, per)], axis=-1)).astype(BF16)
    ba = 0.5 * rg_b_a.reshape(depth, 2, 1, d_rg)
    bx = 0.5 * rg_b_x.reshape(depth, 2, 1, d_rg)
    lam = rg_lam.reshape(depth, 2, 1, d_rg)
    conv_w = rg_conv_w.reshape(depth, 1, RG_CONV, d_rg)
    ftw = _block_diag(ft_w, FT_GROUPS).reshape(depth, d_ft, d_ft).astype(BF16)
    ftb = ft_b.reshape(depth, 1, d_ft)
    tables = _fourier_tables(seq)

    mod4 = _modulation(c, w_ada, b_ada).reshape(depth, bsz, 6, d)

    for l in range(depth):
        gates = (wg, ba, bx, lam)
        ygm, gact, xr, ftx, h_fwd = _mix_in_scan(l, x, mod4, vec(g_pre_mix), w_in_b, vec(gm_ln_g),
                                                 vec(gm_ln_b), ws, bs, vec(g_mix_out), conv_w,
                                                 vec(rg_conv_b), *gates, d_gm=d_gm, d_rg=d_rg)
        yft = _fourier(ftx, tables)
        x = _mix_out_ffn(l, xr, *gates, h_fwd, x, ygm, gact, yft, mod4, vec(g_mix_out), ftw, ftb, w_out_b,
                         vec(g_post_mix), vec(g_pre_ff), w_ff1_b, w_ff2_b, vec(g_post_ff))
    return x
```

```python
import functools
import math

import numpy as np
import jax
import jax.numpy as jnp
from jax import lax
from jax.experimental import pallas as pl
from jax.experimental.pallas import tpu as pltpu

F32 = jnp.float32
BF16 = jnp.bfloat16

EPS = 1e-6
RG_C = 8.0
RG_CONV = 4
RG_CONV_LEFT = 2
GM_HEADS = 4
GM_CHUNK = 128
RG_HEADS = 8
FT_GROUPS = 4
FT_GROUP_DIM = 64

V7X_SUBLANES = 8
V7X_LANES = 128
V7X_VMEM_LIMIT_BYTES = 56 * 1024 * 1024

MOD_COL_BLOCKS = 2
ROW_TILE = 512
FF_CHUNKS = 8
SCAN_ROW_PARTS = 2
SCAN_STAGES = 3 * SCAN_ROW_PARTS + 2
SCAN_SEGS = V7X_SUBLANES
SCAN_SEG_LEN = ROW_TILE // SCAN_SEGS
SCAN_SEG_STRIDE = SCAN_SEG_LEN + V7X_SUBLANES
SEG_PITCH = SCAN_SEG_LEN + 3 * V7X_SUBLANES

FT_N1 = 64
FT_PAD = FT_N1 + V7X_SUBLANES
FT_LANES = 128
FT_OUT_PAD = V7X_SUBLANES
FT_UNROLL_A = 8
FT_UNROLL_B = 16


def _gelu(x):
    c = math.sqrt(2.0 / math.pi)
    t = jnp.tanh(x * ((c * 0.044715) * (x * x) + c))
    return (0.5 * x) * (t + 1.0)


def _sigmoid(x):
    return 0.5 * jnp.tanh(0.5 * x) + 0.5


def _rms(x, g):
    return x * lax.rsqrt(jnp.mean(x * x, axis=-1, keepdims=True) + EPS) * g


def _dot(a, b):
    return jnp.dot(a, b, preferred_element_type=F32)


def _params(*sem):
    return pltpu.CompilerParams(dimension_semantics=sem, vmem_limit_bytes=V7X_VMEM_LIMIT_BYTES)


def _resident(shape, index_map):
    return pl.BlockSpec(shape, index_map, pipeline_mode=pl.Buffered(1))


def _mod_kernel(ct_ref, w_ref, b_ref, o_ref):
    ct = ct_ref[...]
    cond = ct * _sigmoid(ct)
    d = ct.shape[0]
    for j in range(w_ref.shape[-1] // d):
        cols = slice(j * d, (j + 1) * d)
        w = w_ref[0, :, cols]
        for b in range(ct.shape[1]):
            o_ref[0, b:b + 1, cols] = jnp.sum(w * cond[:, b:b + 1], axis=0, keepdims=True) + b_ref[0, :, cols]


def _modulation(c, w_ada, b_ada):
    depth, d, d6 = w_ada.shape
    bsz = c.shape[0]
    wcols = d6 // MOD_COL_BLOCKS
    return pl.pallas_call(
        _mod_kernel,
        grid=(depth, MOD_COL_BLOCKS),
        in_specs=[
            pl.BlockSpec((d, bsz), lambda l, j: (0, 0)),
            pl.BlockSpec((1, d, wcols), lambda l, j: (l, 0, j)),
            pl.BlockSpec((1, 1, wcols), lambda l, j: (l, 0, j)),
        ],
        out_specs=pl.BlockSpec((1, bsz, wcols), lambda l, j: (l, 0, j)),
        out_shape=jax.ShapeDtypeStruct((depth, bsz, d6), F32),
        compiler_params=_params("arbitrary", "arbitrary"),
        name="modulation",
    )(c.T, w_ada, b_ada.reshape(depth, 1, d6))


def _fourier_tables(seq):
    n1 = FT_N1
    n2 = seq // n1
    gd = FT_GROUP_DIM
    j = np.arange(gd)
    ang = 2.0 * np.pi * np.outer(j, j) / gd
    eye = np.eye(FT_LANES // gd)
    w1 = np.concatenate([np.kron(eye, np.cos(ang)), -np.kron(eye, np.sin(ang))], axis=1)
    k2 = np.arange(n2)[None, :, None]
    s2 = np.arange(n2)[None, None, :]
    s1 = np.arange(n1)[:, None, None]
    ang_a = 2.0 * np.pi * ((k2 * (n1 * s2 + s1)) % seq) / seq
    ta = np.concatenate([np.cos(ang_a), np.sin(ang_a)], axis=2)
    i1 = np.arange(n1)
    ang_b = 2.0 * np.pi * np.outer(i1, i1) / n1
    tb = np.concatenate([np.cos(ang_b), np.sin(ang_b)], axis=1)
    return (jnp.asarray(w1, F32), jnp.asarray(ta, F32), jnp.asarray(tb, F32))


def _fourier_kernel(x_ref, w1_ref, ta_ref, tb_ref, wf_ref, o_ref, wb_ref, p_ref, *, seq, scale):
    n1 = FT_N1
    n2 = seq // n1
    ln = FT_LANES

    wb_ref[...] = wf_ref[0].astype(BF16)

    w1 = w1_ref[...].astype(BF16)
    for c in range(seq // ROW_TILE):
        p = _dot(x_ref[0, c * ROW_TILE:(c + 1) * ROW_TILE, :].astype(BF16), w1)
        for r in range(ROW_TILE // n1):
            dst = pl.ds((c * (ROW_TILE // n1) + r) * FT_PAD, n1)
            p_ref[0, dst, :] = p[r * n1:(r + 1) * n1, 0:ln]
            p_ref[1, dst, :] = p[r * n1:(r + 1) * n1, ln:2 * ln]

    def stage_a(u, carry):
        res = []
        for d in range(FT_UNROLL_A):
            s1 = u * FT_UNROLL_A + d
            rows = pl.ds(s1, n2, stride=FT_PAD)
            g = jnp.concatenate([p_ref[0, rows, :], p_ref[1, rows, :]], axis=0).astype(BF16)
            t = ta_ref[s1]
            t_im = jnp.concatenate([-t[:, n2:2 * n2], t[:, 0:n2]], axis=1)
            res.append((rows, _dot(t.astype(BF16), g), _dot(t_im.astype(BF16), g)))
        for rows, b_re, b_im in res:
            p_ref[0, rows, :] = b_re
            p_ref[1, rows, :] = b_im
        return carry

    lax.fori_loop(0, n1 // FT_UNROLL_A, stage_a, 0)

    tb = tb_ref[...].astype(BF16)
    pitch = n2 + FT_OUT_PAD
    for k1 in range(n1):
        o_ref[0, k1 * pitch + n2:(k1 + 1) * pitch, :] = jnp.zeros((FT_OUT_PAD, ln), F32)

    def stage_b(u, carry):
        for d in range(FT_UNROLL_B):
            k2 = u * FT_UNROLL_B + d
            rows = pl.ds(pl.multiple_of(k2 * FT_PAD, V7X_SUBLANES), n1)
            blk = jnp.concatenate([p_ref[0, rows, :], p_ref[1, rows, :]], axis=0).astype(BF16)
            o_ref[0, pl.ds(k2, n1, stride=pitch), :] = _dot(tb, blk) * scale
        return carry

    lax.fori_loop(0, n2 // FT_UNROLL_B, stage_b, 0)


def _fourier(l, ftx, tables, w_ff1):
    bsz, seq, d_ft = ftx.shape
    w1, ta, tb = tables
    n1 = FT_N1
    n2 = seq // n1
    nj = d_ft // FT_LANES
    _, d, d_ff = w_ff1.shape
    wrows = d // (bsz * nj)
    blk = pl.BlockSpec((1, seq, FT_LANES), lambda b, j: (b, 0, j))
    return pl.pallas_call(
        functools.partial(_fourier_kernel, seq=seq, scale=1.0 / math.sqrt(seq * FT_GROUP_DIM)),
        grid=(bsz, nj),
        in_specs=[
            blk,
            pl.BlockSpec((FT_LANES, 2 * FT_LANES), lambda b, j: (0, 0)),
            _resident((n1, n2, 2 * n2), lambda b, j: (0, 0, 0)),
            pl.BlockSpec((n1, 2 * n1), lambda b, j: (0, 0)),
            pl.BlockSpec((1, wrows, d_ff), lambda b, j: (l, b * nj + j, 0)),
        ],
        out_specs=[
            pl.BlockSpec((1, n1 * (n2 + FT_OUT_PAD), FT_LANES), lambda b, j: (b, 0, j)),
            pl.BlockSpec((wrows, d_ff), lambda b, j: (b * nj + j, 0)),
        ],
        out_shape=[
            jax.ShapeDtypeStruct((bsz, n1 * (n2 + FT_OUT_PAD), d_ft), F32),
            jax.ShapeDtypeStruct((d, d_ff), BF16),
        ],
        scratch_shapes=[pltpu.VMEM((2, n2 * FT_PAD, FT_LANES), F32)],
        compiler_params=_params("arbitrary", "arbitrary"),
        name="fourier",
    )(ftx, w1, ta, tb, w_ff1)


class _ScanScratch:
    def __init__(self, a, b, h, e, p, c, carry):
        self.a, self.b, self.h = a, b, h
        self.e, self.p, self.c, self.carry = e, p, c, carry


def _scan_scratch_shapes(d_rg):
    nblk = d_rg // V7X_LANES
    pad_rows = SCAN_SEGS * SCAN_SEG_STRIDE
    return [
        pltpu.VMEM((nblk, ROW_TILE, V7X_LANES), F32),
        pltpu.VMEM((nblk, ROW_TILE, V7X_LANES), F32),
        pltpu.VMEM((nblk, pad_rows, V7X_LANES), F32),
        pltpu.VMEM((SCAN_SEGS, d_rg), F32),
        pltpu.VMEM((SCAN_SEGS, d_rg), F32),
        pltpu.VMEM((SCAN_SEGS, d_rg), F32),
        pltpu.VMEM((1, d_rg), F32),
    ]


def _rg_scan_tile(conv_part, first_step, wg_ref, ba_ref, bx_ref, lam_ref, scr, *, reverse):
    tm = ROW_TILE
    d_rg = scr.e.shape[-1]
    half = d_rg // 2
    nblk = d_rg // V7X_LANES

    nl = -lam_ref[0, 0]
    c8h = (-0.5 * RG_C) * (jnp.maximum(nl, 0.0) + jnp.log1p(jnp.exp(-jnp.abs(nl))))

    rh = tm // SCAN_ROW_PARTS
    for part in range(SCAN_ROW_PARTS):
        xr = conv_part(part)
        xrb = xr.astype(BF16)
        yield
        for hh in range(2):
            cols = slice(hh * half, (hh + 1) * half)
            pre = _dot(xrb[:, cols], wg_ref[0, 0, hh])
            t_r = jnp.tanh(pre[:, 0:half] + ba_ref[0, 0, :, cols])
            t_i = jnp.tanh(pre[:, half:2 * half] + bx_ref[0, 0, :, cols])
            log_a = c8h[:, cols] * t_r + c8h[:, cols]
            a = jnp.exp(log_a)
            th = jnp.tanh(log_a)
            nth = -0.5 * th
            sq = jnp.where(nth > 0.0, nth * lax.rsqrt(nth), 0.0) * lax.rsqrt(1.0 - th)
            bq = sq * ((t_i + 1.0) * xr[:, cols])
            for kk in range(half // V7X_LANES):
                k = hh * (half // V7X_LANES) + kk
                lanes = slice(kk * V7X_LANES, (kk + 1) * V7X_LANES)
                scr.a[k, part * rh:(part + 1) * rh, :] = a[:, lanes]
                scr.b[k, part * rh:(part + 1) * rh, :] = bq[:, lanes]
            yield

    steps = range(SCAN_SEG_LEN - 1, -1, -1) if reverse else range(SCAN_SEG_LEN)

    def at_step(t):
        return slice(t * SCAN_SEGS, (t + 1) * SCAN_SEGS)

    for k in range(nblk):
        lanes = slice(k * V7X_LANES, (k + 1) * V7X_LANES)
        e = jnp.zeros((SCAN_SEGS, V7X_LANES), F32)
        p = e + 1.0
        for t in steps:
            at = scr.a[k, at_step(t), :]
            e = at * e + scr.b[k, at_step(t), :]
            p = at * p
        scr.e[:, lanes] = e
        scr.p[:, lanes] = p
    yield

    c = jnp.where(first_step, 0.0, scr.carry[...])
    order = range(SCAN_SEGS - 1, -1, -1) if reverse else range(SCAN_SEGS)
    for j in order:
        scr.c[j:j + 1, :] = c
        c = scr.p[j:j + 1, :] * c + scr.e[j:j + 1, :]
    scr.carry[...] = c

    for k in range(nblk):
        h = scr.c[:, k * V7X_LANES:(k + 1) * V7X_LANES]
        for t in steps:
            h = scr.a[k, at_step(t), :] * h + scr.b[k, at_step(t), :]
            scr.h[k, pl.ds(t, SCAN_SEGS, stride=SCAN_SEG_STRIDE), :] = h


def _advance(stages, n):
    for _ in range(n):
        next(stages, None)


def _scan_out_block(scr, k, j):
    return scr.h[k, j * SCAN_SEG_STRIDE:j * SCAN_SEG_STRIDE + SCAN_SEG_LEN, :]


def _mix_in_scan_kernel(x_ref, m_ref, gpre_ref, win_ref, lng_ref, lnb_ref, ws_ref, bs_ref, gmix_ref,
                        cw_ref, cb_ref, wg_ref, ba_ref, bx_ref, lam_ref, wf_ref,
                        ygm_ref, gact_ref, xr_ref, ftx_ref, hf_ref, wb_ref, ext_ref, segp_ref, *scratch,
                        nt, ntot, d_gm, d_rg):
    scr = _ScanScratch(*scratch)
    g = pl.program_id(0)
    tm = ROW_TILE
    halo = V7X_SUBLANES
    o_v, o_g, o_x, o_f = d_gm, 2 * d_gm, 2 * d_gm + d_rg, 2 * d_gm + 2 * d_rg

    wb_ref[...] = wf_ref[0].astype(BF16)

    def step(project, scan_on):
        gs = (g - 1) % nt
        if project:
            x = x_ref[0]
            sh1 = m_ref[0, 0, 0:1, :]
            sc1 = m_ref[0, 0, 1:2, :]
            h = (_rms(x, gpre_ref[0] * (1.0 + sc1)) + sh1).astype(BF16)
            rgx = _dot(h, win_ref[0, :, o_x:o_f])

        def conv_part(part):
            nblk = d_rg // V7X_LANES
            seg_rows = SCAN_SEG_LEN + 2 * halo
            if part == 0:
                nxt = (jnp.where(gs == nt - 1, 0.0, rgx[0:halo]) if project
                       else jnp.zeros((halo, d_rg), F32))
                ext_ref[halo + tm:halo + tm + halo, :] = nxt
                for k in range(nblk):
                    for j in range(SCAN_SEGS):
                        segp_ref[k, j * SEG_PITCH:j * SEG_PITCH + seg_rows, :] = (
                            ext_ref[j * SCAN_SEG_LEN:j * SCAN_SEG_LEN + seg_rows,
                                    k * V7X_LANES:(k + 1) * V7X_LANES])
            steps = SCAN_SEG_LEN // SCAN_ROW_PARTS
            cols = []
            for k in range(nblk):
                lanes = slice(k * V7X_LANES, (k + 1) * V7X_LANES)
                taps = [jnp.broadcast_to(cw_ref[0, 0, i:i + 1, lanes], (SCAN_SEGS, V7X_LANES))
                        for i in range(RG_CONV)]
                bias = jnp.broadcast_to(cb_ref[0, :, lanes], (SCAN_SEGS, V7X_LANES))
                slabs = [segp_ref[k, pl.ds(halo - RG_CONV_LEFT + part * steps + i, SCAN_SEGS, stride=SEG_PITCH), :]
                         for i in range(steps + RG_CONV - 1)]
                out = []
                for t in range(steps):
                    acc = bias
                    for i in range(RG_CONV):
                        acc = acc + slabs[t + i] * taps[i]
                    out.append(acc)
                cols.append(jnp.concatenate(out, axis=0))
            xr = jnp.concatenate(cols, axis=1)
            rh = tm // SCAN_ROW_PARTS
            xr_ref[0, part * rh:(part + 1) * rh, :] = xr
            return xr

        if scan_on:
            scan = _rg_scan_tile(conv_part, gs == 0, wg_ref, ba_ref, bx_ref, lam_ref, scr, reverse=False)
        else:
            scr.carry[...] = jnp.zeros_like(scr.carry)
            scan = iter(())
        if not project:
            for _ in scan:
                pass
        else:
            _advance(scan, 1)
            v = _gelu(_dot(h, win_ref[0, :, o_v:o_g]))
            mu = jnp.mean(v, axis=-1, keepdims=True)
            vc = v - mu
            var = jnp.mean(vc * vc, axis=-1, keepdims=True)
            vn = (vc * lax.rsqrt(var + EPS) * lng_ref[0] + lnb_ref[0]).astype(BF16)
            _advance(scan, 1)
            gact_ref[0] = _gelu(_dot(h, win_ref[0, :, o_g:o_x])).astype(BF16)
            _advance(scan, 3 * (SCAN_ROW_PARTS - 1) + 1 - 2)
            if scan_on:
                ext_ref[0:halo, :] = jnp.where(g % nt == 0, 0.0, ext_ref[tm:tm + halo, :])
            else:
                ext_ref[0:halo, :] = jnp.zeros((halo, d_rg), F32)
            ext_ref[halo:halo + tm, :] = rgx
            ftx_ref[0] = _dot(h, win_ref[0, :, o_f:o_f + d_gm]).astype(BF16)
            _advance(scan, 1)
            u = _gelu(_dot(h, win_ref[0, :, 0:o_v]))
            _advance(scan, 1)
            head_dim = d_gm // GM_HEADS
            head = lax.broadcasted_iota(jnp.int32, (GM_CHUNK, d_gm), 1) // head_dim
            for c in range(tm // GM_CHUNK):
                rows = slice(c * GM_CHUNK, (c + 1) * GM_CHUNK)
                r = _dot(ws_ref[0], vn[rows])
                s = r[0:GM_CHUNK]
                for hh in range(1, GM_HEADS):
                    s = jnp.where(head == hh, r[hh * GM_CHUNK:(hh + 1) * GM_CHUNK], s)
                y = u[rows] * (s + bs_ref[0])
                ygm_ref[0, rows, :] = _rms(y, gmix_ref[0, :, 0:d_gm]).astype(BF16)
                if c == 1:
                    _advance(scan, 1)
            for _ in scan:
                pass
        if scan_on:
            for k in range(d_rg // V7X_LANES):
                for j in range(SCAN_SEGS):
                    hf_ref[0, j * SCAN_SEG_LEN:(j + 1) * SCAN_SEG_LEN,
                           k * V7X_LANES:(k + 1) * V7X_LANES] = _scan_out_block(scr, k, j)

    pl.when(g == 0)(functools.partial(step, True, False))
    pl.when((g > 0) & (g < ntot))(functools.partial(step, True, True))
    pl.when(g == ntot)(functools.partial(step, False, True))


def _mix_in_scan(l, x, mod4, g_pre, w_in, ln_g, ln_b, ws, bs, g_mix, conv_w, conv_b, wg, ba, bx, lam,
                 w_ff2, *, d_gm, d_rg):
    bsz, seq, d = x.shape
    d_in = w_in.shape[-1]
    nt = seq // ROW_TILE
    ntot = bsz * nt
    d_ff = w_ff2.shape[1]
    wrows = d_ff // ntot
    wblk = lambda g: jnp.minimum(g, ntot - 1)

    def proj_bt(g):
        gp = jnp.minimum(g, ntot - 1)
        return gp // nt, gp % nt

    def scan_bt(g):
        gs = jnp.maximum(g - 1, 0)
        return gs // nt, gs % nt

    lay = lambda g: (l, 0, 0)
    ldir = lambda g: (l, 0, 0, 0)
    row = lambda g: (*proj_bt(g), 0)
    return pl.pallas_call(
        functools.partial(_mix_in_scan_kernel, nt=nt, ntot=ntot, d_gm=d_gm, d_rg=d_rg),
        grid=(ntot + 1,),
        in_specs=[
            pl.BlockSpec((1, ROW_TILE, d), row),
            pl.BlockSpec((1, 1, 6, d), lambda g: (l, proj_bt(g)[0], 0, 0)),
            pl.BlockSpec((1, 1, d), lay),
            _resident((1, d, d_in), lay),
            pl.BlockSpec((1, 1, d_gm), lay),
            pl.BlockSpec((1, 1, d_gm), lay),
            pl.BlockSpec((1, GM_HEADS * GM_CHUNK, GM_CHUNK), lay),
            pl.BlockSpec((1, GM_CHUNK, d_gm), lay),
            pl.BlockSpec((1, 1, d), lay),
            pl.BlockSpec((1, 1, RG_CONV, d_rg), ldir),
            pl.BlockSpec((1, 1, d_rg), lay),
            _resident((1, 1, 2, d_rg // 2, d_rg), lambda g: (l, 0, 0, 0, 0)),
            pl.BlockSpec((1, 1, 1, d_rg), ldir),
            pl.BlockSpec((1, 1, 1, d_rg), ldir),
            pl.BlockSpec((1, 1, 1, d_rg), ldir),
            pl.BlockSpec((1, wrows, d), lambda g: (l, wblk(g), 0)),
        ],
        out_specs=[
            pl.BlockSpec((1, ROW_TILE, d_gm), row),
            pl.BlockSpec((1, ROW_TILE, d_rg), row),
            pl.BlockSpec((1, ROW_TILE, d_rg), lambda g: (*scan_bt(g), 0)),
            pl.BlockSpec((1, ROW_TILE, d_gm), row),
            pl.BlockSpec((1, ROW_TILE, d_rg), lambda g: (*scan_bt(g), 0)),
            pl.BlockSpec((wrows, d), lambda g: (wblk(g), 0)),
        ],
        out_shape=[
            jax.ShapeDtypeStruct((bsz, seq, d_gm), BF16),
            jax.ShapeDtypeStruct((bsz, seq, d_rg), BF16),
            jax.ShapeDtypeStruct((bsz, seq, d_rg), F32),
            jax.ShapeDtypeStruct((bsz, seq, d_gm), BF16),
            jax.ShapeDtypeStruct((bsz, seq, d_rg), F32),
            jax.ShapeDtypeStruct((d_ff, d), BF16),
        ],
        scratch_shapes=[
            pltpu.VMEM((ROW_TILE + 2 * V7X_SUBLANES, d_rg), F32),
            pltpu.VMEM((d_rg // V7X_LANES, SCAN_SEGS * SEG_PITCH, V7X_LANES), F32),
        ] + _scan_scratch_shapes(d_rg),
        compiler_params=_params("arbitrary"),
        name="mix_in_scan",
    )(x, mod4, g_pre, w_in, ln_g, ln_b, ws, bs, g_mix, conv_w, conv_b, wg, ba, bx, lam, w_ff2)


def _mix_out_ffn_kernel(xr_ref, wg_ref, ba_ref, bx_ref, lam_ref,
                        hf_ref, x_ref, ygm_ref, gact_ref, yft_ref, m_ref, gmix_ref, ftw_ref, ftb_ref,
                        wout_ref, gpost_ref, gpre2_ref, w1_ref, w2_ref, gpost2_ref,
                        o_ref, *scratch, nt, ntot, ff_chunk, ft_groups):
    scr = _ScanScratch(*scratch)
    g = pl.program_id(0)
    d_rg = xr_ref.shape[-1]
    d_gm = ygm_ref.shape[-1]
    o_ft = d_gm + d_rg
    d_mix = wout_ref.shape[1]
    gs = jnp.minimum(g, ntot - 1) % nt

    def conv_part(part):
        rh = ROW_TILE // SCAN_ROW_PARTS
        return xr_ref[0, part * rh:(part + 1) * rh, :]

    def scan_stages():
        return _rg_scan_tile(conv_part, gs == 0, wg_ref, ba_ref, bx_ref, lam_ref, scr, reverse=True)

    @pl.when(g == 0)
    def _():
        scr.carry[...] = jnp.zeros_like(scr.carry)
        for _ in scan_stages():
            pass

    @pl.when(g > 0)
    def _():
        h_bwd = jnp.concatenate(
            [jnp.concatenate([_scan_out_block(scr, k, j) for j in range(SCAN_SEGS)], axis=0)
             for k in range(d_rg // V7X_LANES)], axis=1)

        gt1 = m_ref[0, 0, 2:3, :]
        yrg = _rms((hf_ref[0] + h_bwd) * gact_ref[0].astype(F32), gmix_ref[0, :, d_gm:o_ft])
        ft_pitch = yft_ref.shape[1] // ft_groups
        ft_rows = ROW_TILE // ft_groups
        yft_raw = jnp.concatenate(
            [yft_ref[0, a * ft_pitch:a * ft_pitch + ft_rows, :] for a in range(ft_groups)], axis=0)
        yft = _dot(yft_raw.astype(BF16), ftw_ref[0]) + ftb_ref[0]
        yft = _rms(yft, gmix_ref[0, :, o_ft:d_mix])
        o = _dot(ygm_ref[0], wout_ref[0, 0:d_gm, :])
        o = o + _dot(yrg.astype(BF16), wout_ref[0, d_gm:o_ft, :])
        o = o + _dot(yft.astype(BF16), wout_ref[0, o_ft:d_mix, :])
        x = x_ref[0] + _rms(o, gt1 * gpost_ref[0])

        sh2 = m_ref[0, 0, 3:4, :]
        sc2 = m_ref[0, 0, 4:5, :]
        gt2 = m_ref[0, 0, 5:6, :]
        h = (_rms(x, gpre2_ref[0] * (1.0 + sc2)) + sh2).astype(BF16)

        scan = scan_stages()
        next(scan)
        d_ff = w1_ref.shape[-1]
        acc = None
        for c in range(d_ff // ff_chunk):
            cols = slice(c * ff_chunk, (c + 1) * ff_chunk)
            a = jnp.maximum(_dot(h, w1_ref[:, cols]), 0.0)
            part = _dot((a * a).astype(BF16), w2_ref[cols, :])
            acc = part if acc is None else acc + part
            _advance(scan, pl.cdiv(SCAN_STAGES - 1, d_ff // ff_chunk))
        for _ in scan:
            pass
        o_ref[0] = x + _rms(acc, gt2 * gpost2_ref[0])


def _mix_out_ffn(l, xr, wg, ba, bx, lam, h_fwd, x, ygm, gact, yft, mod4, g_mix, ftw, ftb,
                 w_out, g_post, g_pre2, w1, w2, g_post2):
    bsz, seq, d = x.shape
    d_rg = xr.shape[-1]
    d_gm = ygm.shape[-1]
    d_ft = yft.shape[-1]
    d_mix = w_out.shape[1]
    d_ff = w1.shape[-1]
    nt = seq // ROW_TILE
    ntot = bsz * nt

    def scan_bt(g):
        gs = jnp.minimum(g, ntot - 1)
        return gs // nt, nt - 1 - gs % nt

    def tail_bt(g):
        gf = jnp.maximum(g - 1, 0)
        return gf // nt, nt - 1 - gf % nt

    lay = lambda g: (l, 0, 0)
    ldir = lambda g: (l, 1, 0, 0)
    row = lambda g: (*tail_bt(g), 0)
    in_specs = [
        pl.BlockSpec((1, ROW_TILE, d_rg), lambda g: (*scan_bt(g), 0)),
        _resident((1, 1, 2, d_rg // 2, d_rg), lambda g: (l, 1, 0, 0, 0)),
        pl.BlockSpec((1, 1, 1, d_rg), ldir),
        pl.BlockSpec((1, 1, 1, d_rg), ldir),
        pl.BlockSpec((1, 1, 1, d_rg), ldir),
        pl.BlockSpec((1, ROW_TILE, d_rg), row),
        pl.BlockSpec((1, ROW_TILE, d), row),
        pl.BlockSpec((1, ROW_TILE, d_gm), row),
        pl.BlockSpec((1, ROW_TILE, d_rg), row),
        pl.BlockSpec((1, yft.shape[1] // (seq // ROW_TILE), d_ft), row),
        pl.BlockSpec((1, 1, 6, d), lambda g: (l, tail_bt(g)[0], 0, 0)),
        pl.BlockSpec((1, 1, d_mix), lay),
        _resident((1, d_ft, d_ft), lay),
        pl.BlockSpec((1, 1, d_ft), lay),
        _resident((1, d_mix, d), lay),
        pl.BlockSpec((1, 1, d), lay),
        pl.BlockSpec((1, 1, d), lay),
        _resident((d, d_ff), lambda g: (0, 0)),
        _resident((d_ff, d), lambda g: (0, 0)),
        pl.BlockSpec((1, 1, d), lay),
    ]
    return pl.pallas_call(
        functools.partial(_mix_out_ffn_kernel, nt=nt, ntot=ntot, ff_chunk=d_ff // FF_CHUNKS,
                          ft_groups=ROW_TILE // (seq // FT_N1)),
        grid=(ntot + 1,),
        in_specs=in_specs,
        out_specs=pl.BlockSpec((1, ROW_TILE, d), row),
        out_shape=jax.ShapeDtypeStruct((bsz, seq, d), F32),
        scratch_shapes=_scan_scratch_shapes(d_rg),
        compiler_params=_params("arbitrary"),
        name="mix_out_ffn",
    )(xr, wg, ba, bx, lam, h_fwd, x, ygm, gact, yft, mod4, g_mix, ftw, ftb,
      w_out, g_post, g_pre2, w1, w2, g_post2)


def _block_diag(w, per):
    *lead, n, hd, _ = w.shape
    w = w.reshape(*lead, n // per, per, hd, hd)
    keep = [(0, 0)] * (len(lead) + 2)
    rows = [jnp.pad(w[..., a, :, :], keep + [(a * hd, (per - 1 - a) * hd)]) for a in range(per)]
    return jnp.concatenate(rows, axis=-2)


def kernel(x, c, w_ada, b_ada, g_pre_mix, g_post_mix, w_in, gm_ln_g, gm_ln_b, gm_w_s, gm_b_s,
           rg_conv_w, rg_conv_b, rg_w_a, rg_b_a, rg_w_x, rg_b_x, rg_lam, ft_w, ft_b,
           g_mix_out, w_out, g_pre_ff, g_post_ff, w_ff1, w_ff2):
    bsz, seq, d = x.shape
    depth = w_in.shape[0]
    d_gm = gm_ln_g.shape[-1]
    d_rg = rg_conv_b.shape[-1]
    d_ft = ft_w.shape[1] * ft_w.shape[2]
    assert seq % ROW_TILE == 0 and ROW_TILE % GM_CHUNK == 0 and seq % FT_N1 == 0
    assert d_ft % FT_LANES == 0 and ft_w.shape[2] == FT_GROUP_DIM
    assert rg_conv_w.shape[1] == RG_CONV and rg_w_a.shape[2] == RG_HEADS

    vec = lambda a: a.reshape(depth, 1, a.shape[-1])
    w_in_b = w_in.astype(BF16)
    w_out_b = w_out.astype(BF16)
    ws = gm_w_s.reshape(depth, GM_HEADS * GM_CHUNK, GM_CHUNK).astype(BF16)
    bs = jnp.repeat(jnp.swapaxes(gm_b_s, 1, 2), d_gm // GM_HEADS, axis=2)
    per = RG_HEADS // 2
    wg = (0.5 * jnp.concatenate([_block_diag(rg_w_a, per), _block_diag(rg_w_x, per)], axis=-1)).astype(BF16)
    ba = 0.5 * rg_b_a.reshape(depth, 2, 1, d_rg)
    bx = 0.5 * rg_b_x.reshape(depth, 2, 1, d_rg)
    lam = rg_lam.reshape(depth, 2, 1, d_rg)
    conv_w = rg_conv_w.reshape(depth, 1, RG_CONV, d_rg)
    ftw = _block_diag(ft_w, FT_GROUPS).reshape(depth, d_ft, d_ft).astype(BF16)
    ftb = ft_b.reshape(depth, 1, d_ft)
    tables = _fourier_tables(seq)

    mod4 = _modulation(c, w_ada, b_ada).reshape(depth, bsz, 6, d)

    for l in range(depth):
        gates = (wg, ba, bx, lam)
        ygm, gact, xr, ftx, h_fwd, w2_b = _mix_in_scan(l, x, mod4, vec(g_pre_mix), w_in_b, vec(gm_ln_g),
                                                       vec(gm_ln_b), ws, bs, vec(g_mix_out), conv_w,
                                                       vec(rg_conv_b), *gates, w_ff2, d_gm=d_gm, d_rg=d_rg)
        yft, w1_b = _fourier(l, ftx, tables, w_ff1)
        x = _mix_out_ffn(l, xr, *gates, h_fwd, x, ygm, gact, yft, mod4, vec(g_mix_out), ftw, ftb, w_out_b,
                         vec(g_post_mix), vec(g_pre_ff), w1_b, w2_b, vec(g_post_ff))
    return x
```

```python
import functools
import math

import numpy as np
import jax
import jax.numpy as jnp
from jax import lax
from jax.experimental import pallas as pl
from jax.experimental.pallas import tpu as pltpu

F32 = jnp.float32
BF16 = jnp.bfloat16

EPS = 1e-6
RG_C = 8.0
RG_CONV = 4
RG_CONV_LEFT = 2
GM_HEADS = 4
GM_CHUNK = 128
RG_HEADS = 8
FT_GROUPS = 4
FT_GROUP_DIM = 64

V7X_SUBLANES = 8
V7X_LANES = 128
V7X_VMEM_LIMIT_BYTES = 56 * 1024 * 1024

MOD_COL_BLOCKS = 2
ROW_TILE = 512
FF_CHUNKS = 8
SCAN_ROW_PARTS = 2
SCAN_STAGES = 3 * SCAN_ROW_PARTS + 2
SCAN_SEGS = V7X_SUBLANES
SCAN_SEG_LEN = ROW_TILE // SCAN_SEGS
SCAN_SEG_STRIDE = SCAN_SEG_LEN + V7X_SUBLANES
SEG_PITCH = SCAN_SEG_LEN + 3 * V7X_SUBLANES

FT_N1 = 64
FT_PAD = FT_N1 + V7X_SUBLANES
FT_LANES = 128
FT_OUT_PAD = V7X_SUBLANES
FT_UNROLL_A = 8
FT_UNROLL_B = 16


def _gelu(x):
    c = math.sqrt(2.0 / math.pi)
    t = jnp.tanh(x * ((c * 0.044715) * (x * x) + c))
    return (0.5 * x) * (t + 1.0)


def _sigmoid(x):
    return 0.5 * jnp.tanh(0.5 * x) + 0.5


def _rms(x, g):
    return x * lax.rsqrt(jnp.mean(x * x, axis=-1, keepdims=True) + EPS) * g


def _dot(a, b):
    return jnp.dot(a, b, preferred_element_type=F32)


def _params(*sem):
    return pltpu.CompilerParams(dimension_semantics=sem, vmem_limit_bytes=V7X_VMEM_LIMIT_BYTES)


def _resident(shape, index_map):
    return pl.BlockSpec(shape, index_map, pipeline_mode=pl.Buffered(1))


class _LayerVecs:
    def __init__(self, named):
        self.range = {}
        self.total = 0
        for name, v in named:
            n = v.shape[-1]
            assert n % V7X_LANES == 0
            self.range[name] = (self.total, n)
            self.total += n

    def pack(self, named):
        return jnp.concatenate([v for _, v in named], axis=-1)[:, None, :]

    def get(self, ref, name, lo=0, hi=None):
        start, n = self.range[name]
        return ref[0, :, start + lo:start + (n if hi is None else hi)]

    def spec(self, l):
        return pl.BlockSpec((1, 1, self.total), lambda *g: (l, 0, 0))


def _mod_kernel(ct_ref, w_ref, b_ref, o_ref):
    ct = ct_ref[...]
    cond = ct * _sigmoid(ct)
    d = ct.shape[0]
    for j in range(w_ref.shape[-1] // d):
        cols = slice(j * d, (j + 1) * d)
        w = w_ref[0, :, cols]
        for b in range(ct.shape[1]):
            o_ref[0, b:b + 1, cols] = jnp.sum(w * cond[:, b:b + 1], axis=0, keepdims=True) + b_ref[0, :, cols]


def _modulation(c, w_ada, b_ada):
    depth, d, d6 = w_ada.shape
    bsz = c.shape[0]
    wcols = d6 // MOD_COL_BLOCKS
    return pl.pallas_call(
        _mod_kernel,
        grid=(depth, MOD_COL_BLOCKS),
        in_specs=[
            pl.BlockSpec((d, bsz), lambda l, j: (0, 0)),
            pl.BlockSpec((1, d, wcols), lambda l, j: (l, 0, j)),
            pl.BlockSpec((1, 1, wcols), lambda l, j: (l, 0, j)),
        ],
        out_specs=pl.BlockSpec((1, bsz, wcols), lambda l, j: (l, 0, j)),
        out_shape=jax.ShapeDtypeStruct((depth, bsz, d6), F32),
        compiler_params=_params("arbitrary", "arbitrary"),
        name="modulation",
    )(c.T, w_ada, b_ada.reshape(depth, 1, d6))


def _fourier_tables(seq):
    n1 = FT_N1
    n2 = seq // n1
    gd = FT_GROUP_DIM
    j = np.arange(gd)
    ang = 2.0 * np.pi * np.outer(j, j) / gd
    eye = np.eye(FT_LANES // gd)
    w1 = np.concatenate([np.kron(eye, np.cos(ang)), -np.kron(eye, np.sin(ang))], axis=1)
    k2 = np.arange(n2)[None, :, None]
    s2 = np.arange(n2)[None, None, :]
    s1 = np.arange(n1)[:, None, None]
    ang_a = 2.0 * np.pi * ((k2 * (n1 * s2 + s1)) % seq) / seq
    ta = np.concatenate([np.cos(ang_a), np.sin(ang_a)], axis=2)
    i1 = np.arange(n1)
    ang_b = 2.0 * np.pi * np.outer(i1, i1) / n1
    tb = np.concatenate([np.cos(ang_b), np.sin(ang_b)], axis=1)
    return (jnp.asarray(w1, F32), jnp.asarray(ta, F32), jnp.asarray(tb, F32))


def _fourier_kernel(x_ref, w1_ref, ta_ref, tb_ref, wf_ref, o_ref, wb_ref, p_ref, *, seq, scale):
    n1 = FT_N1
    n2 = seq // n1
    ln = FT_LANES

    wb_ref[...] = wf_ref[0].astype(BF16)

    w1 = w1_ref[...].astype(BF16)
    for c in range(seq // ROW_TILE):
        p = _dot(x_ref[0, c * ROW_TILE:(c + 1) * ROW_TILE, :].astype(BF16), w1)
        for r in range(ROW_TILE // n1):
            dst = pl.ds((c * (ROW_TILE // n1) + r) * FT_PAD, n1)
            p_ref[0, dst, :] = p[r * n1:(r + 1) * n1, 0:ln]
            p_ref[1, dst, :] = p[r * n1:(r + 1) * n1, ln:2 * ln]

    def stage_a(u, carry):
        res = []
        for d in range(FT_UNROLL_A):
            s1 = u * FT_UNROLL_A + d
            rows = pl.ds(s1, n2, stride=FT_PAD)
            g = jnp.concatenate([p_ref[0, rows, :], p_ref[1, rows, :]], axis=0).astype(BF16)
            t = ta_ref[s1]
            t_im = jnp.concatenate([-t[:, n2:2 * n2], t[:, 0:n2]], axis=1)
            res.append((rows, _dot(t.astype(BF16), g), _dot(t_im.astype(BF16), g)))
        for rows, b_re, b_im in res:
            p_ref[0, rows, :] = b_re
            p_ref[1, rows, :] = b_im
        return carry

    lax.fori_loop(0, n1 // FT_UNROLL_A, stage_a, 0)

    tb = tb_ref[...].astype(BF16)
    pitch = n2 + FT_OUT_PAD
    for k1 in range(n1):
        o_ref[0, k1 * pitch + n2:(k1 + 1) * pitch, :] = jnp.zeros((FT_OUT_PAD, ln), F32)

    def stage_b(u, carry):
        for d in range(FT_UNROLL_B):
            k2 = u * FT_UNROLL_B + d
            rows = pl.ds(pl.multiple_of(k2 * FT_PAD, V7X_SUBLANES), n1)
            blk = jnp.concatenate([p_ref[0, rows, :], p_ref[1, rows, :]], axis=0).astype(BF16)
            o_ref[0, pl.ds(k2, n1, stride=pitch), :] = _dot(tb, blk) * scale
        return carry

    lax.fori_loop(0, n2 // FT_UNROLL_B, stage_b, 0)


def _fourier(l, ftx, tables, w_ff1):
    bsz, seq, d_ft = ftx.shape
    w1, ta, tb = tables
    n1 = FT_N1
    n2 = seq // n1
    nj = d_ft // FT_LANES
    _, d, d_ff = w_ff1.shape
    wrows = d // (bsz * nj)
    blk = pl.BlockSpec((1, seq, FT_LANES), lambda b, j: (b, 0, j))
    return pl.pallas_call(
        functools.partial(_fourier_kernel, seq=seq, scale=1.0 / math.sqrt(seq * FT_GROUP_DIM)),
        grid=(bsz, nj),
        in_specs=[
            blk,
            pl.BlockSpec((FT_LANES, 2 * FT_LANES), lambda b, j: (0, 0)),
            _resident((n1, n2, 2 * n2), lambda b, j: (0, 0, 0)),
            pl.BlockSpec((n1, 2 * n1), lambda b, j: (0, 0)),
            pl.BlockSpec((1, wrows, d_ff), lambda b, j: (l, b * nj + j, 0)),
        ],
        out_specs=[
            pl.BlockSpec((1, n1 * (n2 + FT_OUT_PAD), FT_LANES), lambda b, j: (b, 0, j)),
            pl.BlockSpec((wrows, d_ff), lambda b, j: (b * nj + j, 0)),
        ],
        out_shape=[
            jax.ShapeDtypeStruct((bsz, n1 * (n2 + FT_OUT_PAD), d_ft), F32),
            jax.ShapeDtypeStruct((d, d_ff), BF16),
        ],
        scratch_shapes=[pltpu.VMEM((2, n2 * FT_PAD, FT_LANES), F32)],
        compiler_params=_params("arbitrary", "arbitrary"),
        name="fourier",
    )(ftx, w1, ta, tb, w_ff1)


class _ScanScratch:
    def __init__(self, a, b, h, e, p, c, carry):
        self.a, self.b, self.h = a, b, h
        self.e, self.p, self.c, self.carry = e, p, c, carry


def _scan_scratch_shapes(d_rg):
    nblk = d_rg // V7X_LANES
    pad_rows = SCAN_SEGS * SCAN_SEG_STRIDE
    return [
        pltpu.VMEM((nblk, ROW_TILE, V7X_LANES), F32),
        pltpu.VMEM((nblk, ROW_TILE, V7X_LANES), F32),
        pltpu.VMEM((nblk, pad_rows, V7X_LANES), F32),
        pltpu.VMEM((SCAN_SEGS, d_rg), F32),
        pltpu.VMEM((SCAN_SEGS, d_rg), F32),
        pltpu.VMEM((SCAN_SEGS, d_rg), F32),
        pltpu.VMEM((1, d_rg), F32),
    ]


def _rg_scan_tile(conv_part, first_step, wg_ref, ba, bx, lam, scr, *, reverse):
    tm = ROW_TILE
    d_rg = scr.e.shape[-1]
    half = d_rg // 2
    nblk = d_rg // V7X_LANES

    nl = -lam
    c8h = (-0.5 * RG_C) * (jnp.maximum(nl, 0.0) + jnp.log1p(jnp.exp(-jnp.abs(nl))))

    rh = tm // SCAN_ROW_PARTS
    for part in range(SCAN_ROW_PARTS):
        xr = conv_part(part)
        xrb = xr.astype(BF16)
        yield
        for hh in range(2):
            cols = slice(hh * half, (hh + 1) * half)
            pre = _dot(xrb[:, cols], wg_ref[0, 0, hh])
            t_r = jnp.tanh(pre[:, 0:half] + ba[:, cols])
            t_i = jnp.tanh(pre[:, half:2 * half] + bx[:, cols])
            log_a = c8h[:, cols] * t_r + c8h[:, cols]
            a = jnp.exp(log_a)
            th = jnp.tanh(log_a)
            nth = -0.5 * th
            sq = jnp.where(nth > 0.0, nth * lax.rsqrt(nth), 0.0) * lax.rsqrt(1.0 - th)
            bq = sq * ((t_i + 1.0) * xr[:, cols])
            for kk in range(half // V7X_LANES):
                k = hh * (half // V7X_LANES) + kk
                lanes = slice(kk * V7X_LANES, (kk + 1) * V7X_LANES)
                scr.a[k, part * rh:(part + 1) * rh, :] = a[:, lanes]
                scr.b[k, part * rh:(part + 1) * rh, :] = bq[:, lanes]
            yield

    steps = range(SCAN_SEG_LEN - 1, -1, -1) if reverse else range(SCAN_SEG_LEN)

    def at_step(t):
        return slice(t * SCAN_SEGS, (t + 1) * SCAN_SEGS)

    for k in range(nblk):
        lanes = slice(k * V7X_LANES, (k + 1) * V7X_LANES)
        e = jnp.zeros((SCAN_SEGS, V7X_LANES), F32)
        p = e + 1.0
        for t in steps:
            at = scr.a[k, at_step(t), :]
            e = at * e + scr.b[k, at_step(t), :]
            p = at * p
        scr.e[:, lanes] = e
        scr.p[:, lanes] = p
    yield

    c = jnp.where(first_step, 0.0, scr.carry[...])
    order = range(SCAN_SEGS - 1, -1, -1) if reverse else range(SCAN_SEGS)
    for j in order:
        scr.c[j:j + 1, :] = c
        c = scr.p[j:j + 1, :] * c + scr.e[j:j + 1, :]
    scr.carry[...] = c

    for k in range(nblk):
        h = scr.c[:, k * V7X_LANES:(k + 1) * V7X_LANES]
        for t in steps:
            h = scr.a[k, at_step(t), :] * h + scr.b[k, at_step(t), :]
            scr.h[k, pl.ds(t, SCAN_SEGS, stride=SCAN_SEG_STRIDE), :] = h


def _advance(stages, n):
    for _ in range(n):
        next(stages, None)


def _scan_out_block(scr, k, j):
    return scr.h[k, j * SCAN_SEG_STRIDE:j * SCAN_SEG_STRIDE + SCAN_SEG_LEN, :]


def _mix_in_scan_kernel(x_ref, m_ref, p_ref, win_ref, ws_ref, bs_ref, wg_ref, wf_ref,
                        ygm_ref, gact_ref, xr_ref, ftx_ref, hf_ref, wb_ref, ext_ref, segp_ref, *scratch,
                        vecs, nt, ntot, d_gm, d_rg):
    scr = _ScanScratch(*scratch)
    g = pl.program_id(0)
    tm = ROW_TILE
    halo = V7X_SUBLANES
    o_v, o_g, o_x, o_f = d_gm, 2 * d_gm, 2 * d_gm + d_rg, 2 * d_gm + 2 * d_rg

    wb_ref[...] = wf_ref[0].astype(BF16)

    def step(project, scan_on):
        gs = (g - 1) % nt
        if project:
            x = x_ref[0]
            sh1 = m_ref[0, 0, 0:1, :]
            sc1 = m_ref[0, 0, 1:2, :]
            h = (_rms(x, vecs.get(p_ref, "g_pre_mix") * (1.0 + sc1)) + sh1).astype(BF16)
            rgx = _dot(h, win_ref[0, :, o_x:o_f])

        def conv_part(part):
            nblk = d_rg // V7X_LANES
            seg_rows = SCAN_SEG_LEN + 2 * halo
            if part == 0:
                nxt = (jnp.where(gs == nt - 1, 0.0, rgx[0:halo]) if project
                       else jnp.zeros((halo, d_rg), F32))
                ext_ref[halo + tm:halo + tm + halo, :] = nxt
                for k in range(nblk):
                    for j in range(SCAN_SEGS):
                        segp_ref[k, j * SEG_PITCH:j * SEG_PITCH + seg_rows, :] = (
                            ext_ref[j * SCAN_SEG_LEN:j * SCAN_SEG_LEN + seg_rows,
                                    k * V7X_LANES:(k + 1) * V7X_LANES])
            steps = SCAN_SEG_LEN // SCAN_ROW_PARTS
            cols = []
            for k in range(nblk):
                lanes = slice(k * V7X_LANES, (k + 1) * V7X_LANES)
                taps = [jnp.broadcast_to(vecs.get(p_ref, f"conv_w{i}")[:, lanes], (SCAN_SEGS, V7X_LANES))
                        for i in range(RG_CONV)]
                bias = jnp.broadcast_to(vecs.get(p_ref, "conv_b")[:, lanes], (SCAN_SEGS, V7X_LANES))
                slabs = [segp_ref[k, pl.ds(halo - RG_CONV_LEFT + part * steps + i, SCAN_SEGS, stride=SEG_PITCH), :]
                         for i in range(steps + RG_CONV - 1)]
                out = []
                for t in range(steps):
                    acc = bias
                    for i in range(RG_CONV):
                        acc = acc + slabs[t + i] * taps[i]
                    out.append(acc)
                cols.append(jnp.concatenate(out, axis=0))
            xr = jnp.concatenate(cols, axis=1)
            rh = tm // SCAN_ROW_PARTS
            xr_ref[0, part * rh:(part + 1) * rh, :] = xr
            return xr

        if scan_on:
            scan = _rg_scan_tile(conv_part, gs == 0, wg_ref, vecs.get(p_ref, "b_a0"), vecs.get(p_ref, "b_x0"),
                                 vecs.get(p_ref, "lam0"), scr, reverse=False)
        else:
            scr.carry[...] = jnp.zeros_like(scr.carry)
            scan = iter(())
        if not project:
            for _ in scan:
                pass
        else:
            _advance(scan, 1)
            v = _gelu(_dot(h, win_ref[0, :, o_v:o_g]))
            mu = jnp.mean(v, axis=-1, keepdims=True)
            vc = v - mu
            var = jnp.mean(vc * vc, axis=-1, keepdims=True)
            vn = (vc * lax.rsqrt(var + EPS) * vecs.get(p_ref, "ln_g") + vecs.get(p_ref, "ln_b")).astype(BF16)
            _advance(scan, 1)
            gact_ref[0] = _gelu(_dot(h, win_ref[0, :, o_g:o_x])).astype(BF16)
            _advance(scan, 3 * (SCAN_ROW_PARTS - 1) + 1 - 2)
            if scan_on:
                ext_ref[0:halo, :] = jnp.where(g % nt == 0, 0.0, ext_ref[tm:tm + halo, :])
            else:
                ext_ref[0:halo, :] = jnp.zeros((halo, d_rg), F32)
            ext_ref[halo:halo + tm, :] = rgx
            ftx_ref[0] = _dot(h, win_ref[0, :, o_f:o_f + d_gm]).astype(BF16)
            _advance(scan, 1)
            u = _gelu(_dot(h, win_ref[0, :, 0:o_v]))
            _advance(scan, 1)
            head_dim = d_gm // GM_HEADS
            head = lax.broadcasted_iota(jnp.int32, (GM_CHUNK, d_gm), 1) // head_dim
            for c in range(tm // GM_CHUNK):
                rows = slice(c * GM_CHUNK, (c + 1) * GM_CHUNK)
                r = _dot(ws_ref[0], vn[rows])
                s = r[0:GM_CHUNK]
                for hh in range(1, GM_HEADS):
                    s = jnp.where(head == hh, r[hh * GM_CHUNK:(hh + 1) * GM_CHUNK], s)
                y = u[rows] * (s + bs_ref[0])
                ygm_ref[0, rows, :] = _rms(y, vecs.get(p_ref, "g_mix_out", 0, d_gm)).astype(BF16)
                if c == 1:
                    _advance(scan, 1)
            for _ in scan:
                pass
        if scan_on:
            for k in range(d_rg // V7X_LANES):
                for j in range(SCAN_SEGS):
                    hf_ref[0, j * SCAN_SEG_LEN:(j + 1) * SCAN_SEG_LEN,
                           k * V7X_LANES:(k + 1) * V7X_LANES] = _scan_out_block(scr, k, j)

    pl.when(g == 0)(functools.partial(step, True, False))
    pl.when((g > 0) & (g < ntot))(functools.partial(step, True, True))
    pl.when(g == ntot)(functools.partial(step, False, True))


def _mix_in_scan(l, x, mod4, vecs, packed, w_in, ws, bs, wg, w_ff2, *, d_gm, d_rg):
    bsz, seq, d = x.shape
    d_in = w_in.shape[-1]
    nt = seq // ROW_TILE
    ntot = bsz * nt
    d_ff = w_ff2.shape[1]
    wrows = d_ff // ntot
    wblk = lambda g: jnp.minimum(g, ntot - 1)

    def proj_bt(g):
        gp = jnp.minimum(g, ntot - 1)
        return gp // nt, gp % nt

    def scan_bt(g):
        gs = jnp.maximum(g - 1, 0)
        return gs // nt, gs % nt

    lay = lambda g: (l, 0, 0)
    row = lambda g: (*proj_bt(g), 0)
    return pl.pallas_call(
        functools.partial(_mix_in_scan_kernel, vecs=vecs, nt=nt, ntot=ntot, d_gm=d_gm, d_rg=d_rg),
        grid=(ntot + 1,),
        in_specs=[
            pl.BlockSpec((1, ROW_TILE, d), row),
            pl.BlockSpec((1, 1, 6, d), lambda g: (l, proj_bt(g)[0], 0, 0)),
            vecs.spec(l),
            _resident((1, d, d_in), lay),
            pl.BlockSpec((1, GM_HEADS * GM_CHUNK, GM_CHUNK), lay),
            pl.BlockSpec((1, GM_CHUNK, d_gm), lay),
            _resident((1, 1, 2, d_rg // 2, d_rg), lambda g: (l, 0, 0, 0, 0)),
            pl.BlockSpec((1, wrows, d), lambda g: (l, wblk(g), 0)),
        ],
        out_specs=[
            pl.BlockSpec((1, ROW_TILE, d_gm), row),
            pl.BlockSpec((1, ROW_TILE, d_rg), row),
            pl.BlockSpec((1, ROW_TILE, d_rg), lambda g: (*scan_bt(g), 0)),
            pl.BlockSpec((1, ROW_TILE, d_gm), row),
            pl.BlockSpec((1, ROW_TILE, d_rg), lambda g: (*scan_bt(g), 0)),
            pl.BlockSpec((wrows, d), lambda g: (wblk(g), 0)),
        ],
        out_shape=[
            jax.ShapeDtypeStruct((bsz, seq, d_gm), BF16),
            jax.ShapeDtypeStruct((bsz, seq, d_rg), BF16),
            jax.ShapeDtypeStruct((bsz, seq, d_rg), F32),
            jax.ShapeDtypeStruct((bsz, seq, d_gm), BF16),
            jax.ShapeDtypeStruct((bsz, seq, d_rg), F32),
            jax.ShapeDtypeStruct((d_ff, d), BF16),
        ],
        scratch_shapes=[
            pltpu.VMEM((ROW_TILE + 2 * V7X_SUBLANES, d_rg), F32),
            pltpu.VMEM((d_rg // V7X_LANES, SCAN_SEGS * SEG_PITCH, V7X_LANES), F32),
        ] + _scan_scratch_shapes(d_rg),
        compiler_params=_params("arbitrary"),
        name="mix_in_scan",
    )(x, mod4, packed, w_in, ws, bs, wg, w_ff2)


def _mix_out_ffn_kernel(xr_ref, wg_ref, hf_ref, x_ref, ygm_ref, gact_ref, yft_ref, m_ref, p_ref, ftw_ref,
                        wout_ref, w1_ref, w2_ref,
                        o_ref, *scratch, vecs, nt, ntot, ff_chunk, ft_groups):
    scr = _ScanScratch(*scratch)
    g = pl.program_id(0)
    d_rg = xr_ref.shape[-1]
    d_gm = ygm_ref.shape[-1]
    o_ft = d_gm + d_rg
    d_mix = wout_ref.shape[1]
    gs = jnp.minimum(g, ntot - 1) % nt

    def conv_part(part):
        rh = ROW_TILE // SCAN_ROW_PARTS
        return xr_ref[0, part * rh:(part + 1) * rh, :]

    def scan_stages():
        return _rg_scan_tile(conv_part, gs == 0, wg_ref, vecs.get(p_ref, "b_a1"), vecs.get(p_ref, "b_x1"),
                             vecs.get(p_ref, "lam1"), scr, reverse=True)

    @pl.when(g == 0)
    def _():
        scr.carry[...] = jnp.zeros_like(scr.carry)
        for _ in scan_stages():
            pass

    @pl.when(g > 0)
    def _():
        h_bwd = jnp.concatenate(
            [jnp.concatenate([_scan_out_block(scr, k, j) for j in range(SCAN_SEGS)], axis=0)
             for k in range(d_rg // V7X_LANES)], axis=1)

        gt1 = m_ref[0, 0, 2:3, :]
        yrg = _rms((hf_ref[0] + h_bwd) * gact_ref[0].astype(F32), vecs.get(p_ref, "g_mix_out", d_gm, o_ft))
        ft_pitch = yft_ref.shape[1] // ft_groups
        ft_rows = ROW_TILE // ft_groups
        yft_raw = jnp.concatenate(
            [yft_ref[0, a * ft_pitch:a * ft_pitch + ft_rows, :] for a in range(ft_groups)], axis=0)
        yft = _dot(yft_raw.astype(BF16), ftw_ref[0]) + vecs.get(p_ref, "ft_b")
        yft = _rms(yft, vecs.get(p_ref, "g_mix_out", o_ft, d_mix))
        o = _dot(ygm_ref[0], wout_ref[0, 0:d_gm, :])
        o = o + _dot(yrg.astype(BF16), wout_ref[0, d_gm:o_ft, :])
        o = o + _dot(yft.astype(BF16), wout_ref[0, o_ft:d_mix, :])
        x = x_ref[0] + _rms(o, gt1 * vecs.get(p_ref, "g_post_mix"))

        sh2 = m_ref[0, 0, 3:4, :]
        sc2 = m_ref[0, 0, 4:5, :]
        gt2 = m_ref[0, 0, 5:6, :]
        h = (_rms(x, vecs.get(p_ref, "g_pre_ff") * (1.0 + sc2)) + sh2).astype(BF16)

        scan = scan_stages()
        next(scan)
        d_ff = w1_ref.shape[-1]
        acc = None
        for c in range(d_ff // ff_chunk):
            cols = slice(c * ff_chunk, (c + 1) * ff_chunk)
            a = jnp.maximum(_dot(h, w1_ref[:, cols]), 0.0)
            part = _dot((a * a).astype(BF16), w2_ref[cols, :])
            acc = part if acc is None else acc + part
            _advance(scan, pl.cdiv(SCAN_STAGES - 1, d_ff // ff_chunk))
        for _ in scan:
            pass
        o_ref[0] = x + _rms(acc, gt2 * vecs.get(p_ref, "g_post_ff"))


def _mix_out_ffn(l, xr, wg, h_fwd, x, ygm, gact, yft, mod4, vecs, packed, ftw, w_out, w1, w2):
    bsz, seq, d = x.shape
    d_rg = xr.shape[-1]
    d_gm = ygm.shape[-1]
    d_ft = yft.shape[-1]
    d_mix = w_out.shape[1]
    d_ff = w1.shape[-1]
    nt = seq // ROW_TILE
    ntot = bsz * nt

    def scan_bt(g):
        gs = jnp.minimum(g, ntot - 1)
        return gs // nt, nt - 1 - gs % nt

    def tail_bt(g):
        gf = jnp.maximum(g - 1, 0)
        return gf // nt, nt - 1 - gf % nt

    lay = lambda g: (l, 0, 0)
    row = lambda g: (*tail_bt(g), 0)
    in_specs = [
        pl.BlockSpec((1, ROW_TILE, d_rg), lambda g: (*scan_bt(g), 0)),
        _resident((1, 1, 2, d_rg // 2, d_rg), lambda g: (l, 1, 0, 0, 0)),
        pl.BlockSpec((1, ROW_TILE, d_rg), row),
        pl.BlockSpec((1, ROW_TILE, d), row),
        pl.BlockSpec((1, ROW_TILE, d_gm), row),
        pl.BlockSpec((1, ROW_TILE, d_rg), row),
        pl.BlockSpec((1, yft.shape[1] // (seq // ROW_TILE), d_ft), row),
        pl.BlockSpec((1, 1, 6, d), lambda g: (l, tail_bt(g)[0], 0, 0)),
        vecs.spec(l),
        _resident((1, d_ft, d_ft), lay),
        _resident((1, d_mix, d), lay),
        _resident((d, d_ff), lambda g: (0, 0)),
        _resident((d_ff, d), lambda g: (0, 0)),
    ]
    return pl.pallas_call(
        functools.partial(_mix_out_ffn_kernel, vecs=vecs, nt=nt, ntot=ntot, ff_chunk=d_ff // FF_CHUNKS,
                          ft_groups=ROW_TILE // (seq // FT_N1)),
        grid=(ntot + 1,),
        in_specs=in_specs,
        out_specs=pl.BlockSpec((1, ROW_TILE, d), row),
        out_shape=jax.ShapeDtypeStruct((bsz, seq, d), F32),
        scratch_shapes=_scan_scratch_shapes(d_rg),
        compiler_params=_params("arbitrary"),
        name="mix_out_ffn",
    )(xr, wg, h_fwd, x, ygm, gact, yft, mod4, packed, ftw, w_out, w1, w2)


def _block_diag(w, per):
    *lead, n, hd, _ = w.shape
    w = w.reshape(*lead, n // per, per, hd, hd)
    keep = [(0, 0)] * (len(lead) + 2)
    rows = [jnp.pad(w[..., a, :, :], keep + [(a * hd, (per - 1 - a) * hd)]) for a in range(per)]
    return jnp.concatenate(rows, axis=-2)


def kernel(x, c, w_ada, b_ada, g_pre_mix, g_post_mix, w_in, gm_ln_g, gm_ln_b, gm_w_s, gm_b_s,
           rg_conv_w, rg_conv_b, rg_w_a, rg_b_a, rg_w_x, rg_b_x, rg_lam, ft_w, ft_b,
           g_mix_out, w_out, g_pre_ff, g_post_ff, w_ff1, w_ff2):
    bsz, seq, d = x.shape
    depth = w_in.shape[0]
    d_gm = gm_ln_g.shape[-1]
    d_rg = rg_conv_b.shape[-1]
    d_ft = ft_w.shape[1] * ft_w.shape[2]
    assert seq % ROW_TILE == 0 and ROW_TILE % GM_CHUNK == 0 and seq % FT_N1 == 0
    assert d_ft % FT_LANES == 0 and ft_w.shape[2] == FT_GROUP_DIM
    assert rg_conv_w.shape[1] == RG_CONV and rg_w_a.shape[2] == RG_HEADS

    w_in_b = w_in.astype(BF16)
    w_out_b = w_out.astype(BF16)
    ws = gm_w_s.reshape(depth, GM_HEADS * GM_CHUNK, GM_CHUNK).astype(BF16)
    bs = jnp.repeat(jnp.swapaxes(gm_b_s, 1, 2), d_gm // GM_HEADS, axis=2)
    per = RG_HEADS // 2
    wg = (0.5 * jnp.concatenate([_block_diag(rg_w_a, per), _block_diag(rg_w_x, per)], axis=-1)).astype(BF16)
    ftw = _block_diag(ft_w, FT_GROUPS).reshape(depth, d_ft, d_ft).astype(BF16)
    tables = _fourier_tables(seq)

    flat = lambda a: a.reshape(depth, -1)
    named = [("g_pre_mix", g_pre_mix), ("g_post_mix", g_post_mix), ("g_pre_ff", g_pre_ff),
             ("g_post_ff", g_post_ff), ("g_mix_out", g_mix_out), ("ln_g", gm_ln_g), ("ln_b", gm_ln_b),
             ("conv_b", rg_conv_b), ("ft_b", flat(ft_b))]
    for dirn in range(2):
        named += [(f"b_a{dirn}", 0.5 * flat(rg_b_a[:, dirn])), (f"b_x{dirn}", 0.5 * flat(rg_b_x[:, dirn])),
                  (f"lam{dirn}", flat(rg_lam[:, dirn]))]
    named += [(f"conv_w{i}", rg_conv_w[:, i]) for i in range(RG_CONV)]
    vecs = _LayerVecs(named)
    packed = vecs.pack(named)

    mod4 = _modulation(c, w_ada, b_ada).reshape(depth, bsz, 6, d)

    for l in range(depth):
        ygm, gact, xr, ftx, h_fwd, w2_b = _mix_in_scan(l, x, mod4, vecs, packed, w_in_b, ws, bs, wg, w_ff2,
                                                       d_gm=d_gm, d_rg=d_rg)
        yft, w1_b = _fourier(l, ftx, tables, w_ff1)
        x = _mix_out_ffn(l, xr, wg, h_fwd, x, ygm, gact, yft, mod4, vecs, packed, ftw, w_out_b, w1_b, w2_b)
    return x
```

```python
import functools
import math

import numpy as np
import jax
import jax.numpy as jnp
from jax import lax
from jax.experimental import pallas as pl
from jax.experimental.pallas import tpu as pltpu

F32 = jnp.float32
BF16 = jnp.bfloat16

EPS = 1e-6
RG_C = 8.0
RG_CONV = 4
RG_CONV_LEFT = 2
GM_HEADS = 4
GM_CHUNK = 128
RG_HEADS = 8
FT_GROUPS = 4
FT_GROUP_DIM = 64

V7X_SUBLANES = 8
V7X_LANES = 128
V7X_VMEM_LIMIT_BYTES = 56 * 1024 * 1024

MOD_COL_BLOCKS = 2
ROW_TILE = 512
FF_CHUNKS = 8
SCAN_ROW_PARTS = 2
SCAN_STAGES = 3 * SCAN_ROW_PARTS + 2
SCAN_SEGS = V7X_SUBLANES
SCAN_SEG_LEN = ROW_TILE // SCAN_SEGS
SCAN_SEG_STRIDE = SCAN_SEG_LEN + V7X_SUBLANES
SEG_PITCH = SCAN_SEG_LEN + 3 * V7X_SUBLANES

FT_N1 = 64
FT_PAD = FT_N1 + V7X_SUBLANES
FT_LANES = 128
FT_OUT_PAD = V7X_SUBLANES
FT_UNROLL_A = 8
FT_UNROLL_B = 16


def _gelu(x):
    c = math.sqrt(2.0 / math.pi)
    t = jnp.tanh(x * ((c * 0.044715) * (x * x) + c))
    return (0.5 * x) * (t + 1.0)


def _sigmoid(x):
    return 0.5 * jnp.tanh(0.5 * x) + 0.5


def _rms(x, g):
    return x * lax.rsqrt(jnp.mean(x * x, axis=-1, keepdims=True) + EPS) * g


def _dot(a, b):
    return jnp.dot(a, b, preferred_element_type=F32)


def _params(*sem):
    return pltpu.CompilerParams(dimension_semantics=sem, vmem_limit_bytes=V7X_VMEM_LIMIT_BYTES)


def _resident(shape, index_map):
    return pl.BlockSpec(shape, index_map, pipeline_mode=pl.Buffered(1))


class _LayerVecs:
    def __init__(self, named):
        self.range = {}
        self.total = 0
        for name, v in named:
            n = v.shape[-1]
            assert n % V7X_LANES == 0
            self.range[name] = (self.total, n)
            self.total += n

    def pack(self, named):
        return jnp.concatenate([v for _, v in named], axis=-1)[:, None, :]

    def get(self, ref, name, lo=0, hi=None):
        start, n = self.range[name]
        return ref[0, :, start + lo:start + (n if hi is None else hi)]

    def spec(self, l):
        return pl.BlockSpec((1, 1, self.total), lambda *g: (l, 0, 0))


def _mod_kernel(ct_ref, w_ref, b_ref, wf_ref, o_ref, wb_ref):
    wb_ref[0] = wf_ref[0].astype(BF16)

    ct = ct_ref[...]
    cond = ct * _sigmoid(ct)
    d = ct.shape[0]
    for j in range(w_ref.shape[-1] // d):
        cols = slice(j * d, (j + 1) * d)
        w = w_ref[0, :, cols]
        for b in range(ct.shape[1]):
            o_ref[0, b:b + 1, cols] = jnp.sum(w * cond[:, b:b + 1], axis=0, keepdims=True) + b_ref[0, :, cols]


def _modulation(c, w_ada, b_ada, w_in):
    depth, d, d6 = w_ada.shape
    bsz = c.shape[0]
    wcols = d6 // MOD_COL_BLOCKS
    icols = w_in.shape[-1] // MOD_COL_BLOCKS
    return pl.pallas_call(
        _mod_kernel,
        grid=(depth, MOD_COL_BLOCKS),
        in_specs=[
            pl.BlockSpec((d, bsz), lambda l, j: (0, 0)),
            pl.BlockSpec((1, d, wcols), lambda l, j: (l, 0, j)),
            pl.BlockSpec((1, 1, wcols), lambda l, j: (l, 0, j)),
            pl.BlockSpec((1, d, icols), lambda l, j: (l, 0, j)),
        ],
        out_specs=[
            pl.BlockSpec((1, bsz, wcols), lambda l, j: (l, 0, j)),
            pl.BlockSpec((1, d, icols), lambda l, j: (l, 0, j)),
        ],
        out_shape=[
            jax.ShapeDtypeStruct((depth, bsz, d6), F32),
            jax.ShapeDtypeStruct(w_in.shape, BF16),
        ],
        compiler_params=_params("arbitrary", "arbitrary"),
        name="modulation",
    )(c.T, w_ada, b_ada.reshape(depth, 1, d6), w_in)


def _fourier_tables(seq):
    n1 = FT_N1
    n2 = seq // n1
    gd = FT_GROUP_DIM
    j = np.arange(gd)
    ang = 2.0 * np.pi * np.outer(j, j) / gd
    eye = np.eye(FT_LANES // gd)
    w1 = np.concatenate([np.kron(eye, np.cos(ang)), -np.kron(eye, np.sin(ang))], axis=1)
    k2 = np.arange(n2)[None, :, None]
    s2 = np.arange(n2)[None, None, :]
    s1 = np.arange(n1)[:, None, None]
    ang_a = 2.0 * np.pi * ((k2 * (n1 * s2 + s1)) % seq) / seq
    ta = np.concatenate([np.cos(ang_a), np.sin(ang_a)], axis=2)
    i1 = np.arange(n1)
    ang_b = 2.0 * np.pi * np.outer(i1, i1) / n1
    tb = np.concatenate([np.cos(ang_b), np.sin(ang_b)], axis=1)
    return (jnp.asarray(w1, F32), jnp.asarray(ta, F32), jnp.asarray(tb, F32))


def _fourier_kernel(x_ref, w1_ref, ta_ref, tb_ref, wf_ref, o_ref, wb_ref, p_ref, *, seq, scale):
    n1 = FT_N1
    n2 = seq // n1
    ln = FT_LANES

    wb_ref[...] = wf_ref[0].astype(BF16)

    w1 = w1_ref[...].astype(BF16)
    for c in range(seq // ROW_TILE):
        p = _dot(x_ref[0, c * ROW_TILE:(c + 1) * ROW_TILE, :].astype(BF16), w1)
        for r in range(ROW_TILE // n1):
            dst = pl.ds((c * (ROW_TILE // n1) + r) * FT_PAD, n1)
            p_ref[0, dst, :] = p[r * n1:(r + 1) * n1, 0:ln]
            p_ref[1, dst, :] = p[r * n1:(r + 1) * n1, ln:2 * ln]

    def stage_a(u, carry):
        res = []
        for d in range(FT_UNROLL_A):
            s1 = u * FT_UNROLL_A + d
            rows = pl.ds(s1, n2, stride=FT_PAD)
            g = jnp.concatenate([p_ref[0, rows, :], p_ref[1, rows, :]], axis=0).astype(BF16)
            t = ta_ref[s1]
            t_im = jnp.concatenate([-t[:, n2:2 * n2], t[:, 0:n2]], axis=1)
            res.append((rows, _dot(t.astype(BF16), g), _dot(t_im.astype(BF16), g)))
        for rows, b_re, b_im in res:
            p_ref[0, rows, :] = b_re
            p_ref[1, rows, :] = b_im
        return carry

    lax.fori_loop(0, n1 // FT_UNROLL_A, stage_a, 0)

    tb = tb_ref[...].astype(BF16)
    pitch = n2 + FT_OUT_PAD
    for k1 in range(n1):
        o_ref[0, k1 * pitch + n2:(k1 + 1) * pitch, :] = jnp.zeros((FT_OUT_PAD, ln), F32)

    def stage_b(u, carry):
        for d in range(FT_UNROLL_B):
            k2 = u * FT_UNROLL_B + d
            rows = pl.ds(pl.multiple_of(k2 * FT_PAD, V7X_SUBLANES), n1)
            blk = jnp.concatenate([p_ref[0, rows, :], p_ref[1, rows, :]], axis=0).astype(BF16)
            o_ref[0, pl.ds(k2, n1, stride=pitch), :] = _dot(tb, blk) * scale
        return carry

    lax.fori_loop(0, n2 // FT_UNROLL_B, stage_b, 0)


def _fourier(l, ftx, tables, w_ff1):
    bsz, seq, d_ft = ftx.shape
    w1, ta, tb = tables
    n1 = FT_N1
    n2 = seq // n1
    nj = d_ft // FT_LANES
    _, d, d_ff = w_ff1.shape
    wrows = d // (bsz * nj)
    blk = pl.BlockSpec((1, seq, FT_LANES), lambda b, j: (b, 0, j))
    return pl.pallas_call(
        functools.partial(_fourier_kernel, seq=seq, scale=1.0 / math.sqrt(seq * FT_GROUP_DIM)),
        grid=(bsz, nj),
        in_specs=[
            blk,
            pl.BlockSpec((FT_LANES, 2 * FT_LANES), lambda b, j: (0, 0)),
            _resident((n1, n2, 2 * n2), lambda b, j: (0, 0, 0)),
            pl.BlockSpec((n1, 2 * n1), lambda b, j: (0, 0)),
            pl.BlockSpec((1, wrows, d_ff), lambda b, j: (l, b * nj + j, 0)),
        ],
        out_specs=[
            pl.BlockSpec((1, n1 * (n2 + FT_OUT_PAD), FT_LANES), lambda b, j: (b, 0, j)),
            pl.BlockSpec((wrows, d_ff), lambda b, j: (b * nj + j, 0)),
        ],
        out_shape=[
            jax.ShapeDtypeStruct((bsz, n1 * (n2 + FT_OUT_PAD), d_ft), F32),
            jax.ShapeDtypeStruct((d, d_ff), BF16),
        ],
        scratch_shapes=[pltpu.VMEM((2, n2 * FT_PAD, FT_LANES), F32)],
        compiler_params=_params("arbitrary", "arbitrary"),
        name="fourier",
    )(ftx, w1, ta, tb, w_ff1)


class _ScanScratch:
    def __init__(self, a, b, h, e, p, c, carry):
        self.a, self.b, self.h = a, b, h
        self.e, self.p, self.c, self.carry = e, p, c, carry


def _scan_scratch_shapes(d_rg):
    nblk = d_rg // V7X_LANES
    pad_rows = SCAN_SEGS * SCAN_SEG_STRIDE
    return [
        pltpu.VMEM((nblk, ROW_TILE, V7X_LANES), F32),
        pltpu.VMEM((nblk, ROW_TILE, V7X_LANES), F32),
        pltpu.VMEM((nblk, pad_rows, V7X_LANES), F32),
        pltpu.VMEM((SCAN_SEGS, d_rg), F32),
        pltpu.VMEM((SCAN_SEGS, d_rg), F32),
        pltpu.VMEM((SCAN_SEGS, d_rg), F32),
        pltpu.VMEM((1, d_rg), F32),
    ]


def _rg_scan_tile(conv_part, first_step, wg_ref, ba, bx, lam, scr, *, reverse):
    tm = ROW_TILE
    d_rg = scr.e.shape[-1]
    half = d_rg // 2
    nblk = d_rg // V7X_LANES

    nl = -lam
    c8h = (-0.5 * RG_C) * (jnp.maximum(nl, 0.0) + jnp.log1p(jnp.exp(-jnp.abs(nl))))

    rh = tm // SCAN_ROW_PARTS
    for part in range(SCAN_ROW_PARTS):
        xr = conv_part(part)
        xrb = xr.astype(BF16)
        yield
        for hh in range(2):
            cols = slice(hh * half, (hh + 1) * half)
            pre = _dot(xrb[:, cols], wg_ref[0, 0, hh])
            t_r = jnp.tanh(pre[:, 0:half] + ba[:, cols])
            t_i = jnp.tanh(pre[:, half:2 * half] + bx[:, cols])
            log_a = c8h[:, cols] * t_r + c8h[:, cols]
            a = jnp.exp(log_a)
            th = jnp.tanh(log_a)
            nth = -0.5 * th
            sq = jnp.where(nth > 0.0, nth * lax.rsqrt(nth), 0.0) * lax.rsqrt(1.0 - th)
            bq = sq * ((t_i + 1.0) * xr[:, cols])
            for kk in range(half // V7X_LANES):
                k = hh * (half // V7X_LANES) + kk
                lanes = slice(kk * V7X_LANES, (kk + 1) * V7X_LANES)
                scr.a[k, part * rh:(part + 1) * rh, :] = a[:, lanes]
                scr.b[k, part * rh:(part + 1) * rh, :] = bq[:, lanes]
            yield

    steps = range(SCAN_SEG_LEN - 1, -1, -1) if reverse else range(SCAN_SEG_LEN)

    def at_step(t):
        return slice(t * SCAN_SEGS, (t + 1) * SCAN_SEGS)

    for k in range(nblk):
        lanes = slice(k * V7X_LANES, (k + 1) * V7X_LANES)
        e = jnp.zeros((SCAN_SEGS, V7X_LANES), F32)
        p = e + 1.0
        for t in steps:
            at = scr.a[k, at_step(t), :]
            e = at * e + scr.b[k, at_step(t), :]
            p = at * p
        scr.e[:, lanes] = e
        scr.p[:, lanes] = p
    yield

    c = jnp.where(first_step, 0.0, scr.carry[...])
    order = range(SCAN_SEGS - 1, -1, -1) if reverse else range(SCAN_SEGS)
    for j in order:
        scr.c[j:j + 1, :] = c
        c = scr.p[j:j + 1, :] * c + scr.e[j:j + 1, :]
    scr.carry[...] = c

    for k in range(nblk):
        h = scr.c[:, k * V7X_LANES:(k + 1) * V7X_LANES]
        for t in steps:
            h = scr.a[k, at_step(t), :] * h + scr.b[k, at_step(t), :]
            scr.h[k, pl.ds(t, SCAN_SEGS, stride=SCAN_SEG_STRIDE), :] = h


def _advance(stages, n):
    for _ in range(n):
        next(stages, None)


def _scan_out_block(scr, k, j):
    return scr.h[k, j * SCAN_SEG_STRIDE:j * SCAN_SEG_STRIDE + SCAN_SEG_LEN, :]


def _mix_in_scan_kernel(x_ref, m_ref, p_ref, win_ref, ws_ref, bs_ref, wg_ref, wf_ref, wof_ref,
                        ygm_ref, gact_ref, xr_ref, ftx_ref, hf_ref, wb_ref, wob_ref, ext_ref, segp_ref, *scratch,
                        vecs, nt, ntot, d_gm, d_rg):
    scr = _ScanScratch(*scratch)
    g = pl.program_id(0)
    tm = ROW_TILE
    halo = V7X_SUBLANES
    o_v, o_g, o_x, o_f = d_gm, 2 * d_gm, 2 * d_gm + d_rg, 2 * d_gm + 2 * d_rg

    wb_ref[...] = wf_ref[0].astype(BF16)
    wob_ref[...] = wof_ref[0].astype(BF16)

    def step(project, scan_on):
        gs = (g - 1) % nt
        if project:
            x = x_ref[0]
            sh1 = m_ref[0, 0, 0:1, :]
            sc1 = m_ref[0, 0, 1:2, :]
            h = (_rms(x, vecs.get(p_ref, "g_pre_mix") * (1.0 + sc1)) + sh1).astype(BF16)
            rgx = _dot(h, win_ref[0, :, o_x:o_f])

        def conv_part(part):
            nblk = d_rg // V7X_LANES
            seg_rows = SCAN_SEG_LEN + 2 * halo
            if part == 0:
                nxt = (jnp.where(gs == nt - 1, 0.0, rgx[0:halo]) if project
                       else jnp.zeros((halo, d_rg), F32))
                ext_ref[halo + tm:halo + tm + halo, :] = nxt
                for k in range(nblk):
                    for j in range(SCAN_SEGS):
                        segp_ref[k, j * SEG_PITCH:j * SEG_PITCH + seg_rows, :] = (
                            ext_ref[j * SCAN_SEG_LEN:j * SCAN_SEG_LEN + seg_rows,
                                    k * V7X_LANES:(k + 1) * V7X_LANES])
            steps = SCAN_SEG_LEN // SCAN_ROW_PARTS
            cols = []
            for k in range(nblk):
                lanes = slice(k * V7X_LANES, (k + 1) * V7X_LANES)
                taps = [jnp.broadcast_to(vecs.get(p_ref, f"conv_w{i}")[:, lanes], (SCAN_SEGS, V7X_LANES))
                        for i in range(RG_CONV)]
                bias = jnp.broadcast_to(vecs.get(p_ref, "conv_b")[:, lanes], (SCAN_SEGS, V7X_LANES))
                slabs = [segp_ref[k, pl.ds(halo - RG_CONV_LEFT + part * steps + i, SCAN_SEGS, stride=SEG_PITCH), :]
                         for i in range(steps + RG_CONV - 1)]
                out = []
                for t in range(steps):
                    acc = bias
                    for i in range(RG_CONV):
                        acc = acc + slabs[t + i] * taps[i]
                    out.append(acc)
                cols.append(jnp.concatenate(out, axis=0))
            xr = jnp.concatenate(cols, axis=1)
            rh = tm // SCAN_ROW_PARTS
            xr_ref[0, part * rh:(part + 1) * rh, :] = xr
            return xr

        if scan_on:
            scan = _rg_scan_tile(conv_part, gs == 0, wg_ref, vecs.get(p_ref, "b_a0"), vecs.get(p_ref, "b_x0"),
                                 vecs.get(p_ref, "lam0"), scr, reverse=False)
        else:
            scr.carry[...] = jnp.zeros_like(scr.carry)
            scan = iter(())
        if not project:
            for _ in scan:
                pass
        else:
            _advance(scan, 1)
            v = _gelu(_dot(h, win_ref[0, :, o_v:o_g]))
            mu = jnp.mean(v, axis=-1, keepdims=True)
            vc = v - mu
            var = jnp.mean(vc * vc, axis=-1, keepdims=True)
            vn = (vc * lax.rsqrt(var + EPS) * vecs.get(p_ref, "ln_g") + vecs.get(p_ref, "ln_b")).astype(BF16)
            _advance(scan, 1)
            gact_ref[0] = _gelu(_dot(h, win_ref[0, :, o_g:o_x])).astype(BF16)
            _advance(scan, 3 * (SCAN_ROW_PARTS - 1) + 1 - 2)
            if scan_on:
                ext_ref[0:halo, :] = jnp.where(g % nt == 0, 0.0, ext_ref[tm:tm + halo, :])
            else:
                ext_ref[0:halo, :] = jnp.zeros((halo, d_rg), F32)
            ext_ref[halo:halo + tm, :] = rgx
            ftx_ref[0] = _dot(h, win_ref[0, :, o_f:o_f + d_gm]).astype(BF16)
            _advance(scan, 1)
            u = _gelu(_dot(h, win_ref[0, :, 0:o_v]))
            _advance(scan, 1)
            head_dim = d_gm // GM_HEADS
            head = lax.broadcasted_iota(jnp.int32, (GM_CHUNK, d_gm), 1) // head_dim
            for c in range(tm // GM_CHUNK):
                rows = slice(c * GM_CHUNK, (c + 1) * GM_CHUNK)
                r = _dot(ws_ref[0], vn[rows])
                s = r[0:GM_CHUNK]
                for hh in range(1, GM_HEADS):
                    s = jnp.where(head == hh, r[hh * GM_CHUNK:(hh + 1) * GM_CHUNK], s)
                y = u[rows] * (s + bs_ref[0])
                ygm_ref[0, rows, :] = _rms(y, vecs.get(p_ref, "g_mix_out", 0, d_gm)).astype(BF16)
                if c == 1:
                    _advance(scan, 1)
            for _ in scan:
                pass
        if scan_on:
            for k in range(d_rg // V7X_LANES):
                for j in range(SCAN_SEGS):
                    hf_ref[0, j * SCAN_SEG_LEN:(j + 1) * SCAN_SEG_LEN,
                           k * V7X_LANES:(k + 1) * V7X_LANES] = _scan_out_block(scr, k, j)

    pl.when(g == 0)(functools.partial(step, True, False))
    pl.when((g > 0) & (g < ntot))(functools.partial(step, True, True))
    pl.when(g == ntot)(functools.partial(step, False, True))


def _mix_in_scan(l, x, mod4, vecs, packed, w_in, ws, bs, wg, w_ff2, w_out, *, d_gm, d_rg):
    bsz, seq, d = x.shape
    d_in = w_in.shape[-1]
    nt = seq // ROW_TILE
    ntot = bsz * nt
    d_ff = w_ff2.shape[1]
    wrows = d_ff // ntot
    orows = w_out.shape[1] // ntot
    wblk = lambda g: jnp.minimum(g, ntot - 1)

    def proj_bt(g):
        gp = jnp.minimum(g, ntot - 1)
        return gp // nt, gp % nt

    def scan_bt(g):
        gs = jnp.maximum(g - 1, 0)
        return gs // nt, gs % nt

    lay = lambda g: (l, 0, 0)
    row = lambda g: (*proj_bt(g), 0)
    return pl.pallas_call(
        functools.partial(_mix_in_scan_kernel, vecs=vecs, nt=nt, ntot=ntot, d_gm=d_gm, d_rg=d_rg),
        grid=(ntot + 1,),
        in_specs=[
            pl.BlockSpec((1, ROW_TILE, d), row),
            pl.BlockSpec((1, 1, 6, d), lambda g: (l, proj_bt(g)[0], 0, 0)),
            vecs.spec(l),
            _resident((1, d, d_in), lay),
            pl.BlockSpec((1, GM_HEADS * GM_CHUNK, GM_CHUNK), lay),
            pl.BlockSpec((1, GM_CHUNK, d_gm), lay),
            _resident((1, 1, 2, d_rg // 2, d_rg), lambda g: (l, 0, 0, 0, 0)),
            pl.BlockSpec((1, wrows, d), lambda g: (l, wblk(g), 0)),
            pl.BlockSpec((1, orows, d), lambda g: (l, wblk(g), 0)),
        ],
        out_specs=[
            pl.BlockSpec((1, ROW_TILE, d_gm), row),
            pl.BlockSpec((1, ROW_TILE, d_rg), row),
            pl.BlockSpec((1, ROW_TILE, d_rg), lambda g: (*scan_bt(g), 0)),
            pl.BlockSpec((1, ROW_TILE, d_gm), row),
            pl.BlockSpec((1, ROW_TILE, d_rg), lambda g: (*scan_bt(g), 0)),
            pl.BlockSpec((wrows, d), lambda g: (wblk(g), 0)),
            pl.BlockSpec((orows, d), lambda g: (wblk(g), 0)),
        ],
        out_shape=[
            jax.ShapeDtypeStruct((bsz, seq, d_gm), BF16),
            jax.ShapeDtypeStruct((bsz, seq, d_rg), BF16),
            jax.ShapeDtypeStruct((bsz, seq, d_rg), F32),
            jax.ShapeDtypeStruct((bsz, seq, d_gm), BF16),
            jax.ShapeDtypeStruct((bsz, seq, d_rg), F32),
            jax.ShapeDtypeStruct((d_ff, d), BF16),
            jax.ShapeDtypeStruct(w_out.shape[1:], BF16),
        ],
        scratch_shapes=[
            pltpu.VMEM((ROW_TILE + 2 * V7X_SUBLANES, d_rg), F32),
            pltpu.VMEM((d_rg // V7X_LANES, SCAN_SEGS * SEG_PITCH, V7X_LANES), F32),
        ] + _scan_scratch_shapes(d_rg),
        compiler_params=_params("arbitrary"),
        name="mix_in_scan",
    )(x, mod4, packed, w_in, ws, bs, wg, w_ff2, w_out)


def _mix_out_ffn_kernel(xr_ref, wg_ref, hf_ref, x_ref, ygm_ref, gact_ref, yft_ref, m_ref, p_ref, ftw_ref,
                        wout_ref, w1_ref, w2_ref,
                        o_ref, *scratch, vecs, nt, ntot, ff_chunk, ft_groups):
    scr = _ScanScratch(*scratch)
    g = pl.program_id(0)
    d_rg = xr_ref.shape[-1]
    d_gm = ygm_ref.shape[-1]
    o_ft = d_gm + d_rg
    d_mix = wout_ref.shape[0]
    gs = jnp.minimum(g, ntot - 1) % nt

    def conv_part(part):
        rh = ROW_TILE // SCAN_ROW_PARTS
        return xr_ref[0, part * rh:(part + 1) * rh, :]

    def scan_stages():
        return _rg_scan_tile(conv_part, gs == 0, wg_ref, vecs.get(p_ref, "b_a1"), vecs.get(p_ref, "b_x1"),
                             vecs.get(p_ref, "lam1"), scr, reverse=True)

    @pl.when(g == 0)
    def _():
        scr.carry[...] = jnp.zeros_like(scr.carry)
        for _ in scan_stages():
            pass

    @pl.when(g > 0)
    def _():
        h_bwd = jnp.concatenate(
            [jnp.concatenate([_scan_out_block(scr, k, j) for j in range(SCAN_SEGS)], axis=0)
             for k in range(d_rg // V7X_LANES)], axis=1)

        gt1 = m_ref[0, 0, 2:3, :]
        yrg = _rms((hf_ref[0] + h_bwd) * gact_ref[0].astype(F32), vecs.get(p_ref, "g_mix_out", d_gm, o_ft))
        ft_pitch = yft_ref.shape[1] // ft_groups
        ft_rows = ROW_TILE // ft_groups
        yft_raw = jnp.concatenate(
            [yft_ref[0, a * ft_pitch:a * ft_pitch + ft_rows, :] for a in range(ft_groups)], axis=0)
        yft = _dot(yft_raw.astype(BF16), ftw_ref[0]) + vecs.get(p_ref, "ft_b")
        yft = _rms(yft, vecs.get(p_ref, "g_mix_out", o_ft, d_mix))
        o = _dot(ygm_ref[0], wout_ref[0:d_gm, :])
        o = o + _dot(yrg.astype(BF16), wout_ref[d_gm:o_ft, :])
        o = o + _dot(yft.astype(BF16), wout_ref[o_ft:d_mix, :])
        x = x_ref[0] + _rms(o, gt1 * vecs.get(p_ref, "g_post_mix"))

        sh2 = m_ref[0, 0, 3:4, :]
        sc2 = m_ref[0, 0, 4:5, :]
        gt2 = m_ref[0, 0, 5:6, :]
        h = (_rms(x, vecs.get(p_ref, "g_pre_ff") * (1.0 + sc2)) + sh2).astype(BF16)

        scan = scan_stages()
        next(scan)
        d_ff = w1_ref.shape[-1]
        acc = None
        for c in range(d_ff // ff_chunk):
            cols = slice(c * ff_chunk, (c + 1) * ff_chunk)
            a = jnp.maximum(_dot(h, w1_ref[:, cols]), 0.0)
            part = _dot((a * a).astype(BF16), w2_ref[cols, :])
            acc = part if acc is None else acc + part
            _advance(scan, pl.cdiv(SCAN_STAGES - 1, d_ff // ff_chunk))
        for _ in scan:
            pass
        o_ref[0] = x + _rms(acc, gt2 * vecs.get(p_ref, "g_post_ff"))


def _mix_out_ffn(l, xr, wg, h_fwd, x, ygm, gact, yft, mod4, vecs, packed, ftw, w_out, w1, w2):
    bsz, seq, d = x.shape
    d_rg = xr.shape[-1]
    d_gm = ygm.shape[-1]
    d_ft = yft.shape[-1]
    d_mix = w_out.shape[0]
    d_ff = w1.shape[-1]
    nt = seq // ROW_TILE
    ntot = bsz * nt

    def scan_bt(g):
        gs = jnp.minimum(g, ntot - 1)
        return gs // nt, nt - 1 - gs % nt

    def tail_bt(g):
        gf = jnp.maximum(g - 1, 0)
        return gf // nt, nt - 1 - gf % nt

    lay = lambda g: (l, 0, 0)
    row = lambda g: (*tail_bt(g), 0)
    in_specs = [
        pl.BlockSpec((1, ROW_TILE, d_rg), lambda g: (*scan_bt(g), 0)),
        _resident((1, 1, 2, d_rg // 2, d_rg), lambda g: (l, 1, 0, 0, 0)),
        pl.BlockSpec((1, ROW_TILE, d_rg), row),
        pl.BlockSpec((1, ROW_TILE, d), row),
        pl.BlockSpec((1, ROW_TILE, d_gm), row),
        pl.BlockSpec((1, ROW_TILE, d_rg), row),
        pl.BlockSpec((1, yft.shape[1] // (seq // ROW_TILE), d_ft), row),
        pl.BlockSpec((1, 1, 6, d), lambda g: (l, tail_bt(g)[0], 0, 0)),
        vecs.spec(l),
        _resident((1, d_ft, d_ft), lay),
        _resident((d_mix, d), lambda g: (0, 0)),
        _resident((d, d_ff), lambda g: (0, 0)),
        _resident((d_ff, d), lambda g: (0, 0)),
    ]
    return pl.pallas_call(
        functools.partial(_mix_out_ffn_kernel, vecs=vecs, nt=nt, ntot=ntot, ff_chunk=d_ff // FF_CHUNKS,
                          ft_groups=ROW_TILE // (seq // FT_N1)),
        grid=(ntot + 1,),
        in_specs=in_specs,
        out_specs=pl.BlockSpec((1, ROW_TILE, d), row),
        out_shape=jax.ShapeDtypeStruct((bsz, seq, d), F32),
        scratch_shapes=_scan_scratch_shapes(d_rg),
        compiler_params=_params("arbitrary"),
        name="mix_out_ffn",
    )(xr, wg, h_fwd, x, ygm, gact, yft, mod4, packed, ftw, w_out, w1, w2)


def _block_diag(w, per):
    *lead, n, hd, _ = w.shape
    w = w.reshape(*lead, n // per, per, hd, hd)
    keep = [(0, 0)] * (len(lead) + 2)
    rows = [jnp.pad(w[..., a, :, :], keep + [(a * hd, (per - 1 - a) * hd)]) for a in range(per)]
    return jnp.concatenate(rows, axis=-2)


def kernel(x, c, w_ada, b_ada, g_pre_mix, g_post_mix, w_in, gm_ln_g, gm_ln_b, gm_w_s, gm_b_s,
           rg_conv_w, rg_conv_b, rg_w_a, rg_b_a, rg_w_x, rg_b_x, rg_lam, ft_w, ft_b,
           g_mix_out, w_out, g_pre_ff, g_post_ff, w_ff1, w_ff2):
    bsz, seq, d = x.shape
    depth = w_in.shape[0]
    d_gm = gm_ln_g.shape[-1]
    d_rg = rg_conv_b.shape[-1]
    d_ft = ft_w.shape[1] * ft_w.shape[2]
    assert seq % ROW_TILE == 0 and ROW_TILE % GM_CHUNK == 0 and seq % FT_N1 == 0
    assert d_ft % FT_LANES == 0 and ft_w.shape[2] == FT_GROUP_DIM
    assert rg_conv_w.shape[1] == RG_CONV and rg_w_a.shape[2] == RG_HEADS

    ws = gm_w_s.reshape(depth, GM_HEADS * GM_CHUNK, GM_CHUNK).astype(BF16)
    bs = jnp.repeat(jnp.swapaxes(gm_b_s, 1, 2), d_gm // GM_HEADS, axis=2)
    per = RG_HEADS // 2
    wg = (0.5 * jnp.concatenate([_block_diag(rg_w_a, per), _block_diag(rg_w_x, per)], axis=-1)).astype(BF16)
    ftw = _block_diag(ft_w, FT_GROUPS).reshape(depth, d_ft, d_ft).astype(BF16)
    tables = _fourier_tables(seq)

    flat = lambda a: a.reshape(depth, -1)
    named = [("g_pre_mix", g_pre_mix), ("g_post_mix", g_post_mix), ("g_pre_ff", g_pre_ff),
             ("g_post_ff", g_post_ff), ("g_mix_out", g_mix_out), ("ln_g", gm_ln_g), ("ln_b", gm_ln_b),
             ("conv_b", rg_conv_b), ("ft_b", flat(ft_b))]
    for dirn in range(2):
        named += [(f"b_a{dirn}", 0.5 * flat(rg_b_a[:, dirn])), (f"b_x{dirn}", 0.5 * flat(rg_b_x[:, dirn])),
                  (f"lam{dirn}", flat(rg_lam[:, dirn]))]
    named += [(f"conv_w{i}", rg_conv_w[:, i]) for i in range(RG_CONV)]
    vecs = _LayerVecs(named)
    packed = vecs.pack(named)

    mod, w_in_b = _modulation(c, w_ada, b_ada, w_in)
    mod4 = mod.reshape(depth, bsz, 6, d)

    for l in range(depth):
        ygm, gact, xr, ftx, h_fwd, w2_b, w_out_b = _mix_in_scan(l, x, mod4, vecs, packed, w_in_b, ws, bs, wg,
                                                                w_ff2, w_out, d_gm=d_gm, d_rg=d_rg)
        yft, w1_b = _fourier(l, ftx, tables, w_ff1)
        x = _mix_out_ffn(l, xr, wg, h_fwd, x, ygm, gact, yft, mod4, vecs, packed, ftw, w_out_b, w1_b, w2_b)
    return x
```

```python
import functools
import math

import numpy as np
import jax
import jax.numpy as jnp
from jax import lax
from jax.experimental import pallas as pl
from jax.experimental.pallas import tpu as pltpu

F32 = jnp.float32
BF16 = jnp.bfloat16

EPS = 1e-6
RG_C = 8.0
RG_CONV = 4
RG_CONV_LEFT = 2
GM_HEADS = 4
GM_CHUNK = 128
RG_HEADS = 8
FT_GROUPS = 4
FT_GROUP_DIM = 64

V7X_SUBLANES = 8
V7X_LANES = 128
V7X_VMEM_LIMIT_BYTES = 56 * 1024 * 1024

MOD_COL_BLOCKS = 2
ROW_TILE = 512
FF_CHUNKS = 8
SCAN_ROW_PARTS = 2
SCAN_STAGES = 3 * SCAN_ROW_PARTS + 2
SCAN_SEGS = V7X_SUBLANES
SCAN_SEG_LEN = ROW_TILE // SCAN_SEGS
SCAN_SEG_STRIDE = SCAN_SEG_LEN + V7X_SUBLANES
SEG_PITCH = SCAN_SEG_LEN + 3 * V7X_SUBLANES

FT_N1 = 64
FT_PAD = FT_N1 + V7X_SUBLANES
FT_LANES = 128
FT_OUT_PAD = V7X_SUBLANES
FT_UNROLL_A = 16
FT_UNROLL_B = 16


def _gelu(x):
    c = math.sqrt(2.0 / math.pi)
    t = jnp.tanh(x * ((c * 0.044715) * (x * x) + c))
    return (0.5 * x) * (t + 1.0)


def _sigmoid(x):
    return 0.5 * jnp.tanh(0.5 * x) + 0.5


def _rms(x, g):
    return x * lax.rsqrt(jnp.mean(x * x, axis=-1, keepdims=True) + EPS) * g


def _dot(a, b):
    return jnp.dot(a, b, preferred_element_type=F32)


def _params(*sem):
    return pltpu.CompilerParams(dimension_semantics=sem, vmem_limit_bytes=V7X_VMEM_LIMIT_BYTES)


def _resident(shape, index_map):
    return pl.BlockSpec(shape, index_map, pipeline_mode=pl.Buffered(1))


class _LayerVecs:
    def __init__(self, named):
        self.range = {}
        self.total = 0
        for name, v in named:
            n = v.shape[-1]
            assert n % V7X_LANES == 0
            self.range[name] = (self.total, n)
            self.total += n

    def pack(self, named):
        return jnp.concatenate([v for _, v in named], axis=-1)[:, None, :]

    def get(self, ref, name, lo=0, hi=None):
        start, n = self.range[name]
        return ref[0, :, start + lo:start + (n if hi is None else hi)]

    def spec(self, l):
        return pl.BlockSpec((1, 1, self.total), lambda *g: (l, 0, 0))


def _mod_kernel(ct_ref, w_ref, b_ref, wf_ref, o_ref, wb_ref):
    wb_ref[0] = wf_ref[0].astype(BF16)

    ct = ct_ref[...]
    cond = ct * _sigmoid(ct)
    d = ct.shape[0]
    for j in range(w_ref.shape[-1] // d):
        cols = slice(j * d, (j + 1) * d)
        w = w_ref[0, :, cols]
        for b in range(ct.shape[1]):
            o_ref[0, b:b + 1, cols] = jnp.sum(w * cond[:, b:b + 1], axis=0, keepdims=True) + b_ref[0, :, cols]


def _modulation(c, w_ada, b_ada, w_in):
    depth, d, d6 = w_ada.shape
    bsz = c.shape[0]
    wcols = d6 // MOD_COL_BLOCKS
    icols = w_in.shape[-1] // MOD_COL_BLOCKS
    return pl.pallas_call(
        _mod_kernel,
        grid=(depth, MOD_COL_BLOCKS),
        in_specs=[
            pl.BlockSpec((d, bsz), lambda l, j: (0, 0)),
            pl.BlockSpec((1, d, wcols), lambda l, j: (l, 0, j)),
            pl.BlockSpec((1, 1, wcols), lambda l, j: (l, 0, j)),
            pl.BlockSpec((1, d, icols), lambda l, j: (l, 0, j)),
        ],
        out_specs=[
            pl.BlockSpec((1, bsz, wcols), lambda l, j: (l, 0, j)),
            pl.BlockSpec((1, d, icols), lambda l, j: (l, 0, j)),
        ],
        out_shape=[
            jax.ShapeDtypeStruct((depth, bsz, d6), F32),
            jax.ShapeDtypeStruct(w_in.shape, BF16),
        ],
        compiler_params=_params("arbitrary", "arbitrary"),
        name="modulation",
    )(c.T, w_ada, b_ada.reshape(depth, 1, d6), w_in)


def _fourier_tables(seq):
    n1 = FT_N1
    n2 = seq // n1
    gd = FT_GROUP_DIM
    j = np.arange(gd)
    ang = 2.0 * np.pi * np.outer(j, j) / gd
    eye = np.eye(FT_LANES // gd)
    w1 = np.concatenate([np.kron(eye, np.cos(ang)), -np.kron(eye, np.sin(ang))], axis=1)
    k2 = np.arange(n2)[None, :, None]
    s2 = np.arange(n2)[None, None, :]
    s1 = np.arange(n1)[:, None, None]
    ang_a = 2.0 * np.pi * ((k2 * (n1 * s2 + s1)) % seq) / seq
    ta = np.concatenate([np.cos(ang_a), np.sin(ang_a)], axis=2)
    i1 = np.arange(n1)
    ang_b = 2.0 * np.pi * np.outer(i1, i1) / n1
    tb = np.concatenate([np.cos(ang_b), np.sin(ang_b)], axis=1)
    return (jnp.asarray(w1, F32), jnp.asarray(ta, F32), jnp.asarray(tb, F32))


def _fourier_kernel(x_ref, w1_ref, ta_ref, tb_ref, wf_ref, o_ref, wb_ref, p_ref, *, seq, scale):
    n1 = FT_N1
    n2 = seq // n1
    ln = FT_LANES

    wb_ref[...] = wf_ref[0].astype(BF16)

    w1 = w1_ref[...].astype(BF16)
    for c in range(seq // ROW_TILE):
        p = _dot(x_ref[0, c * ROW_TILE:(c + 1) * ROW_TILE, :].astype(BF16), w1)
        for r in range(ROW_TILE // n1):
            dst = pl.ds((c * (ROW_TILE // n1) + r) * FT_PAD, n1)
            p_ref[0, dst, :] = p[r * n1:(r + 1) * n1, 0:ln]
            p_ref[1, dst, :] = p[r * n1:(r + 1) * n1, ln:2 * ln]

    def stage_a(u, carry):
        res = []
        for d in range(FT_UNROLL_A):
            s1 = u * FT_UNROLL_A + d
            rows = pl.ds(s1, n2, stride=FT_PAD)
            g = jnp.concatenate([p_ref[0, rows, :], p_ref[1, rows, :]], axis=0).astype(BF16)
            t = ta_ref[s1]
            t_im = jnp.concatenate([-t[:, n2:2 * n2], t[:, 0:n2]], axis=1)
            res.append((rows, _dot(t.astype(BF16), g), _dot(t_im.astype(BF16), g)))
        for rows, b_re, b_im in res:
            p_ref[0, rows, :] = b_re
            p_ref[1, rows, :] = b_im
        return carry

    lax.fori_loop(0, n1 // FT_UNROLL_A, stage_a, 0)

    tb = tb_ref[...].astype(BF16)
    pitch = n2 + FT_OUT_PAD
    for k1 in range(n1):
        o_ref[0, k1 * pitch + n2:(k1 + 1) * pitch, :] = jnp.zeros((FT_OUT_PAD, ln), F32)

    def stage_b(u, carry):
        for d in range(FT_UNROLL_B):
            k2 = u * FT_UNROLL_B + d
            rows = pl.ds(pl.multiple_of(k2 * FT_PAD, V7X_SUBLANES), n1)
            blk = jnp.concatenate([p_ref[0, rows, :], p_ref[1, rows, :]], axis=0).astype(BF16)
            o_ref[0, pl.ds(k2, n1, stride=pitch), :] = _dot(tb, blk) * scale
        return carry

    lax.fori_loop(0, n2 // FT_UNROLL_B, stage_b, 0)


def _fourier(l, ftx, tables, w_ff1):
    bsz, seq, d_ft = ftx.shape
    w1, ta, tb = tables
    n1 = FT_N1
    n2 = seq // n1
    nj = d_ft // FT_LANES
    _, d, d_ff = w_ff1.shape
    wrows = d // (bsz * nj)
    blk = pl.BlockSpec((1, seq, FT_LANES), lambda b, j: (b, 0, j))
    return pl.pallas_call(
        functools.partial(_fourier_kernel, seq=seq, scale=1.0 / math.sqrt(seq * FT_GROUP_DIM)),
        grid=(bsz, nj),
        in_specs=[
            blk,
            pl.BlockSpec((FT_LANES, 2 * FT_LANES), lambda b, j: (0, 0)),
            _resident((n1, n2, 2 * n2), lambda b, j: (0, 0, 0)),
            pl.BlockSpec((n1, 2 * n1), lambda b, j: (0, 0)),
            pl.BlockSpec((1, wrows, d_ff), lambda b, j: (l, b * nj + j, 0)),
        ],
        out_specs=[
            pl.BlockSpec((1, n1 * (n2 + FT_OUT_PAD), FT_LANES), lambda b, j: (b, 0, j)),
            pl.BlockSpec((wrows, d_ff), lambda b, j: (b * nj + j, 0)),
        ],
        out_shape=[
            jax.ShapeDtypeStruct((bsz, n1 * (n2 + FT_OUT_PAD), d_ft), F32),
            jax.ShapeDtypeStruct((d, d_ff), BF16),
        ],
        scratch_shapes=[pltpu.VMEM((2, n2 * FT_PAD, FT_LANES), F32)],
        compiler_params=_params("arbitrary", "arbitrary"),
        name="fourier",
    )(ftx, w1, ta, tb, w_ff1)


class _ScanScratch:
    def __init__(self, a, b, h, e, p, c, carry):
        self.a, self.b, self.h = a, b, h
        self.e, self.p, self.c, self.carry = e, p, c, carry


def _scan_scratch_shapes(d_rg):
    nblk = d_rg // V7X_LANES
    pad_rows = SCAN_SEGS * SCAN_SEG_STRIDE
    return [
        pltpu.VMEM((nblk, ROW_TILE, V7X_LANES), F32),
        pltpu.VMEM((nblk, ROW_TILE, V7X_LANES), F32),
        pltpu.VMEM((nblk, pad_rows, V7X_LANES), F32),
        pltpu.VMEM((SCAN_SEGS, d_rg), F32),
        pltpu.VMEM((SCAN_SEGS, d_rg), F32),
        pltpu.VMEM((SCAN_SEGS, d_rg), F32),
        pltpu.VMEM((1, d_rg), F32),
    ]


def _rg_scan_tile(conv_part, first_step, wg_ref, ba, bx, lam, scr, *, reverse):
    tm = ROW_TILE
    d_rg = scr.e.shape[-1]
    half = d_rg // 2
    nblk = d_rg // V7X_LANES

    nl = -lam
    c8h = (-0.5 * RG_C) * (jnp.maximum(nl, 0.0) + jnp.log1p(jnp.exp(-jnp.abs(nl))))

    rh = tm // SCAN_ROW_PARTS
    for part in range(SCAN_ROW_PARTS):
        xr = conv_part(part)
        xrb = xr.astype(BF16)
        yield
        for hh in range(2):
            cols = slice(hh * half, (hh + 1) * half)
            pre = _dot(xrb[:, cols], wg_ref[0, 0, hh])
            t_r = jnp.tanh(pre[:, 0:half] + ba[:, cols])
            t_i = jnp.tanh(pre[:, half:2 * half] + bx[:, cols])
            log_a = c8h[:, cols] * t_r + c8h[:, cols]
            a = jnp.exp(log_a)
            th = jnp.tanh(log_a)
            nth = -0.5 * th
            sq = jnp.where(nth > 0.0, nth * lax.rsqrt(nth), 0.0) * lax.rsqrt(1.0 - th)
            bq = sq * ((t_i + 1.0) * xr[:, cols])
            for kk in range(half // V7X_LANES):
                k = hh * (half // V7X_LANES) + kk
                lanes = slice(kk * V7X_LANES, (kk + 1) * V7X_LANES)
                scr.a[k, part * rh:(part + 1) * rh, :] = a[:, lanes]
                scr.b[k, part * rh:(part + 1) * rh, :] = bq[:, lanes]
            yield

    steps = range(SCAN_SEG_LEN - 1, -1, -1) if reverse else range(SCAN_SEG_LEN)

    def at_step(t):
        return slice(t * SCAN_SEGS, (t + 1) * SCAN_SEGS)

    for k in range(nblk):
        lanes = slice(k * V7X_LANES, (k + 1) * V7X_LANES)
        e = jnp.zeros((SCAN_SEGS, V7X_LANES), F32)
        p = e + 1.0
        for t in steps:
            at = scr.a[k, at_step(t), :]
            e = at * e + scr.b[k, at_step(t), :]
            p = at * p
        scr.e[:, lanes] = e
        scr.p[:, lanes] = p
    yield

    c = jnp.where(first_step, 0.0, scr.carry[...])
    order = range(SCAN_SEGS - 1, -1, -1) if reverse else range(SCAN_SEGS)
    for j in order:
        scr.c[j:j + 1, :] = c
        c = scr.p[j:j + 1, :] * c + scr.e[j:j + 1, :]
    scr.carry[...] = c

    for k in range(nblk):
        h = scr.c[:, k * V7X_LANES:(k + 1) * V7X_LANES]
        for t in steps:
            h = scr.a[k, at_step(t), :] * h + scr.b[k, at_step(t), :]
            scr.h[k, pl.ds(t, SCAN_SEGS, stride=SCAN_SEG_STRIDE), :] = h


def _advance(stages, n):
    for _ in range(n):
        next(stages, None)


def _scan_out_block(scr, k, j):
    return scr.h[k, j * SCAN_SEG_STRIDE:j * SCAN_SEG_STRIDE + SCAN_SEG_LEN, :]


def _mix_in_scan_kernel(x_ref, m_ref, p_ref, win_ref, ws_ref, bs_ref, wg_ref, wf_ref, wof_ref,
                        ygm_ref, gact_ref, xr_ref, ftx_ref, hf_ref, wb_ref, wob_ref, ext_ref, segp_ref, *scratch,
                        vecs, nt, ntot, d_gm, d_rg):
    scr = _ScanScratch(*scratch)
    g = pl.program_id(0)
    tm = ROW_TILE
    halo = V7X_SUBLANES
    o_v, o_g, o_x, o_f = d_gm, 2 * d_gm, 2 * d_gm + d_rg, 2 * d_gm + 2 * d_rg

    wb_ref[...] = wf_ref[0].astype(BF16)
    wob_ref[...] = wof_ref[0].astype(BF16)

    def step(project, scan_on):
        gs = (g - 1) % nt
        if project:
            x = x_ref[0]
            sh1 = m_ref[0, 0, 0:1, :]
            sc1 = m_ref[0, 0, 1:2, :]
            h = (_rms(x, vecs.get(p_ref, "g_pre_mix") * (1.0 + sc1)) + sh1).astype(BF16)
            rgx = _dot(h, win_ref[0, :, o_x:o_f])

        def conv_part(part):
            nblk = d_rg // V7X_LANES
            seg_rows = SCAN_SEG_LEN + 2 * halo
            if part == 0:
                nxt = (jnp.where(gs == nt - 1, 0.0, rgx[0:halo]) if project
                       else jnp.zeros((halo, d_rg), F32))
                ext_ref[halo + tm:halo + tm + halo, :] = nxt
                for k in range(nblk):
                    for j in range(SCAN_SEGS):
                        segp_ref[k, j * SEG_PITCH:j * SEG_PITCH + seg_rows, :] = (
                            ext_ref[j * SCAN_SEG_LEN:j * SCAN_SEG_LEN + seg_rows,
                                    k * V7X_LANES:(k + 1) * V7X_LANES])
            steps = SCAN_SEG_LEN // SCAN_ROW_PARTS
            cols = []
            for k in range(nblk):
                lanes = slice(k * V7X_LANES, (k + 1) * V7X_LANES)
                taps = [jnp.broadcast_to(vecs.get(p_ref, f"conv_w{i}")[:, lanes], (SCAN_SEGS, V7X_LANES))
                        for i in range(RG_CONV)]
                bias = jnp.broadcast_to(vecs.get(p_ref, "conv_b")[:, lanes], (SCAN_SEGS, V7X_LANES))
                slabs = [segp_ref[k, pl.ds(halo - RG_CONV_LEFT + part * steps + i, SCAN_SEGS, stride=SEG_PITCH), :]
                         for i in range(steps + RG_CONV - 1)]
                out = []
                for t in range(steps):
                    acc = bias
                    for i in range(RG_CONV):
                        acc = acc + slabs[t + i] * taps[i]
                    out.append(acc)
                cols.append(jnp.concatenate(out, axis=0))
            xr = jnp.concatenate(cols, axis=1)
            rh = tm // SCAN_ROW_PARTS
            xr_ref[0, part * rh:(part + 1) * rh, :] = xr
            return xr

        if scan_on:
            scan = _rg_scan_tile(conv_part, gs == 0, wg_ref, vecs.get(p_ref, "b_a0"), vecs.get(p_ref, "b_x0"),
                                 vecs.get(p_ref, "lam0"), scr, reverse=False)
        else:
            scr.carry[...] = jnp.zeros_like(scr.carry)
            scan = iter(())
        if not project:
            for _ in scan:
                pass
        else:
            _advance(scan, 1)
            v = _gelu(_dot(h, win_ref[0, :, o_v:o_g]))
            mu = jnp.mean(v, axis=-1, keepdims=True)
            vc = v - mu
            var = jnp.mean(vc * vc, axis=-1, keepdims=True)
            vn = (vc * lax.rsqrt(var + EPS) * vecs.get(p_ref, "ln_g") + vecs.get(p_ref, "ln_b")).astype(BF16)
            _advance(scan, 1)
            gact_ref[0] = _gelu(_dot(h, win_ref[0, :, o_g:o_x])).astype(BF16)
            _advance(scan, 3 * (SCAN_ROW_PARTS - 1) + 1 - 2)
            if scan_on:
                ext_ref[0:halo, :] = jnp.where(g % nt == 0, 0.0, ext_ref[tm:tm + halo, :])
            else:
                ext_ref[0:halo, :] = jnp.zeros((halo, d_rg), F32)
            ext_ref[halo:halo + tm, :] = rgx
            ftx_ref[0] = _dot(h, win_ref[0, :, o_f:o_f + d_gm]).astype(BF16)
            _advance(scan, 1)
            u = _gelu(_dot(h, win_ref[0, :, 0:o_v]))
            _advance(scan, 1)
            head_dim = d_gm // GM_HEADS
            head = lax.broadcasted_iota(jnp.int32, (GM_CHUNK, d_gm), 1) // head_dim
            for c in range(tm // GM_CHUNK):
                rows = slice(c * GM_CHUNK, (c + 1) * GM_CHUNK)
                r = _dot(ws_ref[0], vn[rows])
                s = r[0:GM_CHUNK]
                for hh in range(1, GM_HEADS):
                    s = jnp.where(head == hh, r[hh * GM_CHUNK:(hh + 1) * GM_CHUNK], s)
                y = u[rows] * (s + bs_ref[0])
                ygm_ref[0, rows, :] = _rms(y, vecs.get(p_ref, "g_mix_out", 0, d_gm)).astype(BF16)
                if c == 1:
                    _advance(scan, 1)
            for _ in scan:
                pass
        if scan_on:
            for k in range(d_rg // V7X_LANES):
                for j in range(SCAN_SEGS):
                    hf_ref[0, j * SCAN_SEG_LEN:(j + 1) * SCAN_SEG_LEN,
                           k * V7X_LANES:(k + 1) * V7X_LANES] = _scan_out_block(scr, k, j)

    pl.when(g == 0)(functools.partial(step, True, False))
    pl.when((g > 0) & (g < ntot))(functools.partial(step, True, True))
    pl.when(g == ntot)(functools.partial(step, False, True))


def _mix_in_scan(l, x, mod4, vecs, packed, w_in, ws, bs, wg, w_ff2, w_out, *, d_gm, d_rg):
    bsz, seq, d = x.shape
    d_in = w_in.shape[-1]
    nt = seq // ROW_TILE
    ntot = bsz * nt
    d_ff = w_ff2.shape[1]
    wrows = d_ff // ntot
    orows = w_out.shape[1] // ntot
    wblk = lambda g: jnp.minimum(g, ntot - 1)

    def proj_bt(g):
        gp = jnp.minimum(g, ntot - 1)
        return gp // nt, gp % nt

    def scan_bt(g):
        gs = jnp.maximum(g - 1, 0)
        return gs // nt, gs % nt

    lay = lambda g: (l, 0, 0)
    row = lambda g: (*proj_bt(g), 0)
    return pl.pallas_call(
        functools.partial(_mix_in_scan_kernel, vecs=vecs, nt=nt, ntot=ntot, d_gm=d_gm, d_rg=d_rg),
        grid=(ntot + 1,),
        in_specs=[
            pl.BlockSpec((1, ROW_TILE, d), row),
            pl.BlockSpec((1, 1, 6, d), lambda g: (l, proj_bt(g)[0], 0, 0)),
            vecs.spec(l),
            _resident((1, d, d_in), lay),
            pl.BlockSpec((1, GM_HEADS * GM_CHUNK, GM_CHUNK), lay),
            pl.BlockSpec((1, GM_CHUNK, d_gm), lay),
            _resident((1, 1, 2, d_rg // 2, d_rg), lambda g: (l, 0, 0, 0, 0)),
            pl.BlockSpec((1, wrows, d), lambda g: (l, wblk(g), 0)),
            pl.BlockSpec((1, orows, d), lambda g: (l, wblk(g), 0)),
        ],
        out_specs=[
            pl.BlockSpec((1, ROW_TILE, d_gm), row),
            pl.BlockSpec((1, ROW_TILE, d_rg), row),
            pl.BlockSpec((1, ROW_TILE, d_rg), lambda g: (*scan_bt(g), 0)),
            pl.BlockSpec((1, ROW_TILE, d_gm), row),
            pl.BlockSpec((1, ROW_TILE, d_rg), lambda g: (*scan_bt(g), 0)),
            pl.BlockSpec((wrows, d), lambda g: (wblk(g), 0)),
            pl.BlockSpec((orows, d), lambda g: (wblk(g), 0)),
        ],
        out_shape=[
            jax.ShapeDtypeStruct((bsz, seq, d_gm), BF16),
            jax.ShapeDtypeStruct((bsz, seq, d_rg), BF16),
            jax.ShapeDtypeStruct((bsz, seq, d_rg), F32),
            jax.ShapeDtypeStruct((bsz, seq, d_gm), BF16),
            jax.ShapeDtypeStruct((bsz, seq, d_rg), F32),
            jax.ShapeDtypeStruct((d_ff, d), BF16),
            jax.ShapeDtypeStruct(w_out.shape[1:], BF16),
        ],
        scratch_shapes=[
            pltpu.VMEM((ROW_TILE + 2 * V7X_SUBLANES, d_rg), F32),
            pltpu.VMEM((d_rg // V7X_LANES, SCAN_SEGS * SEG_PITCH, V7X_LANES), F32),
        ] + _scan_scratch_shapes(d_rg),
        compiler_params=_params("arbitrary"),
        name="mix_in_scan",
    )(x, mod4, packed, w_in, ws, bs, wg, w_ff2, w_out)


def _mix_out_ffn_kernel(xr_ref, wg_ref, hf_ref, x_ref, ygm_ref, gact_ref, yft_ref, m_ref, p_ref, ftw_ref,
                        wout_ref, w1_ref, w2_ref,
                        o_ref, *scratch, vecs, nt, ntot, ff_chunk, ft_groups):
    scr = _ScanScratch(*scratch)
    g = pl.program_id(0)
    d_rg = xr_ref.shape[-1]
    d_gm = ygm_ref.shape[-1]
    o_ft = d_gm + d_rg
    d_mix = wout_ref.shape[0]
    gs = jnp.minimum(g, ntot - 1) % nt

    def conv_part(part):
        rh = ROW_TILE // SCAN_ROW_PARTS
        return xr_ref[0, part * rh:(part + 1) * rh, :]

    def scan_stages():
        return _rg_scan_tile(conv_part, gs == 0, wg_ref, vecs.get(p_ref, "b_a1"), vecs.get(p_ref, "b_x1"),
                             vecs.get(p_ref, "lam1"), scr, reverse=True)

    @pl.when(g == 0)
    def _():
        scr.carry[...] = jnp.zeros_like(scr.carry)
        for _ in scan_stages():
            pass

    @pl.when(g > 0)
    def _():
        h_bwd = jnp.concatenate(
            [jnp.concatenate([_scan_out_block(scr, k, j) for j in range(SCAN_SEGS)], axis=0)
             for k in range(d_rg // V7X_LANES)], axis=1)

        gt1 = m_ref[0, 0, 2:3, :]
        yrg = _rms((hf_ref[0] + h_bwd) * gact_ref[0].astype(F32), vecs.get(p_ref, "g_mix_out", d_gm, o_ft))
        ft_pitch = yft_ref.shape[1] // ft_groups
        ft_rows = ROW_TILE // ft_groups
        yft_raw = jnp.concatenate(
            [yft_ref[0, a * ft_pitch:a * ft_pitch + ft_rows, :] for a in range(ft_groups)], axis=0)
        yft = _dot(yft_raw.astype(BF16), ftw_ref[0]) + vecs.get(p_ref, "ft_b")
        yft = _rms(yft, vecs.get(p_ref, "g_mix_out", o_ft, d_mix))
        o = _dot(ygm_ref[0], wout_ref[0:d_gm, :])
        o = o + _dot(yrg.astype(BF16), wout_ref[d_gm:o_ft, :])
        o = o + _dot(yft.astype(BF16), wout_ref[o_ft:d_mix, :])
        x = x_ref[0] + _rms(o, gt1 * vecs.get(p_ref, "g_post_mix"))

        sh2 = m_ref[0, 0, 3:4, :]
        sc2 = m_ref[0, 0, 4:5, :]
        gt2 = m_ref[0, 0, 5:6, :]
        h = (_rms(x, vecs.get(p_ref, "g_pre_ff") * (1.0 + sc2)) + sh2).astype(BF16)

        scan = scan_stages()
        next(scan)
        d_ff = w1_ref.shape[-1]
        acc = None
        for c in range(d_ff // ff_chunk):
            cols = slice(c * ff_chunk, (c + 1) * ff_chunk)
            a = jnp.maximum(_dot(h, w1_ref[:, cols]), 0.0)
            part = _dot((a * a).astype(BF16), w2_ref[cols, :])
            acc = part if acc is None else acc + part
            _advance(scan, pl.cdiv(SCAN_STAGES - 1, d_ff // ff_chunk))
        for _ in scan:
            pass
        o_ref[0] = x + _rms(acc, gt2 * vecs.get(p_ref, "g_post_ff"))


def _mix_out_ffn(l, xr, wg, h_fwd, x, ygm, gact, yft, mod4, vecs, packed, ftw, w_out, w1, w2):
    bsz, seq, d = x.shape
    d_rg = xr.shape[-1]
    d_gm = ygm.shape[-1]
    d_ft = yft.shape[-1]
    d_mix = w_out.shape[0]
    d_ff = w1.shape[-1]
    nt = seq // ROW_TILE
    ntot = bsz * nt

    def scan_bt(g):
        gs = jnp.minimum(g, ntot - 1)
        return gs // nt, nt - 1 - gs % nt

    def tail_bt(g):
        gf = jnp.maximum(g - 1, 0)
        return gf // nt, nt - 1 - gf % nt

    lay = lambda g: (l, 0, 0)
    row = lambda g: (*tail_bt(g), 0)
    in_specs = [
        pl.BlockSpec((1, ROW_TILE, d_rg), lambda g: (*scan_bt(g), 0)),
        _resident((1, 1, 2, d_rg // 2, d_rg), lambda g: (l, 1, 0, 0, 0)),
        pl.BlockSpec((1, ROW_TILE, d_rg), row),
        pl.BlockSpec((1, ROW_TILE, d), row),
        pl.BlockSpec((1, ROW_TILE, d_gm), row),
        pl.BlockSpec((1, ROW_TILE, d_rg), row),
        pl.BlockSpec((1, yft.shape[1] // (seq // ROW_TILE), d_ft), row),
        pl.BlockSpec((1, 1, 6, d), lambda g: (l, tail_bt(g)[0], 0, 0)),
        vecs.spec(l),
        _resident((1, d_ft, d_ft), lay),
        _resident((d_mix, d), lambda g: (0, 0)),
        _resident((d, d_ff), lambda g: (0, 0)),
        _resident((d_ff, d), lambda g: (0, 0)),
    ]
    return pl.pallas_call(
        functools.partial(_mix_out_ffn_kernel, vecs=vecs, nt=nt, ntot=ntot, ff_chunk=d_ff // FF_CHUNKS,
                          ft_groups=ROW_TILE // (seq // FT_N1)),
        grid=(ntot + 1,),
        in_specs=in_specs,
        out_specs=pl.BlockSpec((1, ROW_TILE, d), row),
        out_shape=jax.ShapeDtypeStruct((bsz, seq, d), F32),
        scratch_shapes=_scan_scratch_shapes(d_rg),
        compiler_params=_params("arbitrary"),
        name="mix_out_ffn",
    )(xr, wg, h_fwd, x, ygm, gact, yft, mod4, packed, ftw, w_out, w1, w2)


def _block_diag(w, per):
    *lead, n, hd, _ = w.shape
    w = w.reshape(*lead, n // per, per, hd, hd)
    keep = [(0, 0)] * (len(lead) + 2)
    rows = [jnp.pad(w[..., a, :, :], keep + [(a * hd, (per - 1 - a) * hd)]) for a in range(per)]
    return jnp.concatenate(rows, axis=-2)


def kernel(x, c, w_ada, b_ada, g_pre_mix, g_post_mix, w_in, gm_ln_g, gm_ln_b, gm_w_s, gm_b_s,
           rg_conv_w, rg_conv_b, rg_w_a, rg_b_a, rg_w_x, rg_b_x, rg_lam, ft_w, ft_b,
           g_mix_out, w_out, g_pre_ff, g_post_ff, w_ff1, w_ff2):
    bsz, seq, d = x.shape
    depth = w_in.shape[0]
    d_gm = gm_ln_g.shape[-1]
    d_rg = rg_conv_b.shape[-1]
    d_ft = ft_w.shape[1] * ft_w.shape[2]
    assert seq % ROW_TILE == 0 and ROW_TILE % GM_CHUNK == 0 and seq % FT_N1 == 0
    assert d_ft % FT_LANES == 0 and ft_w.shape[2] == FT_GROUP_DIM
    assert rg_conv_w.shape[1] == RG_CONV and rg_w_a.shape[2] == RG_HEADS

    ws = gm_w_s.reshape(depth, GM_HEADS * GM_CHUNK, GM_CHUNK).astype(BF16)
    bs = jnp.repeat(jnp.swapaxes(gm_b_s, 1, 2), d_gm // GM_HEADS, axis=2)
    per = RG_HEADS // 2
    wg = (0.5 * jnp.concatenate([_block_diag(rg_w_a, per), _block_diag(rg_w_x, per)], axis=-1)).astype(BF16)
    ftw = _block_diag(ft_w, FT_GROUPS).reshape(depth, d_ft, d_ft).astype(BF16)
    tables = _fourier_tables(seq)

    flat = lambda a: a.reshape(depth, -1)
    named = [("g_pre_mix", g_pre_mix), ("g_post_mix", g_post_mix), ("g_pre_ff", g_pre_ff),
             ("g_post_ff", g_post_ff), ("g_mix_out", g_mix_out), ("ln_g", gm_ln_g), ("ln_b", gm_ln_b),
             ("conv_b", rg_conv_b), ("ft_b", flat(ft_b))]
    for dirn in range(2):
        named += [(f"b_a{dirn}", 0.5 * flat(rg_b_a[:, dirn])), (f"b_x{dirn}", 0.5 * flat(rg_b_x[:, dirn])),
                  (f"lam{dirn}", flat(rg_lam[:, dirn]))]
    named += [(f"conv_w{i}", rg_conv_w[:, i]) for i in range(RG_CONV)]
    vecs = _LayerVecs(named)
    packed = vecs.pack(named)

    mod, w_in_b = _modulation(c, w_ada, b_ada, w_in)
    mod4 = mod.reshape(depth, bsz, 6, d)

    for l in range(depth):
        ygm, gact, xr, ftx, h_fwd, w2_b, w_out_b = _mix_in_scan(l, x, mod4, vecs, packed, w_in_b, ws, bs, wg,
                                                                w_ff2, w_out, d_gm=d_gm, d_rg=d_rg)
        yft, w1_b = _fourier(l, ftx, tables, w_ff1)
        x = _mix_out_ffn(l, xr, wg, h_fwd, x, ygm, gact, yft, mod4, vecs, packed, ftw, w_out_b, w1_b, w2_b)
    return x
```

```python
import functools
import math

import numpy as np
import jax
import jax.numpy as jnp
from jax import lax
from jax.experimental import pallas as pl
from jax.experimental.pallas import tpu as pltpu

F32 = jnp.float32
BF16 = jnp.bfloat16

EPS = 1e-6
RG_C = 8.0
RG_CONV = 4
RG_CONV_LEFT = 2
GM_HEADS = 4
GM_CHUNK = 128
RG_HEADS = 8
FT_GROUPS = 4
FT_GROUP_DIM = 64

V7X_SUBLANES = 8
V7X_LANES = 128
V7X_VMEM_LIMIT_BYTES = 56 * 1024 * 1024

MOD_COL_BLOCKS = 2
ROW_TILE = 512
FF_CHUNKS = 8
SCAN_ROW_PARTS = 2
SCAN_STAGES = 3 * SCAN_ROW_PARTS + 2
SCAN_SEGS = V7X_SUBLANES
SCAN_SEG_LEN = ROW_TILE // SCAN_SEGS
SCAN_SEG_STRIDE = SCAN_SEG_LEN + V7X_SUBLANES
SEG_PITCH = SCAN_SEG_LEN + 3 * V7X_SUBLANES

FT_N1 = 64
FT_PAD = FT_N1 + V7X_SUBLANES
FT_LANES = 128
FT_OUT_PAD = V7X_SUBLANES
FT_UNROLL_A = 32
FT_UNROLL_B = 32


def _gelu(x):
    c = math.sqrt(2.0 / math.pi)
    t = jnp.tanh(x * ((c * 0.044715) * (x * x) + c))
    return (0.5 * x) * (t + 1.0)


def _sigmoid(x):
    return 0.5 * jnp.tanh(0.5 * x) + 0.5


def _rms(x, g):
    return x * lax.rsqrt(jnp.mean(x * x, axis=-1, keepdims=True) + EPS) * g


def _dot(a, b):
    return jnp.dot(a, b, preferred_element_type=F32)


def _params(*sem):
    return pltpu.CompilerParams(dimension_semantics=sem, vmem_limit_bytes=V7X_VMEM_LIMIT_BYTES)


def _resident(shape, index_map):
    return pl.BlockSpec(shape, index_map, pipeline_mode=pl.Buffered(1))


class _LayerVecs:
    def __init__(self, named):
        self.range = {}
        self.total = 0
        for name, v in named:
            n = v.shape[-1]
            assert n % V7X_LANES == 0
            self.range[name] = (self.total, n)
            self.total += n

    def pack(self, named):
        return jnp.concatenate([v for _, v in named], axis=-1)[:, None, :]

    def get(self, ref, name, lo=0, hi=None):
        start, n = self.range[name]
        return ref[0, :, start + lo:start + (n if hi is None else hi)]

    def spec(self, l):
        return pl.BlockSpec((1, 1, self.total), lambda *g: (l, 0, 0))


def _mod_kernel(ct_ref, w_ref, b_ref, wf_ref, o_ref, wb_ref):
    wb_ref[0] = wf_ref[0].astype(BF16)

    ct = ct_ref[...]
    cond = ct * _sigmoid(ct)
    d = ct.shape[0]
    for j in range(w_ref.shape[-1] // d):
        cols = slice(j * d, (j + 1) * d)
        w = w_ref[0, :, cols]
        for b in range(ct.shape[1]):
            o_ref[0, b:b + 1, cols] = jnp.sum(w * cond[:, b:b + 1], axis=0, keepdims=True) + b_ref[0, :, cols]


def _modulation(c, w_ada, b_ada, w_in):
    depth, d, d6 = w_ada.shape
    bsz = c.shape[0]
    wcols = d6 // MOD_COL_BLOCKS
    icols = w_in.shape[-1] // MOD_COL_BLOCKS
    return pl.pallas_call(
        _mod_kernel,
        grid=(depth, MOD_COL_BLOCKS),
        in_specs=[
            pl.BlockSpec((d, bsz), lambda l, j: (0, 0)),
            pl.BlockSpec((1, d, wcols), lambda l, j: (l, 0, j)),
            pl.BlockSpec((1, 1, wcols), lambda l, j: (l, 0, j)),
            pl.BlockSpec((1, d, icols), lambda l, j: (l, 0, j)),
        ],
        out_specs=[
            pl.BlockSpec((1, bsz, wcols), lambda l, j: (l, 0, j)),
            pl.BlockSpec((1, d, icols), lambda l, j: (l, 0, j)),
        ],
        out_shape=[
            jax.ShapeDtypeStruct((depth, bsz, d6), F32),
            jax.ShapeDtypeStruct(w_in.shape, BF16),
        ],
        compiler_params=_params("arbitrary", "arbitrary"),
        name="modulation",
    )(c.T, w_ada, b_ada.reshape(depth, 1, d6), w_in)


def _fourier_tables(seq):
    n1 = FT_N1
    n2 = seq // n1
    gd = FT_GROUP_DIM
    j = np.arange(gd)
    ang = 2.0 * np.pi * np.outer(j, j) / gd
    eye = np.eye(FT_LANES // gd)
    w1 = np.concatenate([np.kron(eye, np.cos(ang)), -np.kron(eye, np.sin(ang))], axis=1)
    k2 = np.arange(n2)[None, :, None]
    s2 = np.arange(n2)[None, None, :]
    s1 = np.arange(n1)[:, None, None]
    ang_a = 2.0 * np.pi * ((k2 * (n1 * s2 + s1)) % seq) / seq
    ta = np.concatenate([np.cos(ang_a), np.sin(ang_a)], axis=2)
    i1 = np.arange(n1)
    ang_b = 2.0 * np.pi * np.outer(i1, i1) / n1
    tb = np.concatenate([np.cos(ang_b), np.sin(ang_b)], axis=1)
    return (jnp.asarray(w1, F32), jnp.asarray(ta, F32), jnp.asarray(tb, F32))


def _fourier_kernel(x_ref, w1_ref, ta_ref, tb_ref, wf_ref, o_ref, wb_ref, p_ref, *, seq, scale):
    n1 = FT_N1
    n2 = seq // n1
    ln = FT_LANES

    wb_ref[...] = wf_ref[0].astype(BF16)

    w1 = w1_ref[...].astype(BF16)
    for c in range(seq // ROW_TILE):
        p = _dot(x_ref[0, c * ROW_TILE:(c + 1) * ROW_TILE, :].astype(BF16), w1)
        for r in range(ROW_TILE // n1):
            dst = pl.ds((c * (ROW_TILE // n1) + r) * FT_PAD, n1)
            p_ref[0, dst, :] = p[r * n1:(r + 1) * n1, 0:ln]
            p_ref[1, dst, :] = p[r * n1:(r + 1) * n1, ln:2 * ln]

    def stage_a(u, carry):
        res = []
        for d in range(FT_UNROLL_A):
            s1 = u * FT_UNROLL_A + d
            rows = pl.ds(s1, n2, stride=FT_PAD)
            g = jnp.concatenate([p_ref[0, rows, :], p_ref[1, rows, :]], axis=0).astype(BF16)
            t = ta_ref[s1]
            t_im = jnp.concatenate([-t[:, n2:2 * n2], t[:, 0:n2]], axis=1)
            res.append((rows, _dot(t.astype(BF16), g), _dot(t_im.astype(BF16), g)))
        for rows, b_re, b_im in res:
            p_ref[0, rows, :] = b_re
            p_ref[1, rows, :] = b_im
        return carry

    lax.fori_loop(0, n1 // FT_UNROLL_A, stage_a, 0)

    tb = tb_ref[...].astype(BF16)
    pitch = n2 + FT_OUT_PAD
    for k1 in range(n1):
        o_ref[0, k1 * pitch + n2:(k1 + 1) * pitch, :] = jnp.zeros((FT_OUT_PAD, ln), F32)

    def stage_b(u, carry):
        for d in range(FT_UNROLL_B):
            k2 = u * FT_UNROLL_B + d
            rows = pl.ds(pl.multiple_of(k2 * FT_PAD, V7X_SUBLANES), n1)
            blk = jnp.concatenate([p_ref[0, rows, :], p_ref[1, rows, :]], axis=0).astype(BF16)
            o_ref[0, pl.ds(k2, n1, stride=pitch), :] = _dot(tb, blk) * scale
        return carry

    lax.fori_loop(0, n2 // FT_UNROLL_B, stage_b, 0)


def _fourier(l, ftx, tables, w_ff1):
    bsz, seq, d_ft = ftx.shape
    w1, ta, tb = tables
    n1 = FT_N1
    n2 = seq // n1
    nj = d_ft // FT_LANES
    _, d, d_ff = w_ff1.shape
    wrows = d // (bsz * nj)
    blk = pl.BlockSpec((1, seq, FT_LANES), lambda b, j: (b, 0, j))
    return pl.pallas_call(
        functools.partial(_fourier_kernel, seq=seq, scale=1.0 / math.sqrt(seq * FT_GROUP_DIM)),
        grid=(bsz, nj),
        in_specs=[
            blk,
            pl.BlockSpec((FT_LANES, 2 * FT_LANES), lambda b, j: (0, 0)),
            _resident((n1, n2, 2 * n2), lambda b, j: (0, 0, 0)),
            pl.BlockSpec((n1, 2 * n1), lambda b, j: (0, 0)),
            pl.BlockSpec((1, wrows, d_ff), lambda b, j: (l, b * nj + j, 0)),
        ],
        out_specs=[
            pl.BlockSpec((1, n1 * (n2 + FT_OUT_PAD), FT_LANES), lambda b, j: (b, 0, j)),
            pl.BlockSpec((wrows, d_ff), lambda b, j: (b * nj + j, 0)),
        ],
        out_shape=[
            jax.ShapeDtypeStruct((bsz, n1 * (n2 + FT_OUT_PAD), d_ft), F32),
            jax.ShapeDtypeStruct((d, d_ff), BF16),
        ],
        scratch_shapes=[pltpu.VMEM((2, n2 * FT_PAD, FT_LANES), F32)],
        compiler_params=_params("arbitrary", "arbitrary"),
        name="fourier",
    )(ftx, w1, ta, tb, w_ff1)


class _ScanScratch:
    def __init__(self, a, b, h, e, p, c, carry):
        self.a, self.b, self.h = a, b, h
        self.e, self.p, self.c, self.carry = e, p, c, carry


def _scan_scratch_shapes(d_rg):
    nblk = d_rg // V7X_LANES
    pad_rows = SCAN_SEGS * SCAN_SEG_STRIDE
    return [
        pltpu.VMEM((nblk, ROW_TILE, V7X_LANES), F32),
        pltpu.VMEM((nblk, ROW_TILE, V7X_LANES), F32),
        pltpu.VMEM((nblk, pad_rows, V7X_LANES), F32),
        pltpu.VMEM((SCAN_SEGS, d_rg), F32),
        pltpu.VMEM((SCAN_SEGS, d_rg), F32),
        pltpu.VMEM((SCAN_SEGS, d_rg), F32),
        pltpu.VMEM((1, d_rg), F32),
    ]


def _rg_scan_tile(conv_part, first_step, wg_ref, ba, bx, lam, scr, *, reverse):
    tm = ROW_TILE
    d_rg = scr.e.shape[-1]
    half = d_rg // 2
    nblk = d_rg // V7X_LANES

    nl = -lam
    c8h = (-0.5 * RG_C) * (jnp.maximum(nl, 0.0) + jnp.log1p(jnp.exp(-jnp.abs(nl))))

    rh = tm // SCAN_ROW_PARTS
    for part in range(SCAN_ROW_PARTS):
        xr = conv_part(part)
        xrb = xr.astype(BF16)
        yield
        for hh in range(2):
            cols = slice(hh * half, (hh + 1) * half)
            pre = _dot(xrb[:, cols], wg_ref[0, 0, hh])
            t_r = jnp.tanh(pre[:, 0:half] + ba[:, cols])
            t_i = jnp.tanh(pre[:, half:2 * half] + bx[:, cols])
            log_a = c8h[:, cols] * t_r + c8h[:, cols]
            a = jnp.exp(log_a)
            th = jnp.tanh(log_a)
            nth = -0.5 * th
            sq = jnp.where(nth > 0.0, nth * lax.rsqrt(nth), 0.0) * lax.rsqrt(1.0 - th)
            bq = sq * ((t_i + 1.0) * xr[:, cols])
            for kk in range(half // V7X_LANES):
                k = hh * (half // V7X_LANES) + kk
                lanes = slice(kk * V7X_LANES, (kk + 1) * V7X_LANES)
                scr.a[k, part * rh:(part + 1) * rh, :] = a[:, lanes]
                scr.b[k, part * rh:(part + 1) * rh, :] = bq[:, lanes]
            yield

    steps = range(SCAN_SEG_LEN - 1, -1, -1) if reverse else range(SCAN_SEG_LEN)

    def at_step(t):
        return slice(t * SCAN_SEGS, (t + 1) * SCAN_SEGS)

    for k in range(nblk):
        lanes = slice(k * V7X_LANES, (k + 1) * V7X_LANES)
        e = jnp.zeros((SCAN_SEGS, V7X_LANES), F32)
        p = e + 1.0
        for t in steps:
            at = scr.a[k, at_step(t), :]
            e = at * e + scr.b[k, at_step(t), :]
            p = at * p
        scr.e[:, lanes] = e
        scr.p[:, lanes] = p
    yield

    c = jnp.where(first_step, 0.0, scr.carry[...])
    order = range(SCAN_SEGS - 1, -1, -1) if reverse else range(SCAN_SEGS)
    for j in order:
        scr.c[j:j + 1, :] = c
        c = scr.p[j:j + 1, :] * c + scr.e[j:j + 1, :]
    scr.carry[...] = c

    for k in range(nblk):
        h = scr.c[:, k * V7X_LANES:(k + 1) * V7X_LANES]
        for t in steps:
            h = scr.a[k, at_step(t), :] * h + scr.b[k, at_step(t), :]
            scr.h[k, pl.ds(t, SCAN_SEGS, stride=SCAN_SEG_STRIDE), :] = h


def _advance(stages, n):
    for _ in range(n):
        next(stages, None)


def _scan_out_block(scr, k, j):
    return scr.h[k, j * SCAN_SEG_STRIDE:j * SCAN_SEG_STRIDE + SCAN_SEG_LEN, :]


def _mix_in_scan_kernel(x_ref, m_ref, p_ref, win_ref, ws_ref, bs_ref, wg_ref, wf_ref, wof_ref,
                        ygm_ref, gact_ref, xr_ref, ftx_ref, hf_ref, wb_ref, wob_ref, ext_ref, segp_ref, *scratch,
                        vecs, nt, ntot, d_gm, d_rg):
    scr = _ScanScratch(*scratch)
    g = pl.program_id(0)
    tm = ROW_TILE
    halo = V7X_SUBLANES
    o_v, o_g, o_x, o_f = d_gm, 2 * d_gm, 2 * d_gm + d_rg, 2 * d_gm + 2 * d_rg

    wb_ref[...] = wf_ref[0].astype(BF16)
    wob_ref[...] = wof_ref[0].astype(BF16)

    def step(project, scan_on):
        gs = (g - 1) % nt
        if project:
            x = x_ref[0]
            sh1 = m_ref[0, 0, 0:1, :]
            sc1 = m_ref[0, 0, 1:2, :]
            h = (_rms(x, vecs.get(p_ref, "g_pre_mix") * (1.0 + sc1)) + sh1).astype(BF16)
            rgx = _dot(h, win_ref[0, :, o_x:o_f])

        def conv_part(part):
            nblk = d_rg // V7X_LANES
            seg_rows = SCAN_SEG_LEN + 2 * halo
            if part == 0:
                nxt = (jnp.where(gs == nt - 1, 0.0, rgx[0:halo]) if project
                       else jnp.zeros((halo, d_rg), F32))
                ext_ref[halo + tm:halo + tm + halo, :] = nxt
                for k in range(nblk):
                    for j in range(SCAN_SEGS):
                        segp_ref[k, j * SEG_PITCH:j * SEG_PITCH + seg_rows, :] = (
                            ext_ref[j * SCAN_SEG_LEN:j * SCAN_SEG_LEN + seg_rows,
                                    k * V7X_LANES:(k + 1) * V7X_LANES])
            steps = SCAN_SEG_LEN // SCAN_ROW_PARTS
            cols = []
            for k in range(nblk):
                lanes = slice(k * V7X_LANES, (k + 1) * V7X_LANES)
                taps = [jnp.broadcast_to(vecs.get(p_ref, f"conv_w{i}")[:, lanes], (SCAN_SEGS, V7X_LANES))
                        for i in range(RG_CONV)]
                bias = jnp.broadcast_to(vecs.get(p_ref, "conv_b")[:, lanes], (SCAN_SEGS, V7X_LANES))
                slabs = [segp_ref[k, pl.ds(halo - RG_CONV_LEFT + part * steps + i, SCAN_SEGS, stride=SEG_PITCH), :]
                         for i in range(steps + RG_CONV - 1)]
                out = []
                for t in range(steps):
                    acc = bias
                    for i in range(RG_CONV):
                        acc = acc + slabs[t + i] * taps[i]
                    out.append(acc)
                cols.append(jnp.concatenate(out, axis=0))
            xr = jnp.concatenate(cols, axis=1)
            rh = tm // SCAN_ROW_PARTS
            xr_ref[0, part * rh:(part + 1) * rh, :] = xr
            return xr

        if scan_on:
            scan = _rg_scan_tile(conv_part, gs == 0, wg_ref, vecs.get(p_ref, "b_a0"), vecs.get(p_ref, "b_x0"),
                                 vecs.get(p_ref, "lam0"), scr, reverse=False)
        else:
            scr.carry[...] = jnp.zeros_like(scr.carry)
            scan = iter(())
        if not project:
            for _ in scan:
                pass
        else:
            _advance(scan, 1)
            v = _gelu(_dot(h, win_ref[0, :, o_v:o_g]))
            mu = jnp.mean(v, axis=-1, keepdims=True)
            vc = v - mu
            var = jnp.mean(vc * vc, axis=-1, keepdims=True)
            vn = (vc * lax.rsqrt(var + EPS) * vecs.get(p_ref, "ln_g") + vecs.get(p_ref, "ln_b")).astype(BF16)
            _advance(scan, 1)
            gact_ref[0] = _gelu(_dot(h, win_ref[0, :, o_g:o_x])).astype(BF16)
            _advance(scan, 3 * (SCAN_ROW_PARTS - 1) + 1 - 2)
            if scan_on:
                ext_ref[0:halo, :] = jnp.where(g % nt == 0, 0.0, ext_ref[tm:tm + halo, :])
            else:
                ext_ref[0:halo, :] = jnp.zeros((halo, d_rg), F32)
            ext_ref[halo:halo + tm, :] = rgx
            ftx_ref[0] = _dot(h, win_ref[0, :, o_f:o_f + d_gm]).astype(BF16)
            _advance(scan, 1)
            u = _gelu(_dot(h, win_ref[0, :, 0:o_v]))
            _advance(scan, 1)
            head_dim = d_gm // GM_HEADS
            head = lax.broadcasted_iota(jnp.int32, (GM_CHUNK, d_gm), 1) // head_dim
            for c in range(tm // GM_CHUNK):
                rows = slice(c * GM_CHUNK, (c + 1) * GM_CHUNK)
                r = _dot(ws_ref[0], vn[rows])
                s = r[0:GM_CHUNK]
                for hh in range(1, GM_HEADS):
                    s = jnp.where(head == hh, r[hh * GM_CHUNK:(hh + 1) * GM_CHUNK], s)
                y = u[rows] * (s + bs_ref[0])
                ygm_ref[0, rows, :] = _rms(y, vecs.get(p_ref, "g_mix_out", 0, d_gm)).astype(BF16)
                if c == 1:
                    _advance(scan, 1)
            for _ in scan:
                pass
        if scan_on:
            for k in range(d_rg // V7X_LANES):
                for j in range(SCAN_SEGS):
                    hf_ref[0, j * SCAN_SEG_LEN:(j + 1) * SCAN_SEG_LEN,
                           k * V7X_LANES:(k + 1) * V7X_LANES] = _scan_out_block(scr, k, j)

    pl.when(g == 0)(functools.partial(step, True, False))
    pl.when((g > 0) & (g < ntot))(functools.partial(step, True, True))
    pl.when(g == ntot)(functools.partial(step, False, True))


def _mix_in_scan(l, x, mod4, vecs, packed, w_in, ws, bs, wg, w_ff2, w_out, *, d_gm, d_rg):
    bsz, seq, d = x.shape
    d_in = w_in.shape[-1]
    nt = seq // ROW_TILE
    ntot = bsz * nt
    d_ff = w_ff2.shape[1]
    wrows = d_ff // ntot
    orows = w_out.shape[1] // ntot
    wblk = lambda g: jnp.minimum(g, ntot - 1)

    def proj_bt(g):
        gp = jnp.minimum(g, ntot - 1)
        return gp // nt, gp % nt

    def scan_bt(g):
        gs = jnp.maximum(g - 1, 0)
        return gs // nt, gs % nt

    lay = lambda g: (l, 0, 0)
    row = lambda g: (*proj_bt(g), 0)
    return pl.pallas_call(
        functools.partial(_mix_in_scan_kernel, vecs=vecs, nt=nt, ntot=ntot, d_gm=d_gm, d_rg=d_rg),
        grid=(ntot + 1,),
        in_specs=[
            pl.BlockSpec((1, ROW_TILE, d), row),
            pl.BlockSpec((1, 1, 6, d), lambda g: (l, proj_bt(g)[0], 0, 0)),
            vecs.spec(l),
            _resident((1, d, d_in), lay),
            pl.BlockSpec((1, GM_HEADS * GM_CHUNK, GM_CHUNK), lay),
            pl.BlockSpec((1, GM_CHUNK, d_gm), lay),
            _resident((1, 1, 2, d_rg // 2, d_rg), lambda g: (l, 0, 0, 0, 0)),
            pl.BlockSpec((1, wrows, d), lambda g: (l, wblk(g), 0)),
            pl.BlockSpec((1, orows, d), lambda g: (l, wblk(g), 0)),
        ],
        out_specs=[
            pl.BlockSpec((1, ROW_TILE, d_gm), row),
            pl.BlockSpec((1, ROW_TILE, d_rg), row),
            pl.BlockSpec((1, ROW_TILE, d_rg), lambda g: (*scan_bt(g), 0)),
            pl.BlockSpec((1, ROW_TILE, d_gm), row),
            pl.BlockSpec((1, ROW_TILE, d_rg), lambda g: (*scan_bt(g), 0)),
            pl.BlockSpec((wrows, d), lambda g: (wblk(g), 0)),
            pl.BlockSpec((orows, d), lambda g: (wblk(g), 0)),
        ],
        out_shape=[
            jax.ShapeDtypeStruct((bsz, seq, d_gm), BF16),
            jax.ShapeDtypeStruct((bsz, seq, d_rg), BF16),
            jax.ShapeDtypeStruct((bsz, seq, d_rg), F32),
            jax.ShapeDtypeStruct((bsz, seq, d_gm), BF16),
            jax.ShapeDtypeStruct((bsz, seq, d_rg), F32),
            jax.ShapeDtypeStruct((d_ff, d), BF16),
            jax.ShapeDtypeStruct(w_out.shape[1:], BF16),
        ],
        scratch_shapes=[
            pltpu.VMEM((ROW_TILE + 2 * V7X_SUBLANES, d_rg), F32),
            pltpu.VMEM((d_rg // V7X_LANES, SCAN_SEGS * SEG_PITCH, V7X_LANES), F32),
        ] + _scan_scratch_shapes(d_rg),
        compiler_params=_params("arbitrary"),
        name="mix_in_scan",
    )(x, mod4, packed, w_in, ws, bs, wg, w_ff2, w_out)


def _mix_out_ffn_kernel(xr_ref, wg_ref, hf_ref, x_ref, ygm_ref, gact_ref, yft_ref, m_ref, p_ref, ftw_ref,
                        wout_ref, w1_ref, w2_ref,
                        o_ref, *scratch, vecs, nt, ntot, ff_chunk, ft_groups):
    scr = _ScanScratch(*scratch)
    g = pl.program_id(0)
    d_rg = xr_ref.shape[-1]
    d_gm = ygm_ref.shape[-1]
    o_ft = d_gm + d_rg
    d_mix = wout_ref.shape[0]
    gs = jnp.minimum(g, ntot - 1) % nt

    def conv_part(part):
        rh = ROW_TILE // SCAN_ROW_PARTS
        return xr_ref[0, part * rh:(part + 1) * rh, :]

    def scan_stages():
        return _rg_scan_tile(conv_part, gs == 0, wg_ref, vecs.get(p_ref, "b_a1"), vecs.get(p_ref, "b_x1"),
                             vecs.get(p_ref, "lam1"), scr, reverse=True)

    @pl.when(g == 0)
    def _():
        scr.carry[...] = jnp.zeros_like(scr.carry)
        for _ in scan_stages():
            pass

    @pl.when(g > 0)
    def _():
        h_bwd = jnp.concatenate(
            [jnp.concatenate([_scan_out_block(scr, k, j) for j in range(SCAN_SEGS)], axis=0)
             for k in range(d_rg // V7X_LANES)], axis=1)

        gt1 = m_ref[0, 0, 2:3, :]
        yrg = _rms((hf_ref[0] + h_bwd) * gact_ref[0].astype(F32), vecs.get(p_ref, "g_mix_out", d_gm, o_ft))
        ft_pitch = yft_ref.shape[1] // ft_groups
        ft_rows = ROW_TILE // ft_groups
        yft_raw = jnp.concatenate(
            [yft_ref[0, a * ft_pitch:a * ft_pitch + ft_rows, :] for a in range(ft_groups)], axis=0)
        yft = _dot(yft_raw.astype(BF16), ftw_ref[0]) + vecs.get(p_ref, "ft_b")
        yft = _rms(yft, vecs.get(p_ref, "g_mix_out", o_ft, d_mix))
        o = _dot(ygm_ref[0], wout_ref[0:d_gm, :])
        o = o + _dot(yrg.astype(BF16), wout_ref[d_gm:o_ft, :])
        o = o + _dot(yft.astype(BF16), wout_ref[o_ft:d_mix, :])
        x = x_ref[0] + _rms(o, gt1 * vecs.get(p_ref, "g_post_mix"))

        sh2 = m_ref[0, 0, 3:4, :]
        sc2 = m_ref[0, 0, 4:5, :]
        gt2 = m_ref[0, 0, 5:6, :]
        h = (_rms(x, vecs.get(p_ref, "g_pre_ff") * (1.0 + sc2)) + sh2).astype(BF16)

        scan = scan_stages()
        next(scan)
        d_ff = w1_ref.shape[-1]
        acc = None
        for c in range(d_ff // ff_chunk):
            cols = slice(c * ff_chunk, (c + 1) * ff_chunk)
            a = jnp.maximum(_dot(h, w1_ref[:, cols]), 0.0)
            part = _dot((a * a).astype(BF16), w2_ref[cols, :])
            acc = part if acc is None else acc + part
            _advance(scan, pl.cdiv(SCAN_STAGES - 1, d_ff // ff_chunk))
        for _ in scan:
            pass
        o_ref[0] = x + _rms(acc, gt2 * vecs.get(p_ref, "g_post_ff"))


def _mix_out_ffn(l, xr, wg, h_fwd, x, ygm, gact, yft, mod4, vecs, packed, ftw, w_out, w1, w2):
    bsz, seq, d = x.shape
    d_rg = xr.shape[-1]
    d_gm = ygm.shape[-1]
    d_ft = yft.shape[-1]
    d_mix = w_out.shape[0]
    d_ff = w1.shape[-1]
    nt = seq // ROW_TILE
    ntot = bsz * nt

    def scan_bt(g):
        gs = jnp.minimum(g, ntot - 1)
        return gs // nt, nt - 1 - gs % nt

    def tail_bt(g):
        gf = jnp.maximum(g - 1, 0)
        return gf // nt, nt - 1 - gf % nt

    lay = lambda g: (l, 0, 0)
    row = lambda g: (*tail_bt(g), 0)
    in_specs = [
        pl.BlockSpec((1, ROW_TILE, d_rg), lambda g: (*scan_bt(g), 0)),
        _resident((1, 1, 2, d_rg // 2, d_rg), lambda g: (l, 1, 0, 0, 0)),
        pl.BlockSpec((1, ROW_TILE, d_rg), row),
        pl.BlockSpec((1, ROW_TILE, d), row),
        pl.BlockSpec((1, ROW_TILE, d_gm), row),
        pl.BlockSpec((1, ROW_TILE, d_rg), row),
        pl.BlockSpec((1, yft.shape[1] // (seq // ROW_TILE), d_ft), row),
        pl.BlockSpec((1, 1, 6, d), lambda g: (l, tail_bt(g)[0], 0, 0)),
        vecs.spec(l),
        _resident((1, d_ft, d_ft), lay),
        _resident((d_mix, d), lambda g: (0, 0)),
        _resident((d, d_ff), lambda g: (0, 0)),
        _resident((d_ff, d), lambda g: (0, 0)),
    ]
    return pl.pallas_call(
        functools.partial(_mix_out_ffn_kernel, vecs=vecs, nt=nt, ntot=ntot, ff_chunk=d_ff // FF_CHUNKS,
                          ft_groups=ROW_TILE // (seq // FT_N1)),
        grid=(ntot + 1,),
        in_specs=in_specs,
        out_specs=pl.BlockSpec((1, ROW_TILE, d), row),
        out_shape=jax.ShapeDtypeStruct((bsz, seq, d), F32),
        scratch_shapes=_scan_scratch_shapes(d_rg),
        compiler_params=_params("arbitrary"),
        name="mix_out_ffn",
    )(xr, wg, h_fwd, x, ygm, gact, yft, mod4, packed, ftw, w_out, w1, w2)


def _block_diag(w, per):
    *lead, n, hd, _ = w.shape
    w = w.reshape(*lead, n // per, per, hd, hd)
    keep = [(0, 0)] * (len(lead) + 2)
    rows = [jnp.pad(w[..., a, :, :], keep + [(a * hd, (per - 1 - a) * hd)]) for a in range(per)]
    return jnp.concatenate(rows, axis=-2)


def kernel(x, c, w_ada, b_ada, g_pre_mix, g_post_mix, w_in, gm_ln_g, gm_ln_b, gm_w_s, gm_b_s,
           rg_conv_w, rg_conv_b, rg_w_a, rg_b_a, rg_w_x, rg_b_x, rg_lam, ft_w, ft_b,
           g_mix_out, w_out, g_pre_ff, g_post_ff, w_ff1, w_ff2):
    bsz, seq, d = x.shape
    depth = w_in.shape[0]
    d_gm = gm_ln_g.shape[-1]
    d_rg = rg_conv_b.shape[-1]
    d_ft = ft_w.shape[1] * ft_w.shape[2]
    assert seq % ROW_TILE == 0 and ROW_TILE % GM_CHUNK == 0 and seq % FT_N1 == 0
    assert d_ft % FT_LANES == 0 and ft_w.shape[2] == FT_GROUP_DIM
    assert rg_conv_w.shape[1] == RG_CONV and rg_w_a.shape[2] == RG_HEADS

    ws = gm_w_s.reshape(depth, GM_HEADS * GM_CHUNK, GM_CHUNK).astype(BF16)
    bs = jnp.repeat(jnp.swapaxes(gm_b_s, 1, 2), d_gm // GM_HEADS, axis=2)
    per = RG_HEADS // 2
    wg = (0.5 * jnp.concatenate([_block_diag(rg_w_a, per), _block_diag(rg_w_x, per)], axis=-1)).astype(BF16)
    ftw = _block_diag(ft_w, FT_GROUPS).reshape(depth, d_ft, d_ft).astype(BF16)
    tables = _fourier_tables(seq)

    flat = lambda a: a.reshape(depth, -1)
    named = [("g_pre_mix", g_pre_mix), ("g_post_mix", g_post_mix), ("g_pre_ff", g_pre_ff),
             ("g_post_ff", g_post_ff), ("g_mix_out", g_mix_out), ("ln_g", gm_ln_g), ("ln_b", gm_ln_b),
             ("conv_b", rg_conv_b), ("ft_b", flat(ft_b))]
    for dirn in range(2):
        named += [(f"b_a{dirn}", 0.5 * flat(rg_b_a[:, dirn])), (f"b_x{dirn}", 0.5 * flat(rg_b_x[:, dirn])),
                  (f"lam{dirn}", flat(rg_lam[:, dirn]))]
    named += [(f"conv_w{i}", rg_conv_w[:, i]) for i in range(RG_CONV)]
    vecs = _LayerVecs(named)
    packed = vecs.pack(named)

    mod, w_in_b = _modulation(c, w_ada, b_ada, w_in)
    mod4 = mod.reshape(depth, bsz, 6, d)

    for l in range(depth):
        ygm, gact, xr, ftx, h_fwd, w2_b, w_out_b = _mix_in_scan(l, x, mod4, vecs, packed, w_in_b, ws, bs, wg,
                                                                w_ff2, w_out, d_gm=d_gm, d_rg=d_rg)
        yft, w1_b = _fourier(l, ftx, tables, w_ff1)
        x = _mix_out_ffn(l, xr, wg, h_fwd, x, ygm, gact, yft, mod4, vecs, packed, ftw, w_out_b, w1_b, w2_b)
    return x
```

```python
import functools
import math

import numpy as np
import jax
import jax.numpy as jnp
from jax import lax
from jax.experimental import pallas as pl
from jax.experimental.pallas import tpu as pltpu

F32 = jnp.float32
BF16 = jnp.bfloat16

EPS = 1e-6
RG_C = 8.0
RG_CONV = 4
RG_CONV_LEFT = 2
GM_HEADS = 4
GM_CHUNK = 128
RG_HEADS = 8
FT_GROUPS = 4
FT_GROUP_DIM = 64

V7X_SUBLANES = 8
V7X_LANES = 128
V7X_VMEM_LIMIT_BYTES = 56 * 1024 * 1024

MOD_COL_BLOCKS = 2
ROW_TILE = 512
FF_CHUNKS = 8
SCAN_ROW_PARTS = 2
SCAN_STAGES = 3 * SCAN_ROW_PARTS + 2
SCAN_SEGS = V7X_SUBLANES
SCAN_SEG_LEN = ROW_TILE // SCAN_SEGS
SCAN_SEG_STRIDE = SCAN_SEG_LEN + V7X_SUBLANES
SEG_PITCH = SCAN_SEG_LEN + 3 * V7X_SUBLANES

FT_N1 = 64
FT_PAD = FT_N1 + V7X_SUBLANES
FT_LANES = 128
FT_OUT_PAD = V7X_SUBLANES
FT_UNROLL_A = 32
FT_UNROLL_B = 32


def _gelu(x):
    c = math.sqrt(2.0 / math.pi)
    t = jnp.tanh(x * ((c * 0.044715) * (x * x) + c))
    return (0.5 * x) * (t + 1.0)


def _sigmoid(x):
    return 0.5 * jnp.tanh(0.5 * x) + 0.5


def _rms(x, g):
    return x * lax.rsqrt(jnp.mean(x * x, axis=-1, keepdims=True) + EPS) * g


def _dot(a, b):
    return jnp.dot(a, b, preferred_element_type=F32)


def _params(*sem):
    return pltpu.CompilerParams(dimension_semantics=sem, vmem_limit_bytes=V7X_VMEM_LIMIT_BYTES)


def _resident(shape, index_map):
    return pl.BlockSpec(shape, index_map, pipeline_mode=pl.Buffered(1))


class _LayerVecs:
    def __init__(self, named):
        self.range = {}
        self.total = 0
        for name, v in named:
            n = v.shape[-1]
            assert n % V7X_LANES == 0
            self.range[name] = (self.total, n)
            self.total += n

    def pack(self, named):
        return jnp.concatenate([v for _, v in named], axis=-1)[:, None, :]

    def get(self, ref, name, lo=0, hi=None):
        start, n = self.range[name]
        return ref[0, :, start + lo:start + (n if hi is None else hi)]

    def spec(self, l):
        return pl.BlockSpec((1, 1, self.total), lambda *g: (l, 0, 0))


def _mod_kernel(ct_ref, w_ref, b_ref, wf_ref, o_ref, wb_ref):
    wb_ref[0] = wf_ref[0].astype(BF16)

    ct = ct_ref[...]
    cond = ct * _sigmoid(ct)
    d = ct.shape[0]
    for j in range(w_ref.shape[-1] // d):
        cols = slice(j * d, (j + 1) * d)
        w = w_ref[0, :, cols]
        for b in range(ct.shape[1]):
            o_ref[0, b:b + 1, cols] = jnp.sum(w * cond[:, b:b + 1], axis=0, keepdims=True) + b_ref[0, :, cols]


def _modulation(c, w_ada, b_ada, w_in):
    depth, d, d6 = w_ada.shape
    bsz = c.shape[0]
    wcols = d6 // MOD_COL_BLOCKS
    icols = w_in.shape[-1] // MOD_COL_BLOCKS
    return pl.pallas_call(
        _mod_kernel,
        grid=(depth, MOD_COL_BLOCKS),
        in_specs=[
            pl.BlockSpec((d, bsz), lambda l, j: (0, 0)),
            pl.BlockSpec((1, d, wcols), lambda l, j: (l, 0, j)),
            pl.BlockSpec((1, 1, wcols), lambda l, j: (l, 0, j)),
            pl.BlockSpec((1, d, icols), lambda l, j: (l, 0, j)),
        ],
        out_specs=[
            pl.BlockSpec((1, bsz, wcols), lambda l, j: (l, 0, j)),
            pl.BlockSpec((1, d, icols), lambda l, j: (l, 0, j)),
        ],
        out_shape=[
            jax.ShapeDtypeStruct((depth, bsz, d6), F32),
            jax.ShapeDtypeStruct(w_in.shape, BF16),
        ],
        compiler_params=_params("arbitrary", "arbitrary"),
        name="modulation",
    )(c.T, w_ada, b_ada.reshape(depth, 1, d6), w_in)


def _fourier_tables(seq):
    n1 = FT_N1
    n2 = seq // n1
    gd = FT_GROUP_DIM
    j = np.arange(gd)
    ang = 2.0 * np.pi * np.outer(j, j) / gd
    eye = np.eye(FT_LANES // gd)
    w1 = np.concatenate([np.kron(eye, np.cos(ang)), -np.kron(eye, np.sin(ang))], axis=1)
    k2 = np.arange(n2)[None, :, None]
    s2 = np.arange(n2)[None, None, :]
    s1 = np.arange(n1)[:, None, None]
    ang_a = 2.0 * np.pi * ((k2 * (n1 * s2 + s1)) % seq) / seq
    ta = np.concatenate([np.cos(ang_a), np.sin(ang_a)], axis=2)
    i1 = np.arange(n1)
    ang_b = 2.0 * np.pi * np.outer(i1, i1) / n1
    tb = np.concatenate([np.cos(ang_b), np.sin(ang_b)], axis=1)
    return (jnp.asarray(w1, F32), jnp.asarray(ta, F32), jnp.asarray(tb, F32))


def _fourier_kernel(x_ref, w1_ref, ta_ref, tb_ref, wf_ref, o_ref, wb_ref, p_ref, *, seq, scale):
    n1 = FT_N1
    n2 = seq // n1
    ln = FT_LANES

    wb_ref[...] = wf_ref[0].astype(BF16)

    w1 = w1_ref[...].astype(BF16)
    for c in range(seq // ROW_TILE):
        p = _dot(x_ref[0, c * ROW_TILE:(c + 1) * ROW_TILE, :].astype(BF16), w1)
        for r in range(ROW_TILE // n1):
            dst = pl.ds((c * (ROW_TILE // n1) + r) * FT_PAD, n1)
            p_ref[0, dst, :] = p[r * n1:(r + 1) * n1, 0:ln]
            p_ref[1, dst, :] = p[r * n1:(r + 1) * n1, ln:2 * ln]

    def stage_a(u, carry):
        res = []
        for d in range(FT_UNROLL_A):
            s1 = u * FT_UNROLL_A + d
            rows = pl.ds(s1, n2, stride=FT_PAD)
            g = jnp.concatenate([p_ref[0, rows, :], p_ref[1, rows, :]], axis=0).astype(BF16)
            t = ta_ref[s1]
            t_im = jnp.concatenate([-t[:, n2:2 * n2], t[:, 0:n2]], axis=1)
            res.append((rows, _dot(t.astype(BF16), g), _dot(t_im.astype(BF16), g)))
        for rows, b_re, b_im in res:
            p_ref[0, rows, :] = b_re
            p_ref[1, rows, :] = b_im
        return carry

    lax.fori_loop(0, n1 // FT_UNROLL_A, stage_a, 0)

    tb = tb_ref[...].astype(BF16)
    pitch = n2 + FT_OUT_PAD
    for k1 in range(n1):
        o_ref[0, k1 * pitch + n2:(k1 + 1) * pitch, :] = jnp.zeros((FT_OUT_PAD, ln), F32)

    def stage_b(u, carry):
        for d in range(FT_UNROLL_B):
            k2 = u * FT_UNROLL_B + d
            rows = pl.ds(pl.multiple_of(k2 * FT_PAD, V7X_SUBLANES), n1)
            blk = jnp.concatenate([p_ref[0, rows, :], p_ref[1, rows, :]], axis=0).astype(BF16)
            o_ref[0, pl.ds(k2, n1, stride=pitch), :] = _dot(tb, blk) * scale
        return carry

    lax.fori_loop(0, n2 // FT_UNROLL_B, stage_b, 0)


def _fourier(l, ftx, tables, w_ff1):
    bsz, seq, d_ft = ftx.shape
    w1, ta, tb = tables
    n1 = FT_N1
    n2 = seq // n1
    nj = d_ft // FT_LANES
    _, d, d_ff = w_ff1.shape
    wrows = d // (bsz * nj)
    blk = pl.BlockSpec((1, seq, FT_LANES), lambda b, j: (b, 0, j))
    return pl.pallas_call(
        functools.partial(_fourier_kernel, seq=seq, scale=1.0 / math.sqrt(seq * FT_GROUP_DIM)),
        grid=(bsz, nj),
        in_specs=[
            blk,
            pl.BlockSpec((FT_LANES, 2 * FT_LANES), lambda b, j: (0, 0)),
            _resident((n1, n2, 2 * n2), lambda b, j: (0, 0, 0)),
            pl.BlockSpec((n1, 2 * n1), lambda b, j: (0, 0)),
            pl.BlockSpec((1, wrows, d_ff), lambda b, j: (l, b * nj + j, 0)),
        ],
        out_specs=[
            pl.BlockSpec((1, n1 * (n2 + FT_OUT_PAD), FT_LANES), lambda b, j: (b, 0, j)),
            pl.BlockSpec((wrows, d_ff), lambda b, j: (b * nj + j, 0)),
        ],
        out_shape=[
            jax.ShapeDtypeStruct((bsz, n1 * (n2 + FT_OUT_PAD), d_ft), F32),
            jax.ShapeDtypeStruct((d, d_ff), BF16),
        ],
        scratch_shapes=[pltpu.VMEM((2, n2 * FT_PAD, FT_LANES), F32)],
        compiler_params=_params("arbitrary", "arbitrary"),
        name="fourier",
    )(ftx, w1, ta, tb, w_ff1)


class _ScanScratch:
    def __init__(self, a, b, h, e, p, c, carry):
        self.a, self.b, self.h = a, b, h
        self.e, self.p, self.c, self.carry = e, p, c, carry


def _scan_scratch_shapes(d_rg):
    nblk = d_rg // V7X_LANES
    pad_rows = SCAN_SEGS * SCAN_SEG_STRIDE
    return [
        pltpu.VMEM((nblk, ROW_TILE, V7X_LANES), F32),
        pltpu.VMEM((nblk, ROW_TILE, V7X_LANES), F32),
        pltpu.VMEM((nblk, pad_rows, V7X_LANES), F32),
        pltpu.VMEM((SCAN_SEGS, d_rg), F32),
        pltpu.VMEM((SCAN_SEGS, d_rg), F32),
        pltpu.VMEM((SCAN_SEGS, d_rg), F32),
        pltpu.VMEM((1, d_rg), F32),
    ]


def _rg_scan_tile(conv_part, first_step, wg_ref, ba, bx, lam, scr, *, reverse):
    tm = ROW_TILE
    d_rg = scr.e.shape[-1]
    half = d_rg // 2
    nblk = d_rg // V7X_LANES

    nl = -lam
    c8h = (-0.5 * RG_C) * (jnp.maximum(nl, 0.0) + jnp.log1p(jnp.exp(-jnp.abs(nl))))

    rh = tm // SCAN_ROW_PARTS
    for part in range(SCAN_ROW_PARTS):
        xr = conv_part(part)
        xrb = xr.astype(BF16)
        yield
        for hh in range(2):
            cols = slice(hh * half, (hh + 1) * half)
            pre = _dot(xrb[:, cols], wg_ref[0, 0, hh])
            t_r = jnp.tanh(pre[:, 0:half] + ba[:, cols])
            t_i = jnp.tanh(pre[:, half:2 * half] + bx[:, cols])
            log_a = c8h[:, cols] * t_r + c8h[:, cols]
            a = jnp.exp(log_a)
            th = jnp.tanh(log_a)
            nth = -0.5 * th
            sq = jnp.where(nth > 0.0, nth * lax.rsqrt(nth), 0.0) * lax.rsqrt(1.0 - th)
            bq = sq * ((t_i + 1.0) * xr[:, cols])
            for kk in range(half // V7X_LANES):
                k = hh * (half // V7X_LANES) + kk
                lanes = slice(kk * V7X_LANES, (kk + 1) * V7X_LANES)
                scr.a[k, part * rh:(part + 1) * rh, :] = a[:, lanes]
                scr.b[k, part * rh:(part + 1) * rh, :] = bq[:, lanes]
            yield

    steps = range(SCAN_SEG_LEN - 1, -1, -1) if reverse else range(SCAN_SEG_LEN)

    def at_step(t):
        return slice(t * SCAN_SEGS, (t + 1) * SCAN_SEGS)

    for k in range(nblk):
        lanes = slice(k * V7X_LANES, (k + 1) * V7X_LANES)
        e = jnp.zeros((SCAN_SEGS, V7X_LANES), F32)
        p = e + 1.0
        for t in steps:
            at = scr.a[k, at_step(t), :]
            e = at * e + scr.b[k, at_step(t), :]
            p = at * p
        scr.e[:, lanes] = e
        scr.p[:, lanes] = p
    yield

    c = jnp.where(first_step, 0.0, scr.carry[...])
    order = range(SCAN_SEGS - 1, -1, -1) if reverse else range(SCAN_SEGS)
    for j in order:
        scr.c[j:j + 1, :] = c
        c = scr.p[j:j + 1, :] * c + scr.e[j:j + 1, :]
    scr.carry[...] = c

    for k in range(nblk):
        h = scr.c[:, k * V7X_LANES:(k + 1) * V7X_LANES]
        for t in steps:
            h = scr.a[k, at_step(t), :] * h + scr.b[k, at_step(t), :]
            scr.h[k, pl.ds(t, SCAN_SEGS, stride=SCAN_SEG_STRIDE), :] = h


def _advance(stages, n):
    for _ in range(n):
        next(stages, None)


def _scan_out_block(scr, k, j):
    return scr.h[k, j * SCAN_SEG_STRIDE:j * SCAN_SEG_STRIDE + SCAN_SEG_LEN, :]


def _mix_in_scan_kernel(x_ref, m_ref, p_ref, win_ref, ws_ref, bs_ref, wg_ref, wf_ref, wof_ref,
                        ygm_ref, gact_ref, xr_ref, ftx_ref, hf_ref, wb_ref, wob_ref, ext_ref, segp_ref, *scratch,
                        vecs, nt, ntot, d_gm, d_rg):
    scr = _ScanScratch(*scratch)
    g = pl.program_id(0)
    tm = ROW_TILE
    halo = V7X_SUBLANES
    o_v, o_g, o_x, o_f = d_gm, 2 * d_gm, 2 * d_gm + d_rg, 2 * d_gm + 2 * d_rg

    wb_ref[...] = wf_ref[0].astype(BF16)
    wob_ref[...] = wof_ref[0].astype(BF16)

    def step(project, scan_on):
        gs = (g - 1) % nt
        if project:
            x = x_ref[0]
            sh1 = m_ref[0, 0, 0:1, :]
            sc1 = m_ref[0, 0, 1:2, :]
            h = (_rms(x, vecs.get(p_ref, "g_pre_mix") * (1.0 + sc1)) + sh1).astype(BF16)
            rgx = _dot(h, win_ref[0, :, o_x:o_f])

        def conv_part(part):
            nblk = d_rg // V7X_LANES
            seg_rows = SCAN_SEG_LEN + 2 * halo
            if part == 0:
                nxt = (jnp.where(gs == nt - 1, 0.0, rgx[0:halo]) if project
                       else jnp.zeros((halo, d_rg), F32))
                ext_ref[halo + tm:halo + tm + halo, :] = nxt
                for k in range(nblk):
                    for j in range(SCAN_SEGS):
                        segp_ref[k, j * SEG_PITCH:j * SEG_PITCH + seg_rows, :] = (
                            ext_ref[j * SCAN_SEG_LEN:j * SCAN_SEG_LEN + seg_rows,
                                    k * V7X_LANES:(k + 1) * V7X_LANES])
            steps = SCAN_SEG_LEN // SCAN_ROW_PARTS
            cols = []
            for k in range(nblk):
                lanes = slice(k * V7X_LANES, (k + 1) * V7X_LANES)
                taps = [jnp.broadcast_to(vecs.get(p_ref, f"conv_w{i}")[:, lanes], (SCAN_SEGS, V7X_LANES))
                        for i in range(RG_CONV)]
                bias = jnp.broadcast_to(vecs.get(p_ref, "conv_b")[:, lanes], (SCAN_SEGS, V7X_LANES))
                slabs = [segp_ref[k, pl.ds(halo - RG_CONV_LEFT + part * steps + i, SCAN_SEGS, stride=SEG_PITCH), :]
                         for i in range(steps + RG_CONV - 1)]
                out = []
                for t in range(steps):
                    acc = bias
                    for i in range(RG_CONV):
                        acc = acc + slabs[t + i] * taps[i]
                    out.append(acc)
                cols.append(jnp.concatenate(out, axis=0))
            xr = jnp.concatenate(cols, axis=1)
            rh = tm // SCAN_ROW_PARTS
            xr_ref[0, part * rh:(part + 1) * rh, :] = xr
            return xr

        if scan_on:
            scan = _rg_scan_tile(conv_part, gs == 0, wg_ref, vecs.get(p_ref, "b_a0"), vecs.get(p_ref, "b_x0"),
                                 vecs.get(p_ref, "lam0"), scr, reverse=False)
        else:
            scr.carry[...] = jnp.zeros_like(scr.carry)
            scan = iter(())
        if not project:
            for _ in scan:
                pass
        else:
            head_dim = d_gm // GM_HEADS
            head = lax.broadcasted_iota(jnp.int32, (GM_CHUNK, d_gm), 1) // head_dim
            halves = 2
            hrows = tm // halves
            for r in range(halves):
                rows_r = slice(r * hrows, (r + 1) * hrows)
                hr = h[rows_r]
                _advance(scan, 1)
                if r == halves - 1:
                    _advance(scan, max(0, 3 * (SCAN_ROW_PARTS - 1) + 1 - 4))
                    if scan_on:
                        ext_ref[0:halo, :] = jnp.where(g % nt == 0, 0.0, ext_ref[tm:tm + halo, :])
                    else:
                        ext_ref[0:halo, :] = jnp.zeros((halo, d_rg), F32)
                    ext_ref[halo:halo + tm, :] = rgx
                v = _gelu(_dot(hr, win_ref[0, :, o_v:o_g]))
                mu = jnp.mean(v, axis=-1, keepdims=True)
                vc = v - mu
                var = jnp.mean(vc * vc, axis=-1, keepdims=True)
                vn = (vc * lax.rsqrt(var + EPS) * vecs.get(p_ref, "ln_g") + vecs.get(p_ref, "ln_b")).astype(BF16)
                _advance(scan, 1)
                gact_ref[0, rows_r, :] = _gelu(_dot(hr, win_ref[0, :, o_g:o_x])).astype(BF16)
                _advance(scan, 1)
                ftx_ref[0, rows_r, :] = _dot(hr, win_ref[0, :, o_f:o_f + d_gm]).astype(BF16)
                u = _gelu(_dot(hr, win_ref[0, :, 0:o_v]))
                for c in range(hrows // GM_CHUNK):
                    rows = slice(c * GM_CHUNK, (c + 1) * GM_CHUNK)
                    rr = _dot(ws_ref[0], vn[rows])
                    s = rr[0:GM_CHUNK]
                    for hh in range(1, GM_HEADS):
                        s = jnp.where(head == hh, rr[hh * GM_CHUNK:(hh + 1) * GM_CHUNK], s)
                    y = u[rows] * (s + bs_ref[0])
                    ygm_ref[0, r * hrows + c * GM_CHUNK:r * hrows + (c + 1) * GM_CHUNK, :] = (
                        _rms(y, vecs.get(p_ref, "g_mix_out", 0, d_gm)).astype(BF16))
            for _ in scan:
                pass
        if scan_on:
            for k in range(d_rg // V7X_LANES):
                for j in range(SCAN_SEGS):
                    hf_ref[0, j * SCAN_SEG_LEN:(j + 1) * SCAN_SEG_LEN,
                           k * V7X_LANES:(k + 1) * V7X_LANES] = _scan_out_block(scr, k, j)

    pl.when(g == 0)(functools.partial(step, True, False))
    pl.when((g > 0) & (g < ntot))(functools.partial(step, True, True))
    pl.when(g == ntot)(functools.partial(step, False, True))


def _mix_in_scan(l, x, mod4, vecs, packed, w_in, ws, bs, wg, w_ff2, w_out, *, d_gm, d_rg):
    bsz, seq, d = x.shape
    d_in = w_in.shape[-1]
    nt = seq // ROW_TILE
    ntot = bsz * nt
    d_ff = w_ff2.shape[1]
    wrows = d_ff // ntot
    orows = w_out.shape[1] // ntot
    wblk = lambda g: jnp.minimum(g, ntot - 1)

    def proj_bt(g):
        gp = jnp.minimum(g, ntot - 1)
        return gp // nt, gp % nt

    def scan_bt(g):
        gs = jnp.maximum(g - 1, 0)
        return gs // nt, gs % nt

    lay = lambda g: (l, 0, 0)
    row = lambda g: (*proj_bt(g), 0)
    return pl.pallas_call(
        functools.partial(_mix_in_scan_kernel, vecs=vecs, nt=nt, ntot=ntot, d_gm=d_gm, d_rg=d_rg),
        grid=(ntot + 1,),
        in_specs=[
            pl.BlockSpec((1, ROW_TILE, d), row),
            pl.BlockSpec((1, 1, 6, d), lambda g: (l, proj_bt(g)[0], 0, 0)),
            vecs.spec(l),
            _resident((1, d, d_in), lay),
            pl.BlockSpec((1, GM_HEADS * GM_CHUNK, GM_CHUNK), lay),
            pl.BlockSpec((1, GM_CHUNK, d_gm), lay),
            _resident((1, 1, 2, d_rg // 2, d_rg), lambda g: (l, 0, 0, 0, 0)),
            pl.BlockSpec((1, wrows, d), lambda g: (l, wblk(g), 0)),
            pl.BlockSpec((1, orows, d), lambda g: (l, wblk(g), 0)),
        ],
        out_specs=[
            pl.BlockSpec((1, ROW_TILE, d_gm), row),
            pl.BlockSpec((1, ROW_TILE, d_rg), row),
            pl.BlockSpec((1, ROW_TILE, d_rg), lambda g: (*scan_bt(g), 0)),
            pl.BlockSpec((1, ROW_TILE, d_gm), row),
            pl.BlockSpec((1, ROW_TILE, d_rg), lambda g: (*scan_bt(g), 0)),
            pl.BlockSpec((wrows, d), lambda g: (wblk(g), 0)),
            pl.BlockSpec((orows, d), lambda g: (wblk(g), 0)),
        ],
        out_shape=[
            jax.ShapeDtypeStruct((bsz, seq, d_gm), BF16),
            jax.ShapeDtypeStruct((bsz, seq, d_rg), BF16),
            jax.ShapeDtypeStruct((bsz, seq, d_rg), F32),
            jax.ShapeDtypeStruct((bsz, seq, d_gm), BF16),
            jax.ShapeDtypeStruct((bsz, seq, d_rg), F32),
            jax.ShapeDtypeStruct((d_ff, d), BF16),
            jax.ShapeDtypeStruct(w_out.shape[1:], BF16),
        ],
        scratch_shapes=[
            pltpu.VMEM((ROW_TILE + 2 * V7X_SUBLANES, d_rg), F32),
            pltpu.VMEM((d_rg // V7X_LANES, SCAN_SEGS * SEG_PITCH, V7X_LANES), F32),
        ] + _scan_scratch_shapes(d_rg),
        compiler_params=_params("arbitrary"),
        name="mix_in_scan",
    )(x, mod4, packed, w_in, ws, bs, wg, w_ff2, w_out)


def _mix_out_ffn_kernel(xr_ref, wg_ref, hf_ref, x_ref, ygm_ref, gact_ref, yft_ref, m_ref, p_ref, ftw_ref,
                        wout_ref, w1_ref, w2_ref,
                        o_ref, *scratch, vecs, nt, ntot, ff_chunk, ft_groups):
    scr = _ScanScratch(*scratch)
    g = pl.program_id(0)
    d_rg = xr_ref.shape[-1]
    d_gm = ygm_ref.shape[-1]
    o_ft = d_gm + d_rg
    d_mix = wout_ref.shape[0]
    gs = jnp.minimum(g, ntot - 1) % nt

    def conv_part(part):
        rh = ROW_TILE // SCAN_ROW_PARTS
        return xr_ref[0, part * rh:(part + 1) * rh, :]

    def scan_stages():
        return _rg_scan_tile(conv_part, gs == 0, wg_ref, vecs.get(p_ref, "b_a1"), vecs.get(p_ref, "b_x1"),
                             vecs.get(p_ref, "lam1"), scr, reverse=True)

    @pl.when(g == 0)
    def _():
        scr.carry[...] = jnp.zeros_like(scr.carry)
        for _ in scan_stages():
            pass

    @pl.when(g > 0)
    def _():
        h_bwd = jnp.concatenate(
            [jnp.concatenate([_scan_out_block(scr, k, j) for j in range(SCAN_SEGS)], axis=0)
             for k in range(d_rg // V7X_LANES)], axis=1)

        gt1 = m_ref[0, 0, 2:3, :]
        yrg = _rms((hf_ref[0] + h_bwd) * gact_ref[0].astype(F32), vecs.get(p_ref, "g_mix_out", d_gm, o_ft))
        ft_pitch = yft_ref.shape[1] // ft_groups
        ft_rows = ROW_TILE // ft_groups
        yft_raw = jnp.concatenate(
            [yft_ref[0, a * ft_pitch:a * ft_pitch + ft_rows, :] for a in range(ft_groups)], axis=0)
        yft = _dot(yft_raw.astype(BF16), ftw_ref[0]) + vecs.get(p_ref, "ft_b")
        yft = _rms(yft, vecs.get(p_ref, "g_mix_out", o_ft, d_mix))
        o = _dot(ygm_ref[0], wout_ref[0:d_gm, :])
        o = o + _dot(yrg.astype(BF16), wout_ref[d_gm:o_ft, :])
        o = o + _dot(yft.astype(BF16), wout_ref[o_ft:d_mix, :])
        x = x_ref[0] + _rms(o, gt1 * vecs.get(p_ref, "g_post_mix"))

        sh2 = m_ref[0, 0, 3:4, :]
        sc2 = m_ref[0, 0, 4:5, :]
        gt2 = m_ref[0, 0, 5:6, :]
        h = (_rms(x, vecs.get(p_ref, "g_pre_ff") * (1.0 + sc2)) + sh2).astype(BF16)

        scan = scan_stages()
        next(scan)
        d_ff = w1_ref.shape[-1]
        acc = None
        for c in range(d_ff // ff_chunk):
            cols = slice(c * ff_chunk, (c + 1) * ff_chunk)
            a = jnp.maximum(_dot(h, w1_ref[:, cols]), 0.0)
            part = _dot((a * a).astype(BF16), w2_ref[cols, :])
            acc = part if acc is None else acc + part
            _advance(scan, pl.cdiv(SCAN_STAGES - 1, d_ff // ff_chunk))
        for _ in scan:
            pass
        o_ref[0] = x + _rms(acc, gt2 * vecs.get(p_ref, "g_post_ff"))


def _mix_out_ffn(l, xr, wg, h_fwd, x, ygm, gact, yft, mod4, vecs, packed, ftw, w_out, w1, w2):
    bsz, seq, d = x.shape
    d_rg = xr.shape[-1]
    d_gm = ygm.shape[-1]
    d_ft = yft.shape[-1]
    d_mix = w_out.shape[0]
    d_ff = w1.shape[-1]
    nt = seq // ROW_TILE
    ntot = bsz * nt

    def scan_bt(g):
        gs = jnp.minimum(g, ntot - 1)
        return gs // nt, nt - 1 - gs % nt

    def tail_bt(g):
        gf = jnp.maximum(g - 1, 0)
        return gf // nt, nt - 1 - gf % nt

    lay = lambda g: (l, 0, 0)
    row = lambda g: (*tail_bt(g), 0)
    in_specs = [
        pl.BlockSpec((1, ROW_TILE, d_rg), lambda g: (*scan_bt(g), 0)),
        _resident((1, 1, 2, d_rg // 2, d_rg), lambda g: (l, 1, 0, 0, 0)),
        pl.BlockSpec((1, ROW_TILE, d_rg), row),
        pl.BlockSpec((1, ROW_TILE, d), row),
        pl.BlockSpec((1, ROW_TILE, d_gm), row),
        pl.BlockSpec((1, ROW_TILE, d_rg), row),
        pl.BlockSpec((1, yft.shape[1] // (seq // ROW_TILE), d_ft), row),
        pl.BlockSpec((1, 1, 6, d), lambda g: (l, tail_bt(g)[0], 0, 0)),
        vecs.spec(l),
        _resident((1, d_ft, d_ft), lay),
        _resident((d_mix, d), lambda g: (0, 0)),
        _resident((d, d_ff), lambda g: (0, 0)),
        _resident((d_ff, d), lambda g: (0, 0)),
    ]
    return pl.pallas_call(
        functools.partial(_mix_out_ffn_kernel, vecs=vecs, nt=nt, ntot=ntot, ff_chunk=d_ff // FF_CHUNKS,
                          ft_groups=ROW_TILE // (seq // FT_N1)),
        grid=(ntot + 1,),
        in_specs=in_specs,
        out_specs=pl.BlockSpec((1, ROW_TILE, d), row),
        out_shape=jax.ShapeDtypeStruct((bsz, seq, d), F32),
        scratch_shapes=_scan_scratch_shapes(d_rg),
        compiler_params=_params("arbitrary"),
        name="mix_out_ffn",
    )(xr, wg, h_fwd, x, ygm, gact, yft, mod4, packed, ftw, w_out, w1, w2)


def _block_diag(w, per):
    *lead, n, hd, _ = w.shape
    w = w.reshape(*lead, n // per, per, hd, hd)
    keep = [(0, 0)] * (len(lead) + 2)
    rows = [jnp.pad(w[..., a, :, :], keep + [(a * hd, (per - 1 - a) * hd)]) for a in range(per)]
    return jnp.concatenate(rows, axis=-2)


def kernel(x, c, w_ada, b_ada, g_pre_mix, g_post_mix, w_in, gm_ln_g, gm_ln_b, gm_w_s, gm_b_s,
           rg_conv_w, rg_conv_b, rg_w_a, rg_b_a, rg_w_x, rg_b_x, rg_lam, ft_w, ft_b,
           g_mix_out, w_out, g_pre_ff, g_post_ff, w_ff1, w_ff2):
    bsz, seq, d = x.shape
    depth = w_in.shape[0]
    d_gm = gm_ln_g.shape[-1]
    d_rg = rg_conv_b.shape[-1]
    d_ft = ft_w.shape[1] * ft_w.shape[2]
    assert seq % ROW_TILE == 0 and ROW_TILE % GM_CHUNK == 0 and seq % FT_N1 == 0
    assert d_ft % FT_LANES == 0 and ft_w.shape[2] == FT_GROUP_DIM
    assert rg_conv_w.shape[1] == RG_CONV and rg_w_a.shape[2] == RG_HEADS

    ws = gm_w_s.reshape(depth, GM_HEADS * GM_CHUNK, GM_CHUNK).astype(BF16)
    bs = jnp.repeat(jnp.swapaxes(gm_b_s, 1, 2), d_gm // GM_HEADS, axis=2)
    per = RG_HEADS // 2
    wg = (0.5 * jnp.concatenate([_block_diag(rg_w_a, per), _block_diag(rg_w_x, per)], axis=-1)).astype(BF16)
    ftw = _block_diag(ft_w, FT_GROUPS).reshape(depth, d_ft, d_ft).astype(BF16)
    tables = _fourier_tables(seq)

    flat = lambda a: a.reshape(depth, -1)
    named = [("g_pre_mix", g_pre_mix), ("g_post_mix", g_post_mix), ("g_pre_ff", g_pre_ff),
             ("g_post_ff", g_post_ff), ("g_mix_out", g_mix_out), ("ln_g", gm_ln_g), ("ln_b", gm_ln_b),
             ("conv_b", rg_conv_b), ("ft_b", flat(ft_b))]
    for dirn in range(2):
        named += [(f"b_a{dirn}", 0.5 * flat(rg_b_a[:, dirn])), (f"b_x{dirn}", 0.5 * flat(rg_b_x[:, dirn])),
                  (f"lam{dirn}", flat(rg_lam[:, dirn]))]
    named += [(f"conv_w{i}", rg_conv_w[:, i]) for i in range(RG_CONV)]
    vecs = _LayerVecs(named)
    packed = vecs.pack(named)

    mod, w_in_b = _modulation(c, w_ada, b_ada, w_in)
    mod4 = mod.reshape(depth, bsz, 6, d)

    for l in range(depth):
        ygm, gact, xr, ftx, h_fwd, w2_b, w_out_b = _mix_in_scan(l, x, mod4, vecs, packed, w_in_b, ws, bs, wg,
                                                                w_ff2, w_out, d_gm=d_gm, d_rg=d_rg)
        yft, w1_b = _fourier(l, ftx, tables, w_ff1)
        x = _mix_out_ffn(l, xr, wg, h_fwd, x, ygm, gact, yft, mod4, vecs, packed, ftw, w_out_b, w1_b, w2_b)
    return x
```
